```python
import math
import jax, jax.numpy as jnp
from jax import lax
import numpy as np

D_MODEL = 1024
BATCH = 8
SEQ = 2048
DEPTH = 2
DEC_BATCH = 128
DEC_SEQ = 8
PAST_LEN = 2048
PAGE_SIZE = 128

CONV_DIM = 384
CONV_W = 3
SG_DIM = 256
SG_GROUPS = 4
SG_HEAD = SG_DIM // SG_GROUPS
CHUNK = 128
ATT_GROUPS = ((128, 1), (512, 4), (2048, 16))
HPG = 4
HEAD_DIM = 32
N_HEADS_C = HPG * len(ATT_GROUPS)
ATT_DIM = N_HEADS_C * HEAD_DIM
ATT_OUT = HPG * HEAD_DIM
Q_BLOCK = 128
NUM_BUCKETS = 32
MAX_DISTANCE = 2048
N_EXPERTS = 32
TOP_K = 4
D_FF = D_MODEL
SWIGLU_LIMIT = 7.0
SWIGLU_ALPHA = 1.702
DN_ALPHA = (2 * DEPTH) ** 0.25
DN_BETA = (8 * DEPTH) ** -0.25
LN_EPS = 1e-5
OFF_A = 0
OFF_B = OFF_A + 3 * CONV_DIM
OFF_C = OFF_B + 2 * SG_DIM
OFF_G = OFF_C + 3 * ATT_DIM
IN_DIM = OFF_G + 3 * D_MODEL

kernel_name = "hybrid_conv_gmlp_dilated_moe_decoder_step"


def layer_norm(x, g, b):
    xf = x.astype(jnp.float32)
    mu = jnp.mean(xf, axis=-1, keepdims=True)
    var = jnp.mean(jnp.square(xf - mu), axis=-1, keepdims=True)
    return ((xf - mu) * lax.rsqrt(var + LN_EPS) * g.astype(jnp.float32) + b.astype(jnp.float32)).astype(x.dtype)


def t5_bucket(dist):
    max_exact = NUM_BUCKETS // 2
    distf = jnp.maximum(dist, 1).astype(jnp.float32)
    large = max_exact + (jnp.log(distf / max_exact) / math.log(MAX_DISTANCE / max_exact)
                         * (NUM_BUCKETS - max_exact)).astype(jnp.int32)
    large = jnp.minimum(large, NUM_BUCKETS - 1)
    return jnp.where(dist < max_exact, dist, large)


def short_conv_mixer(xa, conv_buf, conv_w):
    bg, cg, h = jnp.split(xa, 3, axis=-1)
    z = cg * h
    zp = jnp.concatenate([conv_buf.astype(z.dtype), z], axis=1)
    T = z.shape[1]
    y = conv_w[0] * zp[:, 0:T]
    for k in range(1, CONV_W):
        y = y + conv_w[k] * zp[:, k:k + T]
    return bg * y, zp[:, -(CONV_W - 1):]


def spatial_gating_mixer(xb, ln_g, ln_b, w_sp, b_sp, n_chunks, chunk_len):
    uv = jax.nn.gelu(xb, approximate=False)
    u, v = jnp.split(uv, 2, axis=-1)
    v = layer_norm(v, ln_g, ln_b)
    bsz = v.shape[0]
    vc = v.reshape(bsz, n_chunks, chunk_len, SG_GROUPS, SG_HEAD)
    mask = jnp.tril(jnp.ones((chunk_len, chunk_len), dtype=bool))
    w = jnp.where(mask[None], w_sp[:, :chunk_len, :chunk_len], 0)
    s = jnp.einsum('gij,bcjgd->bcigd', w, vc) + b_sp[:, :chunk_len].T[None, None, :, :, None]
    return u * s.reshape(bsz, n_chunks * chunk_len, SG_DIM), v


def _attend(qb, pos, k_ext, v_ext, dil, n_keys, bias):
    idx = pos[:, None] - dil * jnp.arange(n_keys, dtype=jnp.int32)[None, :]
    valid = idx >= 0
    idx = jnp.maximum(idx, 0)
    kg = k_ext[:, idx].astype(jnp.float32)
    vg = v_ext[:, idx].astype(jnp.float32)
    s = jnp.einsum('bthd,btjhd->bthj', qb.astype(jnp.float32), kg) * (HEAD_DIM ** -0.5) + bias[None, None]
    s = jnp.where(valid[None, :, None, :], s, -1e30)
    m = jnp.max(s, axis=-1, keepdims=True)
    p = jnp.exp(s - m)
    den = jnp.sum(p, axis=-1, keepdims=True)
    o = jnp.einsum('bthj,btjhd->bthd', p, vg) / den
    lse = (m + jnp.log(den))[..., 0]
    return o, lse


def dilated_attention(q, k_new, v_new, k_bufs, v_bufs, rel_bias, blocked):
    bsz, T = q.shape[0], q.shape[1]
    outs, lses, k_keep, v_keep = [], [], [], []
    for g, (win, dil) in enumerate(ATT_GROUPS):
        hs = slice(g * HPG, (g + 1) * HPG)
        n_keys = win // dil + 1
        bias = rel_bias[t5_bucket(dil * jnp.arange(n_keys, dtype=jnp.int32))][:, hs].T.astype(jnp.float32)
        kg_new, vg_new = k_new[:, :, hs], v_new[:, :, hs]
        k_ext = jnp.concatenate([k_bufs[g].astype(kg_new.dtype), kg_new], axis=1)
        v_ext = jnp.concatenate([v_bufs[g].astype(vg_new.dtype), vg_new], axis=1)
        past = k_bufs[g].shape[1]
        qg = q[:, :, hs]
        if blocked:
            nb = T // Q_BLOCK
            qblocks = qg.reshape(bsz, nb, Q_BLOCK, HPG, HEAD_DIM).transpose(1, 0, 2, 3, 4)

            def body(args, k_ext=k_ext, v_ext=v_ext, dil=dil, n_keys=n_keys, bias=bias, past=past):
                qb, i = args
                pos = past + i * Q_BLOCK + jnp.arange(Q_BLOCK, dtype=jnp.int32)
                return _attend(qb, pos, k_ext, v_ext, dil, n_keys, bias)

            o, lse = lax.map(body, (qblocks, jnp.arange(nb, dtype=jnp.int32)))
            o = o.transpose(1, 0, 2, 3, 4).reshape(bsz, T, HPG, HEAD_DIM)
            lse = lse.transpose(1, 0, 2, 3).reshape(bsz, T, HPG)
        else:
            o, lse = _attend(qg, past + jnp.arange(T, dtype=jnp.int32), k_ext, v_ext, dil, n_keys, bias)
        outs.append(o)
        lses.append(lse)
        keep = min(win, T)
        k_keep.append(kg_new[:, T - keep:])
        v_keep.append(vg_new[:, T - keep:])
    wts = jax.nn.softmax(jnp.stack(lses, axis=0), axis=0)
    o = jnp.sum(wts[..., None] * jnp.stack(outs, axis=0), axis=0)
    return o.reshape(bsz, T, ATT_OUT).astype(q.dtype), k_keep, v_keep


def moe_ffn(x, w_router, b_router, w_gate, b_gate, w_up, b_up, w_down, b_down):
    shp = x.shape
    xt = x.reshape(-1, D_MODEL)
    logits = (xt @ w_router + b_router).astype(jnp.float32)
    top_val, top_idx = lax.top_k(logits, TOP_K)
    top_w = jax.nn.softmax(top_val, axis=-1)
    combine = jnp.sum(jax.nn.one_hot(top_idx, N_EXPERTS, dtype=jnp.float32) * top_w[..., None], axis=1)
    y = jnp.zeros((xt.shape[0], D_MODEL), jnp.float32)
    for e in range(N_EXPERTS):
        gl = jnp.minimum(xt @ w_gate[e] + b_gate[e], SWIGLU_LIMIT)
        ul = jnp.clip(xt @ w_up[e] + b_up[e], -SWIGLU_LIMIT, SWIGLU_LIMIT)
        hid = gl * jax.nn.sigmoid(SWIGLU_ALPHA * gl) * (ul + 1)
        y = y + combine[:, e:e + 1] * (hid @ w_down[e] + b_down[e]).astype(jnp.float32)
    return y.astype(x.dtype).reshape(shp)


def decoder_layer(x, conv_buf, k_bufs, v_bufs, blocked, w_in, conv_w, sg_ln_g, sg_ln_b, w_sp, b_sp, rel_bias,
                  w_up_a, w_up_b, w_up_c, w_o, ln1_g, ln1_b, w_router, b_router, w_gate, b_gate, w_up, b_up,
                  w_down, b_down, ln2_g, ln2_b):
    bsz, T = x.shape[0], x.shape[1]
    p = x @ w_in
    xa, xb, xc = p[..., OFF_A:OFF_B], p[..., OFF_B:OFF_C], p[..., OFF_C:OFF_G]
    ga, gb, gc = jnp.split(jax.nn.sigmoid(p[..., OFF_G:]), 3, axis=-1)
    ya, conv_new = short_conv_mixer(xa, conv_buf, conv_w)
    n_chunks, chunk_len = (T // CHUNK, CHUNK) if blocked else (1, T)
    yb, v_rows = spatial_gating_mixer(xb, sg_ln_g, sg_ln_b, w_sp, b_sp, n_chunks, chunk_len)
    q, k, v = [t.reshape(bsz, T, N_HEADS_C, HEAD_DIM) for t in jnp.split(xc, 3, axis=-1)]
    yc, k_keep, v_keep = dilated_attention(q, k, v, k_bufs, v_bufs, rel_bias, blocked)
    merged = ga * (ya @ w_up_a) + gb * (yb @ w_up_b) + gc * (yc @ w_up_c)
    x = layer_norm(DN_ALPHA * x + merged @ w_o, ln1_g, ln1_b)
    x = layer_norm(DN_ALPHA * x + moe_ffn(x, w_router, b_router, w_gate, b_gate, w_up, b_up, w_down, b_down), ln2_g, ln2_b)
    return x, conv_new, v_rows, k_keep, v_keep


def setup_inputs(seed: int = 0) -> dict:
    key = jax.random.key(seed)
    ks = iter(jax.random.split(key, 40))
    f32 = jnp.float32

    def nrm(shape, scale=1.0):
        return jax.random.normal(next(ks), shape, f32) * scale

    def cache_shape(win):
        return (DEPTH, DEC_BATCH, min(win, PAST_LEN), HPG, HEAD_DIM)

    return {
        "x_prompt": nrm((BATCH, SEQ, D_MODEL)),
        "x_sample": nrm((DEC_BATCH, DEC_SEQ, D_MODEL)),
        "state_conv": nrm((DEPTH, DEC_BATCH, CONV_W - 1, CONV_DIM)),
        "cache_k_w128": nrm(cache_shape(128)),
        "cache_v_w128": nrm(cache_shape(128)),
        "cache_k_w512": nrm(cache_shape(512)),
        "cache_v_w512": nrm(cache_shape(512)),
        "cache_k_w2048": nrm(cache_shape(2048)),
        "cache_v_w2048": nrm(cache_shape(2048)),
        "w_in": nrm((DEPTH, D_MODEL, IN_DIM), D_MODEL ** -0.5),
        "conv_w": nrm((DEPTH, CONV_W, CONV_DIM), CONV_W ** -0.5),
        "sg_ln_g": 1.0 + nrm((DEPTH, SG_DIM), 0.1),
        "sg_ln_b": nrm((DEPTH, SG_DIM), 0.1),
        "w_sp": nrm((DEPTH, SG_GROUPS, CHUNK, CHUNK), CHUNK ** -0.5),
        "b_sp": 1.0 + nrm((DEPTH, SG_GROUPS, CHUNK), 0.1),
        "rel_bias": nrm((NUM_BUCKETS, N_HEADS_C), 0.5),
        "w_up_a": nrm((DEPTH, CONV_DIM, D_MODEL), CONV_DIM ** -0.5),
        "w_up_b": nrm((DEPTH, SG_DIM, D_MODEL), SG_DIM ** -0.5),
        "w_up_c": nrm((DEPTH, ATT_OUT, D_MODEL), ATT_OUT ** -0.5),
        "w_o": nrm((DEPTH, D_MODEL, D_MODEL), D_MODEL ** -0.5 * DN_BETA),
        "ln1_g": 1.0 + nrm((DEPTH, D_MODEL), 0.1),
        "ln1_b": nrm((DEPTH, D_MODEL), 0.1),
        "w_router": nrm((DEPTH, D_MODEL, N_EXPERTS), D_MODEL ** -0.5),
        "b_router": nrm((DEPTH, N_EXPERTS), 0.01),
        "w_gate": nrm((DEPTH, N_EXPERTS, D_MODEL, D_FF), D_MODEL ** -0.5),
        "b_gate": nrm((DEPTH, N_EXPERTS, D_FF), 0.02),
        "w_up": nrm((DEPTH, N_EXPERTS, D_MODEL, D_FF), D_MODEL ** -0.5),
        "b_up": nrm((DEPTH, N_EXPERTS, D_FF), 0.02),
        "w_down": nrm((DEPTH, N_EXPERTS, D_FF, D_MODEL), D_FF ** -0.5 * DN_BETA),
        "b_down": nrm((DEPTH, N_EXPERTS, D_MODEL), 0.02),
        "ln2_g": 1.0 + nrm((DEPTH, D_MODEL), 0.1),
        "ln2_b": nrm((DEPTH, D_MODEL), 0.1),
    }


def reference(x_prompt, x_sample, state_conv, cache_k_w128, cache_v_w128, cache_k_w512, cache_v_w512,
              cache_k_w2048, cache_v_w2048, w_in, conv_w, sg_ln_g, sg_ln_b, w_sp, b_sp, rel_bias,
              w_up_a, w_up_b, w_up_c, w_o, ln1_g, ln1_b, w_router, b_router, w_gate, b_gate, w_up, b_up,
              w_down, b_down, ln2_g, ln2_b):
    k_caches = (cache_k_w128, cache_k_w512, cache_k_w2048)
    v_caches = (cache_v_w128, cache_v_w512, cache_v_w2048)
    n_g = len(ATT_GROUPS)
    xp, xs = x_prompt, x_sample
    conv_p, conv_s, sgv_s = [], [], []
    kp, vp, ksm, vsm = [[] for _ in range(n_g)], [[] for _ in range(n_g)], [[] for _ in range(n_g)], [[] for _ in range(n_g)]
    for l in range(DEPTH):
        lw = (w_in[l], conv_w[l], sg_ln_g[l], sg_ln_b[l], w_sp[l], b_sp[l], rel_bias, w_up_a[l], w_up_b[l],
              w_up_c[l], w_o[l], ln1_g[l], ln1_b[l], w_router[l], b_router[l], w_gate[l], b_gate[l], w_up[l],
              b_up[l], w_down[l], b_down[l], ln2_g[l], ln2_b[l])
        empty_kv = jnp.zeros((BATCH, 0, HPG, HEAD_DIM), xp.dtype)
        zero_conv = jnp.zeros((BATCH, CONV_W - 1, CONV_DIM), xp.dtype)
        xp, cb_p, _, kk, vv = decoder_layer(xp, zero_conv, [empty_kv] * n_g, [empty_kv] * n_g, True, *lw)
        conv_p.append(cb_p)
        for g in range(n_g):
            kp[g].append(kk[g])
            vp[g].append(vv[g])
        xs, cb_s, v_rows, kk, vv = decoder_layer(xs, state_conv[l], [c[l] for c in k_caches],
                                                 [c[l] for c in v_caches], False, *lw)
        conv_s.append(cb_s)
        sgv_s.append(v_rows)
        for g in range(n_g):
            ksm[g].append(kk[g])
            vsm[g].append(vv[g])
    y_prompt, y_sample = xp, xs
    state_conv_prompt = jnp.stack(conv_p, axis=0)
    state_conv_sample = jnp.stack(conv_s, axis=0)
    cache_k_w128_prompt, cache_k_w512_prompt, cache_k_w2048_prompt = [jnp.stack(t, axis=0) for t in kp]
    cache_v_w128_prompt, cache_v_w512_prompt, cache_v_w2048_prompt = [jnp.stack(t, axis=0) for t in vp]
    cache_k_w128_sample, cache_k_w512_sample, cache_k_w2048_sample = [jnp.stack(t, axis=0) for t in ksm]
    cache_v_w128_sample, cache_v_w512_sample, cache_v_w2048_sample = [jnp.stack(t, axis=0) for t in vsm]
    state_sg_v_sample = jnp.stack(sgv_s, axis=0)
    return (y_prompt, y_sample,
            state_conv_prompt, state_conv_sample,
            cache_k_w128_prompt, cache_v_w128_prompt, cache_k_w512_prompt, cache_v_w512_prompt,
            cache_k_w2048_prompt, cache_v_w2048_prompt,
            cache_k_w128_sample, cache_v_w128_sample, cache_k_w512_sample, cache_v_w512_sample,
            cache_k_w2048_sample, cache_v_w2048_sample,
            state_sg_v_sample)
```

```python
import functools
import math

import jax
import jax.numpy as jnp
from jax import lax
from jax.experimental import pallas as pl
from jax.experimental.pallas import tpu as pltpu

D_MODEL = 1024
CONV_DIM = 384
CONV_W = 3
SG_DIM = 256
SG_GROUPS = 4
SG_HEAD = SG_DIM // SG_GROUPS
CHUNK = 128
ATT_GROUPS = ((128, 1), (512, 4), (2048, 16))
N_ATT = len(ATT_GROUPS)
HPG = 4
HEAD_DIM = 32
ATT_DIM = HPG * N_ATT * HEAD_DIM
ATT_OUT = HPG * HEAD_DIM
N_KEYS = 129
NUM_BUCKETS = 32
MAX_DISTANCE = 2048
N_EXPERTS = 32
TOP_K = 4
SWIGLU_LIMIT = 7.0
SWIGLU_ALPHA = 1.702
LN_EPS = 1e-5
OFF_A = 0
OFF_B = OFF_A + 3 * CONV_DIM
OFF_C = OFF_B + 2 * SG_DIM
OFF_G = OFF_C + 3 * ATT_DIM
IN_DIM = OFF_G + 3 * D_MODEL
NEG = -1e30

LANES = 128
SUBLANES = 8
LANE_CHUNKS = D_MODEL // LANES
VMEM_LIMIT = 56 * 1024 * 1024

TM = 256
TM_E = 256
Q_BLK = 128
SEQ_BLK = 4

F32 = jnp.float32
BF16 = jnp.bfloat16


def _dot(a, b):
    return jnp.dot(a.astype(BF16), b.astype(BF16), preferred_element_type=F32)


def _dot_nt(a, b):
    return lax.dot_general(a.astype(BF16), b.astype(BF16), (((1,), (1,)), ((), ())),
                           preferred_element_type=F32)


def _layer_norm(x, g, b):
    mu = jnp.mean(x, axis=-1, keepdims=True)
    xc = x - mu
    var = jnp.mean(xc * xc, axis=-1, keepdims=True)
    return xc * lax.rsqrt(var + LN_EPS) * g + b


def _gelu(x):
    return 0.5 * x * (1.0 + lax.erf(x * (2.0 ** -0.5)))


def _sigmoid(x):
    return 1.0 / (1.0 + jnp.exp(-x))


def _params(n_axes):
    return pltpu.CompilerParams(dimension_semantics=("arbitrary",) * n_axes,
                                vmem_limit_bytes=VMEM_LIMIT)


def _inproj_kernel(n_ptiles, tiles_per_seq, dec_seq,
                   x_ref, w_in_ref, convw_ref, lng_ref, lnb_ref, wsp_ref, bsp_ref, wua_ref, wub_ref,
                   s1_ref, s2_ref,
                   q_ref, k_ref, v_ref, mab_ref, gc_ref, tail_ref, zs_ref, vrows_ref, carry_ref):
    i = pl.program_id(0)
    is_prompt = i < n_ptiles

    @pl.when(jnp.logical_and(is_prompt, i % tiles_per_seq == 0))
    def _():
        carry_ref[...] = jnp.zeros_like(carry_ref)

    x = x_ref[...].astype(BF16)

    pa = jnp.dot(x, w_in_ref[:, OFF_A:OFF_B], preferred_element_type=F32)
    bg, cg, h = pa[:, :CONV_DIM], pa[:, CONV_DIM:2 * CONV_DIM], pa[:, 2 * CONV_DIM:]
    z = cg * h
    row = lax.broadcasted_iota(jnp.int32, (TM, CONV_DIM), 0)
    pos = jnp.where(is_prompt, row, row & (dec_seq - 1))
    prev1 = jnp.where(is_prompt, jnp.broadcast_to(carry_ref[7:8, :], (TM, CONV_DIM)), s1_ref[...])
    prev2 = jnp.where(is_prompt,
                      jnp.where(row == 0, jnp.broadcast_to(carry_ref[6:7, :], (TM, CONV_DIM)),
                                jnp.broadcast_to(carry_ref[7:8, :], (TM, CONV_DIM))),
                      s2_ref[...])
    z1 = jnp.where(pos >= 1, pltpu.roll(z, 1, axis=0), prev1)
    z2 = jnp.where(pos >= 2, pltpu.roll(z, 2, axis=0), prev2)
    cw = convw_ref[...]
    ya = bg * (cw[0:1, :] * z2 + cw[1:2, :] * z1 + cw[2:3, :] * z)

    @pl.when(is_prompt)
    def _():
        carry_ref[...] = z[TM - SUBLANES:, :]
        tail_ref[0] = z[TM - SUBLANES:, :]

    @pl.when(jnp.logical_not(is_prompt))
    def _():
        zs_ref[...] = z

    pb = jnp.dot(x, w_in_ref[:, OFF_B:OFF_C], preferred_element_type=F32)
    uv = _gelu(pb)
    u = uv[:, :SG_DIM]
    vn = _layer_norm(uv[:, SG_DIM:], lng_ref[...], lnb_ref[...])

    @pl.when(jnp.logical_not(is_prompt))
    def _():
        vrows_ref[...] = vn

    vb = vn.astype(BF16)
    low_half = lax.broadcasted_iota(jnp.int32, (CHUNK, LANES), 1) < SG_HEAD
    s_rows = []
    for c in range(TM // CHUNK):
        s_cols = []
        for p in range(SG_DIM // LANES):
            vcp = vb[c * CHUNK:(c + 1) * CHUNK, p * LANES:(p + 1) * LANES]
            sa = jnp.dot(wsp_ref[2 * p], vcp, preferred_element_type=F32)
            sb = jnp.dot(wsp_ref[2 * p + 1], vcp, preferred_element_type=F32)
            s_cols.append(jnp.where(low_half, sa, sb))
        s_rows.append(jnp.concatenate(s_cols, axis=1) + bsp_ref[...])
    yb = u * jnp.concatenate(s_rows, axis=0)

    pc = jnp.dot(x, w_in_ref[:, OFF_C:OFF_G], preferred_element_type=F32)
    q_ref[...] = pc[:, :ATT_DIM]
    k_ref[...] = pc[:, ATT_DIM:2 * ATT_DIM]
    v_ref[...] = pc[:, 2 * ATT_DIM:]

    ga = _sigmoid(jnp.dot(x, w_in_ref[:, OFF_G:OFF_G + D_MODEL], preferred_element_type=F32))
    mab = ga * _dot(ya, wua_ref[...])
    gb = _sigmoid(jnp.dot(x, w_in_ref[:, OFF_G + D_MODEL:OFF_G + 2 * D_MODEL], preferred_element_type=F32))
    mab_ref[...] = mab + gb * _dot(yb, wub_ref[...])
    gc_ref[...] = _sigmoid(jnp.dot(x, w_in_ref[:, OFF_G + 2 * D_MODEL:], preferred_element_type=F32))


def _inproj_call(l, x, w_in, conv_w, sg_ln_g, sg_ln_b, wsp2, bsp2, w_up_a, w_up_b, s1, s2,
                 n_prompt, seq, n_sample, dec_seq):
    n = n_prompt + n_sample
    n_ptiles = n_prompt // TM
    n_stiles = n_sample // TM
    tiles_per_seq = seq // TM
    batch = n_prompt // seq

    def stile(i):
        return jnp.maximum(i - n_ptiles, 0)

    def mode(i):
        return jnp.where(i < n_ptiles, 0, 1)

    row_spec = lambda w: pl.BlockSpec((TM, w), lambda i: (i, 0))
    srow_spec = lambda w: pl.BlockSpec((TM, w), lambda i: (stile(i), 0))
    kern = functools.partial(_inproj_kernel, n_ptiles, tiles_per_seq, dec_seq)
    return pl.pallas_call(
        kern,
        grid=(n_ptiles + n_stiles,),
        in_specs=[
            row_spec(D_MODEL),
            pl.BlockSpec((None, D_MODEL, IN_DIM), lambda i: (l, 0, 0), pipeline_mode=pl.Buffered(1)),
            pl.BlockSpec((None, CONV_W, CONV_DIM), lambda i: (l, 0, 0)),
            pl.BlockSpec((None, 1, SG_DIM), lambda i: (l, 0, 0)),
            pl.BlockSpec((None, 1, SG_DIM), lambda i: (l, 0, 0)),
            pl.BlockSpec((None, None, SG_GROUPS, CHUNK, CHUNK), lambda i: (l, mode(i), 0, 0, 0)),
            pl.BlockSpec((None, None, CHUNK, SG_DIM), lambda i: (l, mode(i), 0, 0)),
            pl.BlockSpec((None, CONV_DIM, D_MODEL), lambda i: (l, 0, 0)),
            pl.BlockSpec((None, SG_DIM, D_MODEL), lambda i: (l, 0, 0)),
            srow_spec(CONV_DIM),
            srow_spec(CONV_DIM),
        ],
        out_specs=[
            row_spec(ATT_DIM), row_spec(ATT_DIM), row_spec(ATT_DIM),
            row_spec(D_MODEL), row_spec(D_MODEL),
            pl.BlockSpec((1, SUBLANES, CONV_DIM),
                         lambda i: (jnp.minimum(i // tiles_per_seq, batch - 1), 0, 0)),
            srow_spec(CONV_DIM),
            srow_spec(SG_DIM),
        ],
        out_shape=[
            jax.ShapeDtypeStruct((n, ATT_DIM), F32), jax.ShapeDtypeStruct((n, ATT_DIM), F32),
            jax.ShapeDtypeStruct((n, ATT_DIM), F32),
            jax.ShapeDtypeStruct((n, D_MODEL), F32), jax.ShapeDtypeStruct((n, D_MODEL), F32),
            jax.ShapeDtypeStruct((batch, SUBLANES, CONV_DIM), F32),
            jax.ShapeDtypeStruct((n_sample, CONV_DIM), F32),
            jax.ShapeDtypeStruct((n_sample, SG_DIM), F32),
        ],
        scratch_shapes=[pltpu.VMEM((SUBLANES, CONV_DIM), F32)],
        compiler_params=_params(1),
        name="inproj_mix",
    )(x, w_in, conv_w, sg_ln_g, sg_ln_b, wsp2, bsp2, w_up_a, w_up_b, s1, s2)


def _head_masks(rows):
    lane_head = lax.broadcasted_iota(jnp.int32, (rows, LANES), 1) // HEAD_DIM
    return [lane_head == h for h in range(HPG)]


def _stack_heads(qb, masks):
    return jnp.concatenate([jnp.where(m, qb, 0.0) for m in masks], axis=0)


def _unstack_heads(stacked, masks, r):
    out = None
    for h, m in enumerate(masks):
        part = jnp.where(m, stacked[h * r:(h + 1) * r], 0.0)
        out = part if out is None else out + part
    return out


def _mix_groups(outs, lses):
    mx = functools.reduce(jnp.maximum, lses)
    ws = [jnp.exp(ls - mx) for ls in lses]
    num = functools.reduce(lambda a, b: a + b, [w * o for w, o in zip(ws, outs)])
    return num / functools.reduce(lambda a, b: a + b, ws)


def _attn_prompt_kernel(seq, q_ref, k_ref, v_ref, bias_ref, o_ref, *scr):
    g = pl.program_id(1)
    o_scr, l_scr = scr[:N_ATT], scr[N_ATT:]
    masks = _head_masks(Q_BLK)
    scale = HEAD_DIM ** -0.5
    col = lax.broadcasted_iota(jnp.int32, (HPG * Q_BLK, 2 * Q_BLK), 1)

    def run_group(gi, dil):
        rows_per_class = seq // dil
        n_blk = rows_per_class // Q_BLK

        def block(it, carry):
            r = it // n_blk
            mb = it % n_blk
            cur = r + dil * Q_BLK * mb
            prev = r + dil * Q_BLK * jnp.maximum(mb - 1, 0)
            ld = lambda ref, start: ref[pl.ds(start, Q_BLK, stride=dil), :] if dil > 1 else ref[pl.ds(start, Q_BLK), :]
            qb = ld(q_ref, cur) * scale
            kw = jnp.concatenate([ld(k_ref, prev), ld(k_ref, cur)], axis=0)
            vw = jnp.concatenate([ld(v_ref, prev), ld(v_ref, cur)], axis=0)
            s = _dot_nt(_stack_heads(qb, masks), kw) + bias_ref[...]
            s = jnp.where(jnp.logical_or(col >= Q_BLK, mb > 0), s, NEG)
            m = jnp.max(s, axis=1, keepdims=True)
            p = jnp.exp(s - m)
            den = jnp.sum(p, axis=1, keepdims=True)
            pv = _dot(p, vw)
            o = _unstack_heads(pv / den, masks, Q_BLK)
            lse = _unstack_heads(m + jnp.log(den), masks, Q_BLK)
            if dil > 1:
                o_scr[gi][pl.ds(cur, Q_BLK, stride=dil), :] = o
                l_scr[gi][pl.ds(cur, Q_BLK, stride=dil), :] = lse
            else:
                o_scr[gi][pl.ds(cur, Q_BLK), :] = o
                l_scr[gi][pl.ds(cur, Q_BLK), :] = lse
            return carry

        lax.fori_loop(0, dil * n_blk, block, 0)

    for gi, (_, dil) in enumerate(ATT_GROUPS):
        @pl.when(g == gi)
        def _(gi=gi, dil=dil):
            run_group(gi, dil)

    @pl.when(g == N_ATT - 1)
    def _():
        o_ref[...] = _mix_groups([s[...] for s in o_scr], [s[...] for s in l_scr])


def _attn_prompt_call(q, k, v, bias_p, batch, seq):
    blk = lambda: pl.BlockSpec((seq, LANES), lambda b, g: (b, g))
    return pl.pallas_call(
        functools.partial(_attn_prompt_kernel, seq),
        grid=(batch, N_ATT),
        in_specs=[blk(), blk(), blk(),
                  pl.BlockSpec((None, HPG * Q_BLK, 2 * Q_BLK), lambda b, g: (g, 0, 0))],
        out_specs=pl.BlockSpec((seq, ATT_OUT), lambda b, g: (b, 0)),
        out_shape=jax.ShapeDtypeStruct((batch * seq, ATT_OUT), F32),
        scratch_shapes=[pltpu.VMEM((seq, LANES), F32)] * (2 * N_ATT),
        compiler_params=_params(2),
        name="attn_prompt",
    )(q, k, v, bias_p)


def _attn_sample_kernel(dec_seq, q_ref, k_ref, v_ref, kc0, vc0, kc1, vc1, kc2, vc2,
                        bc0, bc1, bc2, bn_ref, o_ref):
    kcs, vcs, bcs = (kc0, kc1, kc2), (vc0, vc1, vc2), (bc0, bc1, bc2)
    masks = _head_masks(dec_seq)
    scale = HEAD_DIM ** -0.5
    for s_i in range(SEQ_BLK):
        rows = slice(s_i * dec_seq, (s_i + 1) * dec_seq)
        outs, lses = [], []
        for gi in range(N_ATT):
            cols = slice(gi * LANES, (gi + 1) * LANES)
            qs = _stack_heads(q_ref[rows, cols] * scale, masks)
            k_new, v_new = k_ref[rows, cols], v_ref[rows, cols]
            s_c = _dot_nt(qs, kcs[gi][s_i]) + bcs[gi][...]
            s_n = _dot_nt(qs, k_new) + bn_ref[gi]
            m = jnp.maximum(jnp.max(s_c, axis=1, keepdims=True), jnp.max(s_n, axis=1, keepdims=True))
            p_c = jnp.exp(s_c - m)
            p_n = jnp.exp(s_n - m)
            den = jnp.sum(p_c, axis=1, keepdims=True) + jnp.sum(p_n, axis=1, keepdims=True)
            pv = _dot(p_c, vcs[gi][s_i]) + _dot(p_n, v_new)
            outs.append(_unstack_heads(pv / den, masks, dec_seq))
            lses.append(_unstack_heads(m + jnp.log(den), masks, dec_seq))
        o_ref[rows, :] = _mix_groups(outs, lses)


def _attn_sample_call(l, q, k, v, caches, bias_c, bias_n, n_prompt, dec_batch, dec_seq):
    rows = SEQ_BLK * dec_seq
    off = n_prompt // rows
    new_spec = pl.BlockSpec((rows, ATT_DIM), lambda i: (off + i, 0))
    cache_specs = [pl.BlockSpec((None, SEQ_BLK, c.shape[2], LANES), lambda i: (l, i, 0, 0)) for c in caches]
    bias_specs = [pl.BlockSpec(b.shape, lambda i: (0, 0)) for b in bias_c]
    return pl.pallas_call(
        functools.partial(_attn_sample_kernel, dec_seq),
        grid=(dec_batch // SEQ_BLK,),
        in_specs=[new_spec, new_spec, new_spec] + cache_specs + bias_specs
                 + [pl.BlockSpec(bias_n.shape, lambda i: (0, 0, 0))],
        out_specs=pl.BlockSpec((rows, ATT_OUT), lambda i: (i, 0)),
        out_shape=jax.ShapeDtypeStruct((dec_batch * dec_seq, ATT_OUT), F32),
        compiler_params=_params(1),
        name="attn_sample",
    )(q, k, v, *caches, *bias_c, bias_n)


def _merge_kernel(alpha, x_ref, mab_ref, gc_ref, yc_ref, wuc_ref, wo_ref, g_ref, b_ref, wrt_ref, br_ref,
                  x1_ref, x1g_ref, idx_ref, w_ref):
    merged = mab_ref[...] + gc_ref[...] * _dot(yc_ref[...], wuc_ref[...])
    x1 = _layer_norm(alpha * x_ref[...] + _dot(merged, wo_ref[...]), g_ref[...], b_ref[...])
    x1_ref[...] = x1
    for c in range(LANE_CHUNKS):
        x1g_ref[pl.ds(c, TM, stride=LANE_CHUNKS), :] = x1[:, c * LANES:(c + 1) * LANES]

    logits = lax.dot_general(wrt_ref[...], x1, (((1,), (1,)), ((), ())),
                             precision=lax.Precision.HIGHEST, preferred_element_type=F32) + br_ref[...]
    eio = lax.broadcasted_iota(jnp.int32, (N_EXPERTS, TM), 0)
    vals, idxs = [], []
    for _ in range(TOP_K):
        mv = jnp.max(logits, axis=0, keepdims=True)
        ix = jnp.min(jnp.where(logits == mv, eio, N_EXPERTS), axis=0, keepdims=True)
        vals.append(mv)
        idxs.append(ix)
        logits = jnp.where(eio == ix, -jnp.inf, logits)
    tv = jnp.concatenate(vals, axis=0)
    e = jnp.exp(tv - tv[0:1, :])
    w_ref[...] = e / jnp.sum(e, axis=0, keepdims=True)
    idx_ref[...] = jnp.concatenate(idxs, axis=0)


def _merge_call(l, alpha, x, mab, gc, yc, w_up_c, w_o, ln_g, ln_b, w_router_t, b_router):
    n = x.shape[0]
    row_spec = lambda w: pl.BlockSpec((TM, w), lambda i: (i, 0))
    vec_spec = pl.BlockSpec((None, 1, D_MODEL), lambda i: (l, 0, 0))
    return pl.pallas_call(
        functools.partial(_merge_kernel, alpha),
        grid=(n // TM,),
        in_specs=[
            row_spec(D_MODEL), row_spec(D_MODEL), row_spec(D_MODEL), row_spec(ATT_OUT),
            pl.BlockSpec((None, ATT_OUT, D_MODEL), lambda i: (l, 0, 0)),
            pl.BlockSpec((None, D_MODEL, D_MODEL), lambda i: (l, 0, 0)),
            vec_spec, vec_spec,
            pl.BlockSpec((None, N_EXPERTS, D_MODEL), lambda i: (l, 0, 0)),
            pl.BlockSpec((None, N_EXPERTS, 1), lambda i: (l, 0, 0)),
        ],
        out_specs=[
            row_spec(D_MODEL),
            pl.BlockSpec((TM * LANE_CHUNKS, LANES), lambda i: (i, 0)),
            pl.BlockSpec((TOP_K, TM), lambda i: (0, i)),
            pl.BlockSpec((TOP_K, TM), lambda i: (0, i)),
        ],
        out_shape=[
            jax.ShapeDtypeStruct((n, D_MODEL), F32),
            jax.ShapeDtypeStruct((n * LANE_CHUNKS, LANES), F32),
            jax.ShapeDtypeStruct((TOP_K, n), jnp.int32),
            jax.ShapeDtypeStruct((TOP_K, n), F32),
        ],
        compiler_params=_params(1),
        name="merge_ln1_router",
    )(x, mab, gc, yc, w_up_c, w_o, ln_g, ln_b, w_router_t, b_router)


def _token_slab_copy(src_hbm, row, dst, slot, sem):
    return pltpu.make_async_copy(src_hbm.at[pl.ds(row * LANE_CHUNKS, LANE_CHUNKS), :],
                                 dst.at[pl.ds(slot * LANE_CHUNKS, LANE_CHUNKS), :], sem)


def _destride(ref, rows):
    return jnp.concatenate([ref[pl.ds(c, rows, stride=LANE_CHUNKS), :] for c in range(LANE_CHUNKS)], axis=1)


def _lane_row_to_column(w_row, n):
    r = lax.broadcasted_iota(jnp.int32, (n, n), 0)
    c = lax.broadcasted_iota(jnp.int32, (n, n), 1)
    return jnp.sum(jnp.where(r == c, jnp.broadcast_to(w_row, (n, n)), 0.0), axis=1, keepdims=True)


def _moe_ffn_kernel(te_ref, nv_ref, tok_ref, ws_ref, x1g_hbm, wg_ref, bg_ref, wu_ref, bu_ref, wd_ref, bd_ref,
                    ys_ref, xg, sem):
    t = pl.program_id(0)

    @pl.when(t < nv_ref[0])
    def _():
        def issue(r, carry):
            _token_slab_copy(x1g_hbm, tok_ref[0, 0, r], xg, r, sem).start()
            return carry

        lax.fori_loop(0, TM_E, issue, 0)
        pltpu.make_async_copy(x1g_hbm.at[pl.ds(0, TM_E * LANE_CHUNKS), :], xg, sem).wait()
        x = _destride(xg, TM_E).astype(BF16)
        gl = jnp.minimum(_dot(x, wg_ref[...]) + bg_ref[...], SWIGLU_LIMIT)
        ul = jnp.clip(_dot(x, wu_ref[...]) + bu_ref[...], -SWIGLU_LIMIT, SWIGLU_LIMIT)
        hid = gl * _sigmoid(SWIGLU_ALPHA * gl) * (ul + 1.0)
        y = (_dot(hid, wd_ref[...]) + bd_ref[...]) * _lane_row_to_column(ws_ref[...], TM_E)
        for c in range(LANE_CHUNKS):
            ys_ref[pl.ds(c, TM_E, stride=LANE_CHUNKS), :] = y[:, c * LANES:(c + 1) * LANES]

    @pl.when(t >= nv_ref[0])
    def _():
        ys_ref[...] = jnp.zeros_like(ys_ref)


def _moe_ffn_call(l, tile_expert, n_valid, tok_sorted, w_sorted, x1g, w_gate, b_gate, w_up, b_up, w_down, b_down):
    n_tiles = tile_expert.shape[0]
    mat_spec = pl.BlockSpec((None, None, D_MODEL, D_MODEL), lambda t, te, nv: (l, te[t], 0, 0))
    vec_spec = pl.BlockSpec((None, None, 1, D_MODEL), lambda t, te, nv: (l, te[t], 0, 0))
    grid_spec = pltpu.PrefetchScalarGridSpec(
        num_scalar_prefetch=2,
        grid=(n_tiles,),
        in_specs=[
            pl.BlockSpec((1, 1, TM_E), lambda t, te, nv: (t, 0, 0), memory_space=pltpu.SMEM),
            pl.BlockSpec((None, 1, TM_E), lambda t, te, nv: (t, 0, 0)),
            pl.BlockSpec(memory_space=pl.ANY),
            mat_spec, vec_spec, mat_spec, vec_spec, mat_spec, vec_spec,
        ],
        out_specs=pl.BlockSpec((TM_E * LANE_CHUNKS, LANES), lambda t, te, nv: (t, 0)),
        scratch_shapes=[pltpu.VMEM((TM_E * LANE_CHUNKS, LANES), F32), pltpu.SemaphoreType.DMA(())],
    )
    return pl.pallas_call(
        _moe_ffn_kernel,
        grid_spec=grid_spec,
        out_shape=jax.ShapeDtypeStruct((n_tiles * TM_E * LANE_CHUNKS, LANES), F32),
        compiler_params=_params(1),
        name="moe_ffn",
    )(tile_expert, n_valid, tok_sorted, w_sorted, x1g, w_gate, b_gate, w_up, b_up, w_down, b_down)


def _combine_kernel(alpha, pos_ref, x1_ref, ys_hbm, g_ref, b_ref, x2_ref, buf, acc, sem):
    def issue(r, carry):
        for kk in range(TOP_K):
            _token_slab_copy(ys_hbm, pos_ref[kk, r], buf.at[kk], r, sem).start()
        return carry

    lax.fori_loop(0, TM, issue, 0)
    for kk in range(TOP_K):
        pltpu.make_async_copy(ys_hbm.at[pl.ds(0, TM * LANE_CHUNKS), :], buf.at[kk], sem).wait()
    acc[...] = (buf[0] + buf[1]) + (buf[2] + buf[3])
    x2_ref[...] = _layer_norm(alpha * x1_ref[...] + _destride(acc, TM), g_ref[...], b_ref[...])


def _combine_call(l, alpha, pos, x1, ys, ln_g, ln_b):
    n = x1.shape[0]
    vec_spec = pl.BlockSpec((None, 1, D_MODEL), lambda i: (l, 0, 0))
    return pl.pallas_call(
        functools.partial(_combine_kernel, alpha),
        grid=(n // TM,),
        in_specs=[
            pl.BlockSpec((TOP_K, TM), lambda i: (0, i), memory_space=pltpu.SMEM),
            pl.BlockSpec((TM, D_MODEL), lambda i: (i, 0)),
            pl.BlockSpec(memory_space=pl.ANY),
            vec_spec, vec_spec,
        ],
        out_specs=pl.BlockSpec((TM, D_MODEL), lambda i: (i, 0)),
        out_shape=jax.ShapeDtypeStruct((n, D_MODEL), F32),
        scratch_shapes=[pltpu.VMEM((TOP_K, TM * LANE_CHUNKS, LANES), F32),
                        pltpu.VMEM((TM * LANE_CHUNKS, LANES), F32),
                        pltpu.SemaphoreType.DMA(())],
        compiler_params=_params(1),
        name="moe_combine_ln2",
    )(pos, x1, ys, ln_g, ln_b)


def _route(top_idx, top_w, n_tiles):
    n = top_idx.shape[1]
    e_flat = top_idx.reshape(-1)
    onehot = (e_flat[:, None] == jnp.arange(N_EXPERTS, dtype=jnp.int32)[None, :]).astype(jnp.int32)
    csum = jnp.cumsum(onehot, axis=0)
    rank = jnp.sum((csum - onehot) * onehot, axis=1)
    counts = csum[-1]
    padded = ((counts + TM_E - 1) // TM_E) * TM_E
    ends = jnp.cumsum(padded)
    offs = ends - padded
    pos = (offs[e_flat] + rank).astype(jnp.int32)
    tok = jnp.tile(jnp.arange(n, dtype=jnp.int32), TOP_K)
    tok_sorted = jnp.zeros((n_tiles * TM_E,), jnp.int32).at[pos].set(tok)
    w_sorted = jnp.zeros((n_tiles * TM_E,), F32).at[pos].set(top_w.reshape(-1))
    n_valid = (ends[-1] // TM_E).astype(jnp.int32)
    tile_start = jnp.arange(n_tiles, dtype=jnp.int32) * TM_E
    tile_expert = jnp.searchsorted(ends, jnp.minimum(tile_start, ends[-1] - 1), side="right").astype(jnp.int32)
    return (tile_expert, n_valid.reshape(1), tok_sorted.reshape(n_tiles, 1, TM_E),
            w_sorted.reshape(n_tiles, 1, TM_E), pos.reshape(TOP_K, n))


def _t5_bucket(dist):
    max_exact = NUM_BUCKETS // 2
    distf = jnp.maximum(dist, 1).astype(F32)
    large = max_exact + (jnp.log(distf / max_exact) / math.log(MAX_DISTANCE / max_exact)
                         * (NUM_BUCKETS - max_exact)).astype(jnp.int32)
    large = jnp.minimum(large, NUM_BUCKETS - 1)
    return jnp.where(dist < max_exact, dist, large)


def _group_bias(rel_bias, gi, dil):
    j = jnp.arange(N_KEYS, dtype=jnp.int32)
    return rel_bias[_t5_bucket(dil * j)][:, gi * HPG:(gi + 1) * HPG].T.astype(F32)


def _bias_from_steps(bias, steps):
    valid = jnp.logical_and(steps >= 0, steps < N_KEYS)
    tab = jnp.where(valid[None], bias[:, jnp.clip(steps, 0, N_KEYS - 1)], NEG)
    return tab.reshape(HPG * steps.shape[0], steps.shape[1])


def _prompt_bias(rel_bias):
    i = jnp.arange(Q_BLK, dtype=jnp.int32)[:, None]
    c = jnp.arange(2 * Q_BLK, dtype=jnp.int32)[None, :]
    return jnp.stack([_bias_from_steps(_group_bias(rel_bias, gi, dil), i + Q_BLK - c)
                      for gi, (_, dil) in enumerate(ATT_GROUPS)], axis=0)


def _sample_bias(rel_bias, pasts, dec_seq):
    i = jnp.arange(dec_seq, dtype=jnp.int32)[:, None]
    bias_c, bias_n = [], []
    for gi, (_, dil) in enumerate(ATT_GROUPS):
        bias = _group_bias(rel_bias, gi, dil)
        past = pasts[gi]

        def steps(rows):
            dist = past + i - rows
            return jnp.where(dist % dil == 0, dist // dil, -1)

        bias_c.append(_bias_from_steps(bias, steps(jnp.arange(past, dtype=jnp.int32)[None, :])))
        bias_n.append(_bias_from_steps(bias, steps(past + jnp.arange(dec_seq, dtype=jnp.int32)[None, :])))
    return bias_c, jnp.stack(bias_n, axis=0)


def _spatial_tables(w_sp, b_sp, dec_seq):
    depth = w_sp.shape[0]
    tril = jnp.tril(jnp.ones((CHUNK, CHUNK), bool))
    wp = jnp.where(tril, w_sp, 0.0)
    ws_small = jnp.where(tril[:dec_seq, :dec_seq], w_sp[:, :, :dec_seq, :dec_seq], 0.0)
    eye = jnp.eye(CHUNK // dec_seq, dtype=w_sp.dtype)
    ws = jnp.einsum("ab,lgij->lgaibj", eye, ws_small).reshape(depth, SG_GROUPS, CHUNK, CHUNK)
    bp = jnp.repeat(jnp.swapaxes(b_sp, 1, 2), SG_HEAD, axis=2)
    bs = jnp.tile(bp[:, :dec_seq], (1, CHUNK // dec_seq, 1))
    return jnp.stack([wp, ws], axis=1).astype(BF16), jnp.stack([bp, bs], axis=1)


def kernel(x_prompt, x_sample, state_conv, cache_k_w128, cache_v_w128, cache_k_w512, cache_v_w512,
           cache_k_w2048, cache_v_w2048, w_in, conv_w, sg_ln_g, sg_ln_b, w_sp, b_sp, rel_bias,
           w_up_a, w_up_b, w_up_c, w_o, ln1_g, ln1_b, w_router, b_router, w_gate, b_gate, w_up, b_up,
           w_down, b_down, ln2_g, ln2_b):
    batch, seq, _ = x_prompt.shape
    dec_batch, dec_seq, _ = x_sample.shape
    depth = w_in.shape[0]
    n_prompt, n_sample = batch * seq, dec_batch * dec_seq
    n = n_prompt + n_sample
    alpha = (2 * depth) ** 0.25
    assert seq % TM == 0 and n_sample % TM == 0 and dec_seq & (dec_seq - 1) == 0 and CHUNK % dec_seq == 0
    assert all(seq % (dil * Q_BLK) == 0 for _, dil in ATT_GROUPS) and dec_batch % SEQ_BLK == 0
    n_tiles = (n * TOP_K) // TM_E + N_EXPERTS

    caches = [c.reshape(c.shape[0], c.shape[1], c.shape[2], LANES) for c in
              (cache_k_w128, cache_v_w128, cache_k_w512, cache_v_w512, cache_k_w2048, cache_v_w2048)]
    pasts = [caches[2 * gi].shape[2] for gi in range(N_ATT)]
    bias_p = _prompt_bias(rel_bias)
    bias_c, bias_n = _sample_bias(rel_bias, pasts, dec_seq)
    wsp2, bsp2 = _spatial_tables(w_sp, b_sp, dec_seq)
    w_in_b, w_up_a_b, w_up_b_b, w_up_c_b, w_o_b = (w.astype(BF16) for w in (w_in, w_up_a, w_up_b, w_up_c, w_o))
    vec = lambda a: a.reshape(depth, 1, a.shape[-1])
    w_router_t = jnp.swapaxes(w_router, 1, 2)
    b_router_c = b_router.reshape(depth, N_EXPERTS, 1)
    expert_vec = lambda a: a.reshape(depth, N_EXPERTS, 1, a.shape[-1])

    x = jnp.concatenate([x_prompt.reshape(n_prompt, D_MODEL), x_sample.reshape(n_sample, D_MODEL)], axis=0)
    outs = {name: [] for name in ("conv_p", "conv_s", "sgv", "kp", "vp", "ks", "vs")}
    for l in range(depth):
        s1 = jnp.zeros((dec_batch, dec_seq, CONV_DIM), F32).at[:, 0].set(state_conv[l, :, 1])
        s2 = jnp.zeros((dec_batch, dec_seq, CONV_DIM), F32).at[:, 0].set(state_conv[l, :, 0])
        s2 = s2.at[:, 1].set(state_conv[l, :, 1])
        q, k, v, mab, gc, tail, z_s, v_rows = _inproj_call(
            l, x, w_in_b, conv_w, vec(sg_ln_g), vec(sg_ln_b), wsp2, bsp2, w_up_a_b, w_up_b_b,
            s1.reshape(n_sample, CONV_DIM), s2.reshape(n_sample, CONV_DIM), n_prompt, seq, n_sample, dec_seq)
        yc_p = _attn_prompt_call(q, k, v, bias_p, batch, seq)
        yc_s = _attn_sample_call(l, q, k, v, [c for c in caches], bias_c, bias_n, n_prompt, dec_batch, dec_seq)
        yc = jnp.concatenate([yc_p, yc_s], axis=0)
        x1, x1g, top_idx, top_w = _merge_call(l, alpha, x, mab, gc, yc, w_up_c_b, w_o_b, vec(ln1_g), vec(ln1_b),
                                              w_router_t, b_router_c)
        tile_expert, n_valid, tok_sorted, w_sorted, pos = _route(top_idx, top_w, n_tiles)
        ys = _moe_ffn_call(l, tile_expert, n_valid, tok_sorted, w_sorted, x1g, w_gate, expert_vec(b_gate),
                           w_up, expert_vec(b_up), w_down, expert_vec(b_down))
        x = _combine_call(l, alpha, pos, x1, ys, vec(ln2_g), vec(ln2_b))

        outs["conv_p"].append(tail[:, SUBLANES - (CONV_W - 1):])
        outs["conv_s"].append(z_s.reshape(dec_batch, dec_seq, CONV_DIM)[:, dec_seq - (CONV_W - 1):])
        outs["sgv"].append(v_rows.reshape(dec_batch, dec_seq, SG_DIM))
        kp4 = k[:n_prompt].reshape(batch, seq, N_ATT, HPG, HEAD_DIM)
        vp4 = v[:n_prompt].reshape(batch, seq, N_ATT, HPG, HEAD_DIM)
        ks4 = k[n_prompt:].reshape(dec_batch, dec_seq, N_ATT, HPG, HEAD_DIM)
        vs4 = v[n_prompt:].reshape(dec_batch, dec_seq, N_ATT, HPG, HEAD_DIM)
        outs["kp"].append([kp4[:, seq - min(win, seq):, gi] for gi, (win, _) in enumerate(ATT_GROUPS)])
        outs["vp"].append([vp4[:, seq - min(win, seq):, gi] for gi, (win, _) in enumerate(ATT_GROUPS)])
        outs["ks"].append([ks4[:, dec_seq - min(win, dec_seq):, gi] for gi, (win, _) in enumerate(ATT_GROUPS)])
        outs["vs"].append([vs4[:, dec_seq - min(win, dec_seq):, gi] for gi, (win, _) in enumerate(ATT_GROUPS)])

    stack = lambda name: jnp.stack(outs[name], axis=0)
    per_group = lambda name, gi: jnp.stack([layer[gi] for layer in outs[name]], axis=0)
    result = [x[:n_prompt].reshape(batch, seq, D_MODEL), x[n_prompt:].reshape(dec_batch, dec_seq, D_MODEL),
              stack("conv_p"), stack("conv_s")]
    for gi in range(N_ATT):
        result += [per_group("kp", gi), per_group("vp", gi)]
    for gi in range(N_ATT):
        result += [per_group("ks", gi), per_group("vs", gi)]
    result.append(stack("sgv"))
    return tuple(result)
```

```python
import functools
import math

import jax
import jax.numpy as jnp
from jax import lax
from jax.experimental import pallas as pl
from jax.experimental.pallas import tpu as pltpu

D_MODEL = 1024
CONV_DIM = 384
CONV_W = 3
SG_DIM = 256
SG_GROUPS = 4
SG_HEAD = SG_DIM // SG_GROUPS
CHUNK = 128
ATT_GROUPS = ((128, 1), (512, 4), (2048, 16))
N_ATT = len(ATT_GROUPS)
HPG = 4
HEAD_DIM = 32
ATT_DIM = HPG * N_ATT * HEAD_DIM
ATT_OUT = HPG * HEAD_DIM
N_KEYS = 129
NUM_BUCKETS = 32
MAX_DISTANCE = 2048
N_EXPERTS = 32
TOP_K = 4
SWIGLU_LIMIT = 7.0
SWIGLU_ALPHA = 1.702
LN_EPS = 1e-5
OFF_A = 0
OFF_B = OFF_A + 3 * CONV_DIM
OFF_C = OFF_B + 2 * SG_DIM
OFF_G = OFF_C + 3 * ATT_DIM
IN_DIM = OFF_G + 3 * D_MODEL
NEG = -1e30

LANES = 128
SUBLANES = 8
LANE_CHUNKS = D_MODEL // LANES
VMEM_LIMIT = 56 * 1024 * 1024

TM = 256
TM_E = 256
Q_BLK = 128
SEQ_BLK = 4
DMA_UNROLL = 8

F32 = jnp.float32
BF16 = jnp.bfloat16


def _dot(a, b):
    return jnp.dot(a.astype(BF16), b.astype(BF16), preferred_element_type=F32)


def _dot_nt(a, b):
    return lax.dot_general(a.astype(BF16), b.astype(BF16), (((1,), (1,)), ((), ())),
                           preferred_element_type=F32)


def _layer_norm(x, g, b):
    mu = jnp.mean(x, axis=-1, keepdims=True)
    xc = x - mu
    var = jnp.mean(xc * xc, axis=-1, keepdims=True)
    return xc * lax.rsqrt(var + LN_EPS) * g + b


def _gelu(x):
    return 0.5 * x * (1.0 + lax.erf(x * (2.0 ** -0.5)))


def _sigmoid(x):
    return 1.0 / (1.0 + jnp.exp(-x))


def _params(n_axes):
    return pltpu.CompilerParams(dimension_semantics=("arbitrary",) * n_axes,
                                vmem_limit_bytes=VMEM_LIMIT)


def _inproj_kernel(n_ptiles, tiles_per_seq, dec_seq,
                   x_ref, w_in_ref, convw_ref, lng_ref, lnb_ref, wsp_ref, bsp_ref, wua_ref, wub_ref,
                   s1_ref, s2_ref,
                   q_ref, k_ref, v_ref, mab_ref, gc_ref, tail_ref, zs_ref, vrows_ref, carry_ref):
    i = pl.program_id(0)
    is_prompt = i < n_ptiles

    @pl.when(jnp.logical_and(is_prompt, i % tiles_per_seq == 0))
    def _():
        carry_ref[...] = jnp.zeros_like(carry_ref)

    x = x_ref[...].astype(BF16)

    pa = jnp.dot(x, w_in_ref[:, OFF_A:OFF_B], preferred_element_type=F32)
    bg, cg, h = pa[:, :CONV_DIM], pa[:, CONV_DIM:2 * CONV_DIM], pa[:, 2 * CONV_DIM:]
    z = cg * h
    row = lax.broadcasted_iota(jnp.int32, (TM, CONV_DIM), 0)
    pos = jnp.where(is_prompt, row, row & (dec_seq - 1))
    prev1 = jnp.where(is_prompt, jnp.broadcast_to(carry_ref[7:8, :], (TM, CONV_DIM)), s1_ref[...])
    prev2 = jnp.where(is_prompt,
                      jnp.where(row == 0, jnp.broadcast_to(carry_ref[6:7, :], (TM, CONV_DIM)),
                                jnp.broadcast_to(carry_ref[7:8, :], (TM, CONV_DIM))),
                      s2_ref[...])
    z1 = jnp.where(pos >= 1, pltpu.roll(z, 1, axis=0), prev1)
    z2 = jnp.where(pos >= 2, pltpu.roll(z, 2, axis=0), prev2)
    cw = convw_ref[...]
    ya = bg * (cw[0:1, :] * z2 + cw[1:2, :] * z1 + cw[2:3, :] * z)

    @pl.when(is_prompt)
    def _():
        carry_ref[...] = z[TM - SUBLANES:, :]
        tail_ref[0] = z[TM - SUBLANES:, :]

    @pl.when(jnp.logical_not(is_prompt))
    def _():
        zs_ref[...] = z

    pb = jnp.dot(x, w_in_ref[:, OFF_B:OFF_C], preferred_element_type=F32)
    uv = _gelu(pb)
    u = uv[:, :SG_DIM]
    vn = _layer_norm(uv[:, SG_DIM:], lng_ref[...], lnb_ref[...])

    @pl.when(jnp.logical_not(is_prompt))
    def _():
        vrows_ref[...] = vn

    vb = vn.astype(BF16)
    low_half = lax.broadcasted_iota(jnp.int32, (CHUNK, LANES), 1) < SG_HEAD
    s_rows = []
    for c in range(TM // CHUNK):
        s_cols = []
        for p in range(SG_DIM // LANES):
            vcp = vb[c * CHUNK:(c + 1) * CHUNK, p * LANES:(p + 1) * LANES]
            sa = jnp.dot(wsp_ref[2 * p], vcp, preferred_element_type=F32)
            sb = jnp.dot(wsp_ref[2 * p + 1], vcp, preferred_element_type=F32)
            s_cols.append(jnp.where(low_half, sa, sb))
        s_rows.append(jnp.concatenate(s_cols, axis=1) + bsp_ref[...])
    yb = u * jnp.concatenate(s_rows, axis=0)

    pc = jnp.dot(x, w_in_ref[:, OFF_C:OFF_G], preferred_element_type=F32)
    q_ref[...] = pc[:, :ATT_DIM]
    k_ref[...] = pc[:, ATT_DIM:2 * ATT_DIM]
    v_ref[...] = pc[:, 2 * ATT_DIM:]

    ga = _sigmoid(jnp.dot(x, w_in_ref[:, OFF_G:OFF_G + D_MODEL], preferred_element_type=F32))
    mab = ga * _dot(ya, wua_ref[...])
    gb = _sigmoid(jnp.dot(x, w_in_ref[:, OFF_G + D_MODEL:OFF_G + 2 * D_MODEL], preferred_element_type=F32))
    mab_ref[...] = mab + gb * _dot(yb, wub_ref[...])
    gc_ref[...] = _sigmoid(jnp.dot(x, w_in_ref[:, OFF_G + 2 * D_MODEL:], preferred_element_type=F32))


def _inproj_call(l, x, w_in, conv_w, sg_ln_g, sg_ln_b, wsp2, bsp2, w_up_a, w_up_b, s1, s2,
                 n_prompt, seq, n_sample, dec_seq):
    n = n_prompt + n_sample
    n_ptiles = n_prompt // TM
    n_stiles = n_sample // TM
    tiles_per_seq = seq // TM
    batch = n_prompt // seq

    def stile(i):
        return jnp.maximum(i - n_ptiles, 0)

    def mode(i):
        return jnp.where(i < n_ptiles, 0, 1)

    row_spec = lambda w: pl.BlockSpec((TM, w), lambda i: (i, 0))
    srow_spec = lambda w: pl.BlockSpec((TM, w), lambda i: (stile(i), 0))
    kern = functools.partial(_inproj_kernel, n_ptiles, tiles_per_seq, dec_seq)
    return pl.pallas_call(
        kern,
        grid=(n_ptiles + n_stiles,),
        in_specs=[
            row_spec(D_MODEL),
            pl.BlockSpec((None, D_MODEL, IN_DIM), lambda i: (l, 0, 0), pipeline_mode=pl.Buffered(1)),
            pl.BlockSpec((None, CONV_W, CONV_DIM), lambda i: (l, 0, 0)),
            pl.BlockSpec((None, 1, SG_DIM), lambda i: (l, 0, 0)),
            pl.BlockSpec((None, 1, SG_DIM), lambda i: (l, 0, 0)),
            pl.BlockSpec((None, None, SG_GROUPS, CHUNK, CHUNK), lambda i: (l, mode(i), 0, 0, 0)),
            pl.BlockSpec((None, None, CHUNK, SG_DIM), lambda i: (l, mode(i), 0, 0)),
            pl.BlockSpec((None, CONV_DIM, D_MODEL), lambda i: (l, 0, 0)),
            pl.BlockSpec((None, SG_DIM, D_MODEL), lambda i: (l, 0, 0)),
            srow_spec(CONV_DIM),
            srow_spec(CONV_DIM),
        ],
        out_specs=[
            row_spec(ATT_DIM), row_spec(ATT_DIM), row_spec(ATT_DIM),
            row_spec(D_MODEL), row_spec(D_MODEL),
            pl.BlockSpec((1, SUBLANES, CONV_DIM),
                         lambda i: (jnp.minimum(i // tiles_per_seq, batch - 1), 0, 0)),
            srow_spec(CONV_DIM),
            srow_spec(SG_DIM),
        ],
        out_shape=[
            jax.ShapeDtypeStruct((n, ATT_DIM), F32), jax.ShapeDtypeStruct((n, ATT_DIM), F32),
            jax.ShapeDtypeStruct((n, ATT_DIM), F32),
            jax.ShapeDtypeStruct((n, D_MODEL), F32), jax.ShapeDtypeStruct((n, D_MODEL), F32),
            jax.ShapeDtypeStruct((batch, SUBLANES, CONV_DIM), F32),
            jax.ShapeDtypeStruct((n_sample, CONV_DIM), F32),
            jax.ShapeDtypeStruct((n_sample, SG_DIM), F32),
        ],
        scratch_shapes=[pltpu.VMEM((SUBLANES, CONV_DIM), F32)],
        compiler_params=_params(1),
        name="inproj_mix",
    )(x, w_in, conv_w, sg_ln_g, sg_ln_b, wsp2, bsp2, w_up_a, w_up_b, s1, s2)


def _head_masks(rows):
    lane_head = lax.broadcasted_iota(jnp.int32, (rows, LANES), 1) // HEAD_DIM
    return [lane_head == h for h in range(HPG)]


def _stack_heads(qb, masks):
    return jnp.concatenate([jnp.where(m, qb, 0.0) for m in masks], axis=0)


def _unstack_heads(stacked, masks, r):
    out = None
    for h, m in enumerate(masks):
        part = jnp.where(m, stacked[h * r:(h + 1) * r], 0.0)
        out = part if out is None else out + part
    return out


def _mix_groups(outs, lses):
    mx = functools.reduce(jnp.maximum, lses)
    ws = [jnp.exp(ls - mx) for ls in lses]
    num = functools.reduce(lambda a, b: a + b, [w * o for w, o in zip(ws, outs)])
    return num / functools.reduce(lambda a, b: a + b, ws)


def _bias_table(bias_ref, base, steps):
    def body(j, tab):
        return jnp.where(steps == j, bias_ref[base + j], tab)

    return lax.fori_loop(0, N_KEYS, body, jnp.full(steps.shape, NEG, F32))


def _attn_prompt_kernel(seq, bias_ref, q_ref, k_ref, v_ref, o_ref, tab_scr, *scr):
    b = pl.program_id(0)
    g = pl.program_id(1)
    o_scr, l_scr = scr[:N_ATT], scr[N_ATT:]
    masks = _head_masks(Q_BLK)
    scale = HEAD_DIM ** -0.5
    col = lax.broadcasted_iota(jnp.int32, (HPG * Q_BLK, 2 * Q_BLK), 1)

    def build_table(gi):
        qi = lax.broadcasted_iota(jnp.int32, (Q_BLK, 2 * Q_BLK), 0)
        kc = lax.broadcasted_iota(jnp.int32, (Q_BLK, 2 * Q_BLK), 1)
        for h in range(HPG):
            tab_scr[gi, h * Q_BLK:(h + 1) * Q_BLK, :] = _bias_table(bias_ref, (gi * HPG + h) * N_KEYS,
                                                                    qi + Q_BLK - kc)

    def run_group(gi, dil):
        rows_per_class = seq // dil
        n_blk = rows_per_class // Q_BLK

        def ld(ref, start):
            if dil > 1:
                return ref[pl.ds(start, Q_BLK, stride=dil), :]
            return ref[pl.ds(start, Q_BLK), :]

        def st(ref, start, val):
            if dil > 1:
                ref[pl.ds(start, Q_BLK, stride=dil), :] = val
            else:
                ref[pl.ds(start, Q_BLK), :] = val

        def block(it, carry):
            r = it // n_blk
            mb = it % n_blk
            cur = r + dil * Q_BLK * mb
            prev = r + dil * Q_BLK * jnp.maximum(mb - 1, 0)
            qb = ld(q_ref, cur) * scale
            kw = jnp.concatenate([ld(k_ref, prev), ld(k_ref, cur)], axis=0)
            vw = jnp.concatenate([ld(v_ref, prev), ld(v_ref, cur)], axis=0)
            s = _dot_nt(_stack_heads(qb, masks), kw) + tab_scr[gi]
            s = jnp.where(jnp.logical_or(col >= Q_BLK, mb > 0), s, NEG)
            m = jnp.max(s, axis=1, keepdims=True)
            p = jnp.exp(s - m)
            den = jnp.sum(p, axis=1, keepdims=True)
            pv = _dot(p, vw)
            st(o_scr[gi], cur, _unstack_heads(pv / den, masks, Q_BLK))
            st(l_scr[gi], cur, _unstack_heads(m + jnp.log(den), masks, Q_BLK))
            return carry

        lax.fori_loop(0, dil * n_blk, block, 0)

    for gi, (_, dil) in enumerate(ATT_GROUPS):
        @pl.when(jnp.logical_and(g == gi, b == 0))
        def _(gi=gi):
            build_table(gi)

        @pl.when(g == gi)
        def _(gi=gi, dil=dil):
            run_group(gi, dil)

    @pl.when(g == N_ATT - 1)
    def _():
        o_ref[...] = _mix_groups([s[...] for s in o_scr], [s[...] for s in l_scr])


def _attn_prompt_call(q, k, v, bias_flat, batch, seq):
    blk = lambda: pl.BlockSpec((seq, LANES), lambda b, g: (b, g))
    return pl.pallas_call(
        functools.partial(_attn_prompt_kernel, seq),
        grid=(batch, N_ATT),
        in_specs=[pl.BlockSpec(memory_space=pltpu.SMEM), blk(), blk(), blk()],
        out_specs=pl.BlockSpec((seq, ATT_OUT), lambda b, g: (b, 0)),
        out_shape=jax.ShapeDtypeStruct((batch * seq, ATT_OUT), F32),
        scratch_shapes=[pltpu.VMEM((N_ATT, HPG * Q_BLK, 2 * Q_BLK), F32)]
                       + [pltpu.VMEM((seq, LANES), F32)] * (2 * N_ATT),
        compiler_params=_params(2),
        name="attn_prompt",
    )(bias_flat, q, k, v)


def _attn_sample_kernel(dec_seq, pasts, bias_ref, q_ref, k_ref, v_ref, kc0, vc0, kc1, vc1, kc2, vc2,
                        o_ref, tc0, tc1, tc2, tn_scr):
    kcs, vcs, tcs = (kc0, kc1, kc2), (vc0, vc1, vc2), (tc0, tc1, tc2)
    masks = _head_masks(dec_seq)
    scale = HEAD_DIM ** -0.5

    @pl.when(pl.program_id(0) == 0)
    def _():
        for gi, (_, dil) in enumerate(ATT_GROUPS):
            past = pasts[gi]
            shift = dil.bit_length() - 1

            def steps(dist, dil=dil, shift=shift):
                return jnp.where((dist & (dil - 1)) == 0, dist >> shift, -1)

            qi_c = lax.broadcasted_iota(jnp.int32, (dec_seq, past), 0)
            row_c = lax.broadcasted_iota(jnp.int32, (dec_seq, past), 1)
            qi_n = lax.broadcasted_iota(jnp.int32, (dec_seq, dec_seq), 0)
            row_n = lax.broadcasted_iota(jnp.int32, (dec_seq, dec_seq), 1)
            for h in range(HPG):
                base = (gi * HPG + h) * N_KEYS
                tcs[gi][h * dec_seq:(h + 1) * dec_seq, :] = _bias_table(bias_ref, base, steps(past + qi_c - row_c))
                tn_scr[gi, h * dec_seq:(h + 1) * dec_seq, :] = _bias_table(bias_ref, base, steps(qi_n - row_n))

    for s_i in range(SEQ_BLK):
        rows = slice(s_i * dec_seq, (s_i + 1) * dec_seq)
        outs, lses = [], []
        for gi in range(N_ATT):
            cols = slice(gi * LANES, (gi + 1) * LANES)
            qs = _stack_heads(q_ref[rows, cols] * scale, masks)
            k_new, v_new = k_ref[rows, cols], v_ref[rows, cols]
            s_c = _dot(qs, kcs[gi][s_i]) + tcs[gi][...]
            s_n = _dot_nt(qs, k_new) + tn_scr[gi]
            m = jnp.maximum(jnp.max(s_c, axis=1, keepdims=True), jnp.max(s_n, axis=1, keepdims=True))
            p_c = jnp.exp(s_c - m)
            p_n = jnp.exp(s_n - m)
            den = jnp.sum(p_c, axis=1, keepdims=True) + jnp.sum(p_n, axis=1, keepdims=True)
            pv = _dot_nt(p_c, vcs[gi][s_i]) + _dot(p_n, v_new)
            outs.append(_unstack_heads(pv / den, masks, dec_seq))
            lses.append(_unstack_heads(m + jnp.log(den), masks, dec_seq))
        o_ref[rows, :] = _mix_groups(outs, lses)


def _attn_sample_call(l, q, k, v, caches_t, bias_flat, n_prompt, dec_batch, dec_seq):
    rows = SEQ_BLK * dec_seq
    off = n_prompt // rows
    pasts = tuple(caches_t[2 * gi].shape[3] for gi in range(N_ATT))
    new_spec = pl.BlockSpec((rows, ATT_DIM), lambda i: (off + i, 0))
    cache_specs = [pl.BlockSpec((None, SEQ_BLK, LANES, c.shape[3]), lambda i: (l, i, 0, 0)) for c in caches_t]
    return pl.pallas_call(
        functools.partial(_attn_sample_kernel, dec_seq, pasts),
        grid=(dec_batch // SEQ_BLK,),
        in_specs=[pl.BlockSpec(memory_space=pltpu.SMEM), new_spec, new_spec, new_spec] + cache_specs,
        out_specs=pl.BlockSpec((rows, ATT_OUT), lambda i: (i, 0)),
        out_shape=jax.ShapeDtypeStruct((dec_batch * dec_seq, ATT_OUT), F32),
        scratch_shapes=[pltpu.VMEM((HPG * dec_seq, p), F32) for p in pasts]
                       + [pltpu.VMEM((N_ATT, HPG * dec_seq, dec_seq), F32)],
        compiler_params=_params(1),
        name="attn_sample",
    )(bias_flat, q, k, v, *caches_t)


def _merge_kernel(alpha, x_ref, mab_ref, gc_ref, yc_ref, wuc_ref, wo_ref, g_ref, b_ref, wrt_ref, br_ref,
                  x1_ref, x1g_ref, idx_ref, w_ref, rank_ref, cnt_ref, cnt_scr):
    @pl.when(pl.program_id(0) == 0)
    def _():
        cnt_scr[...] = jnp.zeros_like(cnt_scr)

    merged = mab_ref[...] + gc_ref[...] * _dot(yc_ref[...], wuc_ref[...])
    x1 = _layer_norm(alpha * x_ref[...] + _dot(merged, wo_ref[...]), g_ref[...], b_ref[...])
    x1_ref[...] = x1
    for c in range(LANE_CHUNKS):
        x1g_ref[pl.ds(c, TM, stride=LANE_CHUNKS), :] = x1[:, c * LANES:(c + 1) * LANES]

    logits = lax.dot_general(wrt_ref[...], x1, (((1,), (1,)), ((), ())),
                             precision=lax.Precision.HIGHEST, preferred_element_type=F32) + br_ref[...]
    eio = lax.broadcasted_iota(jnp.int32, (N_EXPERTS, TM), 0)
    vals, idxs = [], []
    for _ in range(TOP_K):
        mv = jnp.max(logits, axis=0, keepdims=True)
        ix = jnp.min(jnp.where(logits == mv, eio, N_EXPERTS), axis=0, keepdims=True)
        vals.append(mv)
        idxs.append(ix)
        logits = jnp.where(eio == ix, -jnp.inf, logits)
    tv = jnp.concatenate(vals, axis=0)
    e = jnp.exp(tv - tv[0:1, :])
    w_ref[...] = e / jnp.sum(e, axis=0, keepdims=True)
    idx_ref[...] = jnp.concatenate(idxs, axis=0)

    onehots = [eio == ix for ix in idxs]
    chosen = functools.reduce(lambda a, c: a + c, [jnp.where(oh, 1.0, 0.0) for oh in onehots])
    earlier = (lax.broadcasted_iota(jnp.int32, (TM, TM), 0) < lax.broadcasted_iota(jnp.int32, (TM, TM), 1))
    prefix = _dot(chosen, jnp.where(earlier, 1.0, 0.0)) + cnt_scr[:, 0:1]
    rank_ref[...] = jnp.concatenate(
        [jnp.sum(jnp.where(oh, prefix, 0.0), axis=0, keepdims=True) for oh in onehots], axis=0).astype(jnp.int32)
    cnt_scr[...] = cnt_scr[...] + jnp.sum(chosen, axis=1, keepdims=True)
    cnt_ref[...] = cnt_scr[...].astype(jnp.int32)


def _merge_call(l, alpha, x, mab, gc, yc, w_up_c, w_o, ln_g, ln_b, w_router_t, b_router):
    n = x.shape[0]
    row_spec = lambda w: pl.BlockSpec((TM, w), lambda i: (i, 0))
    vec_spec = pl.BlockSpec((None, 1, D_MODEL), lambda i: (l, 0, 0))
    return pl.pallas_call(
        functools.partial(_merge_kernel, alpha),
        grid=(n // TM,),
        in_specs=[
            row_spec(D_MODEL), row_spec(D_MODEL), row_spec(D_MODEL), row_spec(ATT_OUT),
            pl.BlockSpec((None, ATT_OUT, D_MODEL), lambda i: (l, 0, 0)),
            pl.BlockSpec((None, D_MODEL, D_MODEL), lambda i: (l, 0, 0)),
            vec_spec, vec_spec,
            pl.BlockSpec((None, N_EXPERTS, D_MODEL), lambda i: (l, 0, 0)),
            pl.BlockSpec((None, N_EXPERTS, 1), lambda i: (l, 0, 0)),
        ],
        out_specs=[
            row_spec(D_MODEL),
            pl.BlockSpec((TM * LANE_CHUNKS, LANES), lambda i: (i, 0)),
            pl.BlockSpec((TOP_K, TM), lambda i: (0, i)),
            pl.BlockSpec((TOP_K, TM), lambda i: (0, i)),
            pl.BlockSpec((TOP_K, TM), lambda i: (0, i)),
            pl.BlockSpec((N_EXPERTS, LANES), lambda i: (0, 0)),
        ],
        out_shape=[
            jax.ShapeDtypeStruct((n, D_MODEL), F32),
            jax.ShapeDtypeStruct((n * LANE_CHUNKS, LANES), F32),
            jax.ShapeDtypeStruct((TOP_K, n), jnp.int32),
            jax.ShapeDtypeStruct((TOP_K, n), F32),
            jax.ShapeDtypeStruct((TOP_K, n), jnp.int32),
            jax.ShapeDtypeStruct((N_EXPERTS, LANES), jnp.int32),
        ],
        scratch_shapes=[pltpu.VMEM((N_EXPERTS, LANES), F32)],
        compiler_params=_params(1),
        name="merge_ln1_router",
    )(x, mab, gc, yc, w_up_c, w_o, ln_g, ln_b, w_router_t, b_router)


def _token_slab_copy(src_hbm, row, dst, slot, sem):
    return pltpu.make_async_copy(src_hbm.at[pl.ds(row * LANE_CHUNKS, LANE_CHUNKS), :],
                                 dst.at[pl.ds(slot * LANE_CHUNKS, LANE_CHUNKS), :], sem)


def _destride(ref, rows):
    return jnp.concatenate([ref[pl.ds(c, rows, stride=LANE_CHUNKS), :] for c in range(LANE_CHUNKS)], axis=1)


def _lane_row_to_column(w_row, n):
    r = lax.broadcasted_iota(jnp.int32, (n, n), 0)
    c = lax.broadcasted_iota(jnp.int32, (n, n), 1)
    return jnp.sum(jnp.where(r == c, jnp.broadcast_to(w_row, (n, n)), 0.0), axis=1, keepdims=True)


def _dispatch_kernel(n_tiles, lt_ref, has_ref, nv_ref, pos_ref, x1g_ref, xs_hbm, zbuf, zsem, sem):
    tile_rows = TM_E * LANE_CHUNKS

    @pl.when(pl.program_id(0) == 0)
    def _():
        zbuf[...] = jnp.zeros_like(zbuf)
        zero_copy = lambda t: pltpu.make_async_copy(zbuf, xs_hbm.at[pl.ds(t * tile_rows, tile_rows), :], zsem)
        for e in range(N_EXPERTS):
            @pl.when(has_ref[e] > 0)
            def _(e=e):
                zero_copy(lt_ref[e]).start()
        lax.fori_loop(nv_ref[0], n_tiles, lambda t, c: (zero_copy(t).start(), c)[1], 0)
        for e in range(N_EXPERTS):
            @pl.when(has_ref[e] > 0)
            def _(e=e):
                zero_copy(lt_ref[e]).wait()
        lax.fori_loop(nv_ref[0], n_tiles, lambda t, c: (zero_copy(t).wait(), c)[1], 0)

    def issue(r, carry):
        src = x1g_ref.at[pl.ds(r * LANE_CHUNKS, LANE_CHUNKS), :]
        for kk in range(TOP_K):
            pltpu.make_async_copy(src, xs_hbm.at[pl.ds(pos_ref[kk, r] * LANE_CHUNKS, LANE_CHUNKS), :], sem).start()
        return carry

    lax.fori_loop(0, TM, issue, 0, unroll=DMA_UNROLL)
    for kk in range(TOP_K):
        pltpu.make_async_copy(x1g_ref, xs_hbm.at[pl.ds(0, TM * LANE_CHUNKS), :], sem).wait()


def _dispatch_call(last_tile, has_rows, n_valid, pos, x1g, n_tiles):
    n = pos.shape[1]
    grid_spec = pltpu.PrefetchScalarGridSpec(
        num_scalar_prefetch=3,
        grid=(n // TM,),
        in_specs=[
            pl.BlockSpec((TOP_K, TM), lambda i, lt, has, nv: (0, i), memory_space=pltpu.SMEM),
            pl.BlockSpec((TM * LANE_CHUNKS, LANES), lambda i, lt, has, nv: (i, 0)),
        ],
        out_specs=pl.BlockSpec(memory_space=pl.ANY),
        scratch_shapes=[pltpu.VMEM((TM_E * LANE_CHUNKS, LANES), F32),
                        pltpu.SemaphoreType.DMA(()), pltpu.SemaphoreType.DMA(())],
    )
    return pl.pallas_call(
        functools.partial(_dispatch_kernel, n_tiles),
        grid_spec=grid_spec,
        out_shape=jax.ShapeDtypeStruct((n_tiles * TM_E * LANE_CHUNKS, LANES), F32),
        compiler_params=_params(1),
        name="moe_dispatch",
    )(last_tile, has_rows, n_valid, pos, x1g)


def _moe_ffn_kernel(te_ref, nv_ref, xs_ref, wg_ref, bg_ref, wu_ref, bu_ref, wd_ref, bd_ref,
                    ys_ref, wg_b, wu_b, wd_b):
    t = pl.program_id(0)

    @pl.when(t < nv_ref[0])
    def _():
        @pl.when(jnp.logical_or(t == 0, te_ref[t] != te_ref[jnp.maximum(t - 1, 0)]))
        def _():
            wg_b[...] = wg_ref[...].astype(BF16)
            wu_b[...] = wu_ref[...].astype(BF16)
            wd_b[...] = wd_ref[...].astype(BF16)

        x = _destride(xs_ref, TM_E).astype(BF16)
        gl = jnp.minimum(jnp.dot(x, wg_b[...], preferred_element_type=F32) + bg_ref[...], SWIGLU_LIMIT)
        ul = jnp.clip(jnp.dot(x, wu_b[...], preferred_element_type=F32) + bu_ref[...], -SWIGLU_LIMIT, SWIGLU_LIMIT)
        hid = gl * _sigmoid(SWIGLU_ALPHA * gl) * (ul + 1.0)
        y = jnp.dot(hid.astype(BF16), wd_b[...], preferred_element_type=F32) + bd_ref[...]
        for c in range(LANE_CHUNKS):
            ys_ref[pl.ds(c, TM_E, stride=LANE_CHUNKS), :] = y[:, c * LANES:(c + 1) * LANES]

    @pl.when(t >= nv_ref[0])
    def _():
        ys_ref[...] = jnp.zeros_like(ys_ref)


def _moe_ffn_call(l, tile_expert, n_valid, xs, w_gate, b_gate, w_up, b_up, w_down, b_down):
    n_tiles = tile_expert.shape[0]
    mat_spec = pl.BlockSpec((None, None, D_MODEL, D_MODEL), lambda t, te, nv: (l, te[t], 0, 0))
    vec_spec = pl.BlockSpec((None, None, 1, D_MODEL), lambda t, te, nv: (l, te[t], 0, 0))
    in_slab = pl.BlockSpec((TM_E * LANE_CHUNKS, LANES), lambda t, te, nv: (jnp.minimum(t, nv[0] - 1), 0))
    grid_spec = pltpu.PrefetchScalarGridSpec(
        num_scalar_prefetch=2,
        grid=(n_tiles,),
        in_specs=[in_slab, mat_spec, vec_spec, mat_spec, vec_spec, mat_spec, vec_spec],
        out_specs=pl.BlockSpec((TM_E * LANE_CHUNKS, LANES), lambda t, te, nv: (t, 0)),
        scratch_shapes=[pltpu.VMEM((D_MODEL, D_MODEL), BF16)] * 3,
    )
    return pl.pallas_call(
        _moe_ffn_kernel,
        grid_spec=grid_spec,
        out_shape=jax.ShapeDtypeStruct((n_tiles * TM_E * LANE_CHUNKS, LANES), F32),
        compiler_params=_params(1),
        name="moe_ffn",
    )(tile_expert, n_valid, xs, w_gate, b_gate, w_up, b_up, w_down, b_down)


def _combine_kernel(alpha, pos_ref, w_ref, x1_ref, ys_hbm, g_ref, b_ref, x2_ref, buf, sem):
    def issue(r, carry):
        for kk in range(TOP_K):
            _token_slab_copy(ys_hbm, pos_ref[kk, r], buf.at[kk], r, sem).start()
        return carry

    lax.fori_loop(0, TM, issue, 0, unroll=DMA_UNROLL)
    for kk in range(TOP_K):
        pltpu.make_async_copy(ys_hbm.at[pl.ds(0, TM * LANE_CHUNKS), :], buf.at[kk], sem).wait()
    y = alpha * x1_ref[...]
    for kk in range(TOP_K):
        y = y + _lane_row_to_column(w_ref[kk:kk + 1, :], TM) * _destride(buf.at[kk], TM)
    x2_ref[...] = _layer_norm(y, g_ref[...], b_ref[...])


def _combine_call(l, alpha, pos, top_w, x1, ys, ln_g, ln_b):
    n = x1.shape[0]
    vec_spec = pl.BlockSpec((None, 1, D_MODEL), lambda i: (l, 0, 0))
    return pl.pallas_call(
        functools.partial(_combine_kernel, alpha),
        grid=(n // TM,),
        in_specs=[
            pl.BlockSpec((TOP_K, TM), lambda i: (0, i), memory_space=pltpu.SMEM),
            pl.BlockSpec((TOP_K, TM), lambda i: (0, i)),
            pl.BlockSpec((TM, D_MODEL), lambda i: (i, 0)),
            pl.BlockSpec(memory_space=pl.ANY),
            vec_spec, vec_spec,
        ],
        out_specs=pl.BlockSpec((TM, D_MODEL), lambda i: (i, 0)),
        out_shape=jax.ShapeDtypeStruct((n, D_MODEL), F32),
        scratch_shapes=[pltpu.VMEM((TOP_K, TM * LANE_CHUNKS, LANES), F32),
                        pltpu.SemaphoreType.DMA(())],
        compiler_params=_params(1),
        name="moe_combine_ln2",
    )(pos, top_w, x1, ys, ln_g, ln_b)


def _route(top_idx, rank, counts, n_tiles):
    experts = jnp.arange(N_EXPERTS, dtype=jnp.int32)
    padded = ((counts + TM_E - 1) // TM_E) * TM_E
    ends = jnp.cumsum(padded)
    offs = ends - padded
    pos = rank + jnp.sum(jnp.where(top_idx[..., None] == experts, offs, 0), axis=-1)
    n_valid = ends[-1] // TM_E
    last_expert = jnp.max(jnp.where(counts > 0, experts, 0))
    tile_start = jnp.arange(n_tiles, dtype=jnp.int32) * TM_E
    tile_expert = jnp.minimum(jnp.sum(ends[None, :] <= tile_start[:, None], axis=1), last_expert)
    return (tile_expert.astype(jnp.int32), n_valid.astype(jnp.int32).reshape(1), pos.astype(jnp.int32),
            (ends // TM_E - 1).astype(jnp.int32), (counts > 0).astype(jnp.int32))


def _t5_bucket(dist):
    max_exact = NUM_BUCKETS // 2
    distf = jnp.maximum(dist, 1).astype(F32)
    large = max_exact + (jnp.log(distf / max_exact) / math.log(MAX_DISTANCE / max_exact)
                         * (NUM_BUCKETS - max_exact)).astype(jnp.int32)
    large = jnp.minimum(large, NUM_BUCKETS - 1)
    return jnp.where(dist < max_exact, dist, large)


def _group_bias(rel_bias, gi, dil):
    j = jnp.arange(N_KEYS, dtype=jnp.int32)
    return rel_bias[_t5_bucket(dil * j)][:, gi * HPG:(gi + 1) * HPG].T.astype(F32)


def _step_bias(rel_bias):
    return jnp.stack([_group_bias(rel_bias, gi, dil) for gi, (_, dil) in enumerate(ATT_GROUPS)], axis=0).reshape(-1)


def _spatial_tables(w_sp, b_sp, dec_seq):
    depth = w_sp.shape[0]
    tril = jnp.tril(jnp.ones((CHUNK, CHUNK), bool))
    wp = jnp.where(tril, w_sp, 0.0)
    ws_small = jnp.where(tril[:dec_seq, :dec_seq], w_sp[:, :, :dec_seq, :dec_seq], 0.0)
    eye = jnp.eye(CHUNK // dec_seq, dtype=w_sp.dtype)
    ws = jnp.einsum("ab,lgij->lgaibj", eye, ws_small).reshape(depth, SG_GROUPS, CHUNK, CHUNK)
    bp = jnp.repeat(jnp.swapaxes(b_sp, 1, 2), SG_HEAD, axis=2)
    bs = jnp.tile(bp[:, :dec_seq], (1, CHUNK // dec_seq, 1))
    return jnp.stack([wp, ws], axis=1).astype(BF16), jnp.stack([bp, bs], axis=1)


def kernel(x_prompt, x_sample, state_conv, cache_k_w128, cache_v_w128, cache_k_w512, cache_v_w512,
           cache_k_w2048, cache_v_w2048, w_in, conv_w, sg_ln_g, sg_ln_b, w_sp, b_sp, rel_bias,
           w_up_a, w_up_b, w_up_c, w_o, ln1_g, ln1_b, w_router, b_router, w_gate, b_gate, w_up, b_up,
           w_down, b_down, ln2_g, ln2_b):
    batch, seq, _ = x_prompt.shape
    dec_batch, dec_seq, _ = x_sample.shape
    depth = w_in.shape[0]
    n_prompt, n_sample = batch * seq, dec_batch * dec_seq
    n = n_prompt + n_sample
    alpha = (2 * depth) ** 0.25
    assert seq % TM == 0 and n_sample % TM == 0 and dec_seq & (dec_seq - 1) == 0 and CHUNK % dec_seq == 0
    assert all(seq % (dil * Q_BLK) == 0 and dil & (dil - 1) == 0 for _, dil in ATT_GROUPS)
    assert dec_batch % SEQ_BLK == 0
    n_tiles = (n * TOP_K) // TM_E + N_EXPERTS

    caches_t = [jnp.transpose(c, (0, 1, 3, 4, 2)).reshape(c.shape[0], c.shape[1], LANES, c.shape[2]) for c in
                (cache_k_w128, cache_v_w128, cache_k_w512, cache_v_w512, cache_k_w2048, cache_v_w2048)]
    bias_flat = _step_bias(rel_bias)
    wsp2, bsp2 = _spatial_tables(w_sp, b_sp, dec_seq)
    w_in_b, w_up_a_b, w_up_b_b, w_up_c_b, w_o_b = (w.astype(BF16) for w in (w_in, w_up_a, w_up_b, w_up_c, w_o))
    vec = lambda a: a.reshape(depth, 1, a.shape[-1])
    w_router_t = jnp.swapaxes(w_router, 1, 2)
    b_router_c = b_router.reshape(depth, N_EXPERTS, 1)
    expert_vec = lambda a: a.reshape(depth, N_EXPERTS, 1, a.shape[-1])

    x = jnp.concatenate([x_prompt.reshape(n_prompt, D_MODEL), x_sample.reshape(n_sample, D_MODEL)], axis=0)
    outs = {name: [] for name in ("conv_p", "conv_s", "sgv", "kp", "vp", "ks", "vs")}
    for l in range(depth):
        s1 = jnp.zeros((dec_batch, dec_seq, CONV_DIM), F32).at[:, 0].set(state_conv[l, :, 1])
        s2 = jnp.zeros((dec_batch, dec_seq, CONV_DIM), F32).at[:, 0].set(state_conv[l, :, 0])
        s2 = s2.at[:, 1].set(state_conv[l, :, 1])
        q, k, v, mab, gc, tail, z_s, v_rows = _inproj_call(
            l, x, w_in_b, conv_w, vec(sg_ln_g), vec(sg_ln_b), wsp2, bsp2, w_up_a_b, w_up_b_b,
            s1.reshape(n_sample, CONV_DIM), s2.reshape(n_sample, CONV_DIM), n_prompt, seq, n_sample, dec_seq)
        yc_p = _attn_prompt_call(q, k, v, bias_flat, batch, seq)
        yc_s = _attn_sample_call(l, q, k, v, caches_t, bias_flat, n_prompt, dec_batch, dec_seq)
        yc = jnp.concatenate([yc_p, yc_s], axis=0)
        x1, x1g, top_idx, top_w, rank, counts = _merge_call(
            l, alpha, x, mab, gc, yc, w_up_c_b, w_o_b, vec(ln1_g), vec(ln1_b), w_router_t, b_router_c)
        tile_expert, n_valid, pos, last_tile, has_rows = _route(top_idx, rank, counts[:, 0], n_tiles)
        xs = _dispatch_call(last_tile, has_rows, n_valid, pos, x1g, n_tiles)
        ys = _moe_ffn_call(l, tile_expert, n_valid, xs, w_gate, expert_vec(b_gate),
                           w_up, expert_vec(b_up), w_down, expert_vec(b_down))
        x = _combine_call(l, alpha, pos, top_w, x1, ys, vec(ln2_g), vec(ln2_b))

        outs["conv_p"].append(tail[:, SUBLANES - (CONV_W - 1):])
        outs["conv_s"].append(z_s.reshape(dec_batch, dec_seq, CONV_DIM)[:, dec_seq - (CONV_W - 1):])
        outs["sgv"].append(v_rows.reshape(dec_batch, dec_seq, SG_DIM))
        kp4 = k[:n_prompt].reshape(batch, seq, N_ATT, HPG, HEAD_DIM)
        vp4 = v[:n_prompt].reshape(batch, seq, N_ATT, HPG, HEAD_DIM)
        ks4 = k[n_prompt:].reshape(dec_batch, dec_seq, N_ATT, HPG, HEAD_DIM)
        vs4 = v[n_prompt:].reshape(dec_batch, dec_seq, N_ATT, HPG, HEAD_DIM)
        outs["kp"].append([kp4[:, seq - min(win, seq):, gi] for gi, (win, _) in enumerate(ATT_GROUPS)])
        outs["vp"].append([vp4[:, seq - min(win, seq):, gi] for gi, (win, _) in enumerate(ATT_GROUPS)])
        outs["ks"].append([ks4[:, dec_seq - min(win, dec_seq):, gi] for gi, (win, _) in enumerate(ATT_GROUPS)])
        outs["vs"].append([vs4[:, dec_seq - min(win, dec_seq):, gi] for gi, (win, _) in enumerate(ATT_GROUPS)])

    stack = lambda name: jnp.stack(outs[name], axis=0)
    per_group = lambda name, gi: jnp.stack([layer[gi] for layer in outs[name]], axis=0)
    result = [x[:n_prompt].reshape(batch, seq, D_MODEL), x[n_prompt:].reshape(dec_batch, dec_seq, D_MODEL),
              stack("conv_p"), stack("conv_s")]
    for gi in range(N_ATT):
        result += [per_group("kp", gi), per_group("vp", gi)]
    for gi in range(N_ATT):
        result += [per_group("ks", gi), per_group("vs", gi)]
    result.append(stack("sgv"))
    return tuple(result)
```

```python
import functools
import math

import jax
import jax.numpy as jnp
from jax import lax
from jax.experimental import pallas as pl
from jax.experimental.pallas import tpu as pltpu

D_MODEL = 1024
CONV_DIM = 384
CONV_W = 3
SG_DIM = 256
SG_GROUPS = 4
SG_HEAD = SG_DIM // SG_GROUPS
CHUNK = 128
ATT_GROUPS = ((128, 1), (512, 4), (2048, 16))
N_ATT = len(ATT_GROUPS)
HPG = 4
HEAD_DIM = 32
ATT_DIM = HPG * N_ATT * HEAD_DIM
ATT_OUT = HPG * HEAD_DIM
N_KEYS = 129
NUM_BUCKETS = 32
MAX_DISTANCE = 2048
N_EXPERTS = 32
TOP_K = 4
SWIGLU_LIMIT = 7.0
SWIGLU_ALPHA = 1.702
LN_EPS = 1e-5
OFF_A = 0
OFF_B = OFF_A + 3 * CONV_DIM
OFF_C = OFF_B + 2 * SG_DIM
OFF_G = OFF_C + 3 * ATT_DIM
IN_DIM = OFF_G + 3 * D_MODEL
NEG = -1e30

LANES = 128
SUBLANES = 8
LANE_CHUNKS = D_MODEL // LANES
VMEM_LIMIT = 56 * 1024 * 1024

TM = 256
TM_E = 256
Q_BLK = 128
SEQ_BLK = 4
DMA_UNROLL = 8

F32 = jnp.float32
BF16 = jnp.bfloat16


def _dot(a, b):
    return jnp.dot(a.astype(BF16), b.astype(BF16), preferred_element_type=F32)


def _dot_nt(a, b):
    return lax.dot_general(a.astype(BF16), b.astype(BF16), (((1,), (1,)), ((), ())),
                           preferred_element_type=F32)


def _layer_norm(x, g, b):
    mu = jnp.mean(x, axis=-1, keepdims=True)
    xc = x - mu
    var = jnp.mean(xc * xc, axis=-1, keepdims=True)
    return xc * lax.rsqrt(var + LN_EPS) * g + b


def _gelu(x):
    return 0.5 * x * (1.0 + lax.erf(x * (2.0 ** -0.5)))


def _sigmoid(x):
    return 1.0 / (1.0 + jnp.exp(-x))


def _params(n_axes):
    return pltpu.CompilerParams(dimension_semantics=("arbitrary",) * n_axes,
                                vmem_limit_bytes=VMEM_LIMIT)


def _inproj_kernel(n_ptiles, tiles_per_seq, dec_seq,
                   x_ref, w_in_ref, convw_ref, lng_ref, lnb_ref, wsp_ref, bsp_ref, wua_ref, wub_ref,
                   s1_ref, s2_ref,
                   q_ref, k_ref, v_ref, mab_ref, gc_ref, tail_ref, zs_ref, vrows_ref, carry_ref):
    i = pl.program_id(0)
    is_prompt = i < n_ptiles

    @pl.when(jnp.logical_and(is_prompt, i % tiles_per_seq == 0))
    def _():
        carry_ref[...] = jnp.zeros_like(carry_ref)

    x = x_ref[...].astype(BF16)

    pa = jnp.dot(x, w_in_ref[:, OFF_A:OFF_B], preferred_element_type=F32)
    bg, cg, h = pa[:, :CONV_DIM], pa[:, CONV_DIM:2 * CONV_DIM], pa[:, 2 * CONV_DIM:]
    z = cg * h
    row = lax.broadcasted_iota(jnp.int32, (TM, CONV_DIM), 0)
    pos = jnp.where(is_prompt, row, row & (dec_seq - 1))
    prev1 = jnp.where(is_prompt, jnp.broadcast_to(carry_ref[7:8, :], (TM, CONV_DIM)), s1_ref[...])
    prev2 = jnp.where(is_prompt,
                      jnp.where(row == 0, jnp.broadcast_to(carry_ref[6:7, :], (TM, CONV_DIM)),
                                jnp.broadcast_to(carry_ref[7:8, :], (TM, CONV_DIM))),
                      s2_ref[...])
    z1 = jnp.where(pos >= 1, pltpu.roll(z, 1, axis=0), prev1)
    z2 = jnp.where(pos >= 2, pltpu.roll(z, 2, axis=0), prev2)
    cw = convw_ref[...]
    ya = bg * (cw[0:1, :] * z2 + cw[1:2, :] * z1 + cw[2:3, :] * z)

    @pl.when(is_prompt)
    def _():
        carry_ref[...] = z[TM - SUBLANES:, :]
        tail_ref[0] = z[TM - SUBLANES:, :]

    @pl.when(jnp.logical_not(is_prompt))
    def _():
        zs_ref[...] = z

    pb = jnp.dot(x, w_in_ref[:, OFF_B:OFF_C], preferred_element_type=F32)
    uv = _gelu(pb)
    u = uv[:, :SG_DIM]
    vn = _layer_norm(uv[:, SG_DIM:], lng_ref[...], lnb_ref[...])

    @pl.when(jnp.logical_not(is_prompt))
    def _():
        vrows_ref[...] = vn

    vb = vn.astype(BF16)
    low_half = lax.broadcasted_iota(jnp.int32, (CHUNK, LANES), 1) < SG_HEAD
    s_rows = []
    for c in range(TM // CHUNK):
        s_cols = []
        for p in range(SG_DIM // LANES):
            vcp = vb[c * CHUNK:(c + 1) * CHUNK, p * LANES:(p + 1) * LANES]
            sa = jnp.dot(wsp_ref[2 * p], vcp, preferred_element_type=F32)
            sb = jnp.dot(wsp_ref[2 * p + 1], vcp, preferred_element_type=F32)
            s_cols.append(jnp.where(low_half, sa, sb))
        s_rows.append(jnp.concatenate(s_cols, axis=1) + bsp_ref[...])
    yb = u * jnp.concatenate(s_rows, axis=0)

    pc = jnp.dot(x, w_in_ref[:, OFF_C:OFF_G], preferred_element_type=F32)
    q_ref[...] = pc[:, :ATT_DIM]
    k_ref[...] = pc[:, ATT_DIM:2 * ATT_DIM]
    v_ref[...] = pc[:, 2 * ATT_DIM:]

    ga = _sigmoid(jnp.dot(x, w_in_ref[:, OFF_G:OFF_G + D_MODEL], preferred_element_type=F32))
    mab = ga * _dot(ya, wua_ref[...])
    gb = _sigmoid(jnp.dot(x, w_in_ref[:, OFF_G + D_MODEL:OFF_G + 2 * D_MODEL], preferred_element_type=F32))
    mab_ref[...] = mab + gb * _dot(yb, wub_ref[...])
    gc_ref[...] = _sigmoid(jnp.dot(x, w_in_ref[:, OFF_G + 2 * D_MODEL:], preferred_element_type=F32))


def _inproj_call(l, x, w_in, conv_w, sg_ln_g, sg_ln_b, wsp2, bsp2, w_up_a, w_up_b, s1, s2,
                 n_prompt, seq, n_sample, dec_seq):
    n = n_prompt + n_sample
    n_ptiles = n_prompt // TM
    n_stiles = n_sample // TM
    tiles_per_seq = seq // TM
    batch = n_prompt // seq

    def stile(i):
        return jnp.maximum(i - n_ptiles, 0)

    def mode(i):
        return jnp.where(i < n_ptiles, 0, 1)

    row_spec = lambda w: pl.BlockSpec((TM, w), lambda i: (i, 0))
    srow_spec = lambda w: pl.BlockSpec((TM, w), lambda i: (stile(i), 0))
    kern = functools.partial(_inproj_kernel, n_ptiles, tiles_per_seq, dec_seq)
    return pl.pallas_call(
        kern,
        grid=(n_ptiles + n_stiles,),
        in_specs=[
            row_spec(D_MODEL),
            pl.BlockSpec((None, D_MODEL, IN_DIM), lambda i: (l, 0, 0), pipeline_mode=pl.Buffered(1)),
            pl.BlockSpec((None, CONV_W, CONV_DIM), lambda i: (l, 0, 0)),
            pl.BlockSpec((None, 1, SG_DIM), lambda i: (l, 0, 0)),
            pl.BlockSpec((None, 1, SG_DIM), lambda i: (l, 0, 0)),
            pl.BlockSpec((None, None, SG_GROUPS, CHUNK, CHUNK), lambda i: (l, mode(i), 0, 0, 0)),
            pl.BlockSpec((None, None, CHUNK, SG_DIM), lambda i: (l, mode(i), 0, 0)),
            pl.BlockSpec((None, CONV_DIM, D_MODEL), lambda i: (l, 0, 0)),
            pl.BlockSpec((None, SG_DIM, D_MODEL), lambda i: (l, 0, 0)),
            srow_spec(CONV_DIM),
            srow_spec(CONV_DIM),
        ],
        out_specs=[
            row_spec(ATT_DIM), row_spec(ATT_DIM), row_spec(ATT_DIM),
            row_spec(D_MODEL), row_spec(D_MODEL),
            pl.BlockSpec((1, SUBLANES, CONV_DIM),
                         lambda i: (jnp.minimum(i // tiles_per_seq, batch - 1), 0, 0)),
            srow_spec(CONV_DIM),
            srow_spec(SG_DIM),
        ],
        out_shape=[
            jax.ShapeDtypeStruct((n, ATT_DIM), F32), jax.ShapeDtypeStruct((n, ATT_DIM), F32),
            jax.ShapeDtypeStruct((n, ATT_DIM), F32),
            jax.ShapeDtypeStruct((n, D_MODEL), F32), jax.ShapeDtypeStruct((n, D_MODEL), F32),
            jax.ShapeDtypeStruct((batch, SUBLANES, CONV_DIM), F32),
            jax.ShapeDtypeStruct((n_sample, CONV_DIM), F32),
            jax.ShapeDtypeStruct((n_sample, SG_DIM), F32),
        ],
        scratch_shapes=[pltpu.VMEM((SUBLANES, CONV_DIM), F32)],
        compiler_params=_params(1),
        name="inproj_mix",
    )(x, w_in, conv_w, sg_ln_g, sg_ln_b, wsp2, bsp2, w_up_a, w_up_b, s1, s2)


def _head_masks(rows):
    lane_head = lax.broadcasted_iota(jnp.int32, (rows, LANES), 1) // HEAD_DIM
    return [lane_head == h for h in range(HPG)]


def _stack_heads(qb, masks):
    return jnp.concatenate([jnp.where(m, qb, 0.0) for m in masks], axis=0)


def _unstack_heads(stacked, masks, r):
    out = None
    for h, m in enumerate(masks):
        part = jnp.where(m, stacked[h * r:(h + 1) * r], 0.0)
        out = part if out is None else out + part
    return out


def _mix_groups(outs, lses):
    mx = functools.reduce(jnp.maximum, lses)
    ws = [jnp.exp(ls - mx) for ls in lses]
    num = functools.reduce(lambda a, b: a + b, [w * o for w, o in zip(ws, outs)])
    return num / functools.reduce(lambda a, b: a + b, ws)


def _bias_table(bias_ref, base, steps):
    def body(j, tab):
        return jnp.where(steps == j, bias_ref[base + j], tab)

    return lax.fori_loop(0, N_KEYS, body, jnp.full(steps.shape, NEG, F32))


def _attn_prompt_kernel(seq, bias_ref, q_ref, k_ref, v_ref, o_ref, tab_scr, *scr):
    b = pl.program_id(0)
    g = pl.program_id(1)
    o_scr, l_scr = scr[:N_ATT], scr[N_ATT:]
    masks = _head_masks(Q_BLK)
    scale = HEAD_DIM ** -0.5
    col = lax.broadcasted_iota(jnp.int32, (HPG * Q_BLK, 2 * Q_BLK), 1)

    def build_table(gi):
        qi = lax.broadcasted_iota(jnp.int32, (Q_BLK, 2 * Q_BLK), 0)
        kc = lax.broadcasted_iota(jnp.int32, (Q_BLK, 2 * Q_BLK), 1)
        for h in range(HPG):
            tab_scr[gi, h * Q_BLK:(h + 1) * Q_BLK, :] = _bias_table(bias_ref, (gi * HPG + h) * N_KEYS,
                                                                    qi + Q_BLK - kc)

    def run_group(gi, dil):
        rows_per_class = seq // dil
        n_blk = rows_per_class // Q_BLK

        def ld(ref, start):
            if dil > 1:
                return ref[pl.ds(start, Q_BLK, stride=dil), :]
            return ref[pl.ds(start, Q_BLK), :]

        def st(ref, start, val):
            if dil > 1:
                ref[pl.ds(start, Q_BLK, stride=dil), :] = val
            else:
                ref[pl.ds(start, Q_BLK), :] = val

        def block(it, carry):
            r = it // n_blk
            mb = it % n_blk
            cur = r + dil * Q_BLK * mb
            prev = r + dil * Q_BLK * jnp.maximum(mb - 1, 0)
            qb = ld(q_ref, cur) * scale
            kw = jnp.concatenate([ld(k_ref, prev), ld(k_ref, cur)], axis=0)
            vw = jnp.concatenate([ld(v_ref, prev), ld(v_ref, cur)], axis=0)
            s = _dot_nt(_stack_heads(qb, masks), kw) + tab_scr[gi]
            s = jnp.where(jnp.logical_or(col >= Q_BLK, mb > 0), s, NEG)
            m = jnp.max(s, axis=1, keepdims=True)
            p = jnp.exp(s - m)
            den = jnp.sum(p, axis=1, keepdims=True)
            pv = _dot(p, vw)
            st(o_scr[gi], cur, _unstack_heads(pv / den, masks, Q_BLK))
            st(l_scr[gi], cur, _unstack_heads(m + jnp.log(den), masks, Q_BLK))
            return carry

        lax.fori_loop(0, dil * n_blk, block, 0)

    for gi, (_, dil) in enumerate(ATT_GROUPS):
        @pl.when(jnp.logical_and(g == gi, b == 0))
        def _(gi=gi):
            build_table(gi)

        @pl.when(g == gi)
        def _(gi=gi, dil=dil):
            run_group(gi, dil)

    @pl.when(g == N_ATT - 1)
    def _():
        o_ref[...] = _mix_groups([s[...] for s in o_scr], [s[...] for s in l_scr])


def _attn_prompt_call(q, k, v, bias_flat, batch, seq):
    blk = lambda: pl.BlockSpec((seq, LANES), lambda b, g: (b, g))
    return pl.pallas_call(
        functools.partial(_attn_prompt_kernel, seq),
        grid=(batch, N_ATT),
        in_specs=[pl.BlockSpec(memory_space=pltpu.SMEM), blk(), blk(), blk()],
        out_specs=pl.BlockSpec((seq, ATT_OUT), lambda b, g: (b, 0)),
        out_shape=jax.ShapeDtypeStruct((batch * seq, ATT_OUT), F32),
        scratch_shapes=[pltpu.VMEM((N_ATT, HPG * Q_BLK, 2 * Q_BLK), F32)]
                       + [pltpu.VMEM((seq, LANES), F32)] * (2 * N_ATT),
        compiler_params=_params(2),
        name="attn_prompt",
    )(bias_flat, q, k, v)


def _attn_sample_kernel(dec_seq, pasts, bias_ref, q_ref, k_ref, v_ref, kc0, vc0, kc1, vc1, kc2, vc2,
                        o_ref, tc0, tc1, tc2, tn_scr):
    kcs, vcs, tcs = (kc0, kc1, kc2), (vc0, vc1, vc2), (tc0, tc1, tc2)
    masks = _head_masks(dec_seq)
    scale = HEAD_DIM ** -0.5

    @pl.when(pl.program_id(0) == 0)
    def _():
        for gi, (_, dil) in enumerate(ATT_GROUPS):
            past = pasts[gi]
            shift = dil.bit_length() - 1

            def steps(dist, dil=dil, shift=shift):
                return jnp.where((dist & (dil - 1)) == 0, dist >> shift, -1)

            qi_c = lax.broadcasted_iota(jnp.int32, (dec_seq, past), 0)
            row_c = lax.broadcasted_iota(jnp.int32, (dec_seq, past), 1)
            qi_n = lax.broadcasted_iota(jnp.int32, (dec_seq, dec_seq), 0)
            row_n = lax.broadcasted_iota(jnp.int32, (dec_seq, dec_seq), 1)
            for h in range(HPG):
                base = (gi * HPG + h) * N_KEYS
                tcs[gi][h * dec_seq:(h + 1) * dec_seq, :] = _bias_table(bias_ref, base, steps(past + qi_c - row_c))
                tn_scr[gi, h * dec_seq:(h + 1) * dec_seq, :] = _bias_table(bias_ref, base, steps(qi_n - row_n))

    for s_i in range(SEQ_BLK):
        rows = slice(s_i * dec_seq, (s_i + 1) * dec_seq)
        outs, lses = [], []
        for gi in range(N_ATT):
            cols = slice(gi * LANES, (gi + 1) * LANES)
            qs = _stack_heads(q_ref[rows, cols] * scale, masks)
            k_new, v_new = k_ref[rows, cols], v_ref[rows, cols]
            s_c = _dot(qs, kcs[gi][s_i]) + tcs[gi][...]
            s_n = _dot_nt(qs, k_new) + tn_scr[gi]
            m = jnp.maximum(jnp.max(s_c, axis=1, keepdims=True), jnp.max(s_n, axis=1, keepdims=True))
            p_c = jnp.exp(s_c - m)
            p_n = jnp.exp(s_n - m)
            den = jnp.sum(p_c, axis=1, keepdims=True) + jnp.sum(p_n, axis=1, keepdims=True)
            pv = _dot_nt(p_c, vcs[gi][s_i]) + _dot(p_n, v_new)
            outs.append(_unstack_heads(pv / den, masks, dec_seq))
            lses.append(_unstack_heads(m + jnp.log(den), masks, dec_seq))
        o_ref[rows, :] = _mix_groups(outs, lses)


def _attn_sample_call(l, q, k, v, caches_t, bias_flat, n_prompt, dec_batch, dec_seq):
    rows = SEQ_BLK * dec_seq
    off = n_prompt // rows
    pasts = tuple(caches_t[2 * gi].shape[3] for gi in range(N_ATT))
    new_spec = pl.BlockSpec((rows, ATT_DIM), lambda i: (off + i, 0))
    cache_specs = [pl.BlockSpec((None, SEQ_BLK, LANES, c.shape[3]), lambda i: (l, i, 0, 0)) for c in caches_t]
    return pl.pallas_call(
        functools.partial(_attn_sample_kernel, dec_seq, pasts),
        grid=(dec_batch // SEQ_BLK,),
        in_specs=[pl.BlockSpec(memory_space=pltpu.SMEM), new_spec, new_spec, new_spec] + cache_specs,
        out_specs=pl.BlockSpec((rows, ATT_OUT), lambda i: (i, 0)),
        out_shape=jax.ShapeDtypeStruct((dec_batch * dec_seq, ATT_OUT), F32),
        scratch_shapes=[pltpu.VMEM((HPG * dec_seq, p), F32) for p in pasts]
                       + [pltpu.VMEM((N_ATT, HPG * dec_seq, dec_seq), F32)],
        compiler_params=_params(1),
        name="attn_sample",
    )(bias_flat, q, k, v, *caches_t)


def _merge_kernel(alpha, x_ref, mab_ref, gc_ref, yc_ref, wuc_ref, wo_ref, g_ref, b_ref, wrt_ref, br_ref,
                  x1_ref, x1g_ref, w_ref, slot_ref, tcnt_ref, tbase_ref, cnt_scr):
    @pl.when(pl.program_id(0) == 0)
    def _():
        cnt_scr[...] = jnp.zeros_like(cnt_scr)

    merged = mab_ref[...] + gc_ref[...] * _dot(yc_ref[...], wuc_ref[...])
    x1 = _layer_norm(alpha * x_ref[...] + _dot(merged, wo_ref[...]), g_ref[...], b_ref[...])
    x1_ref[...] = x1
    for c in range(LANE_CHUNKS):
        x1g_ref[pl.ds(c, TM, stride=LANE_CHUNKS), :] = x1[:, c * LANES:(c + 1) * LANES]

    logits = lax.dot_general(wrt_ref[...], x1, (((1,), (1,)), ((), ())),
                             precision=lax.Precision.HIGHEST, preferred_element_type=F32) + br_ref[...]
    eio = lax.broadcasted_iota(jnp.int32, (N_EXPERTS, TM), 0)
    vals, idxs = [], []
    for _ in range(TOP_K):
        mv = jnp.max(logits, axis=0, keepdims=True)
        ix = jnp.min(jnp.where(logits == mv, eio, N_EXPERTS), axis=0, keepdims=True)
        vals.append(mv)
        idxs.append(ix)
        logits = jnp.where(eio == ix, -jnp.inf, logits)
    tv = jnp.concatenate(vals, axis=0)
    e = jnp.exp(tv - tv[0:1, :])
    w_ref[...] = e / jnp.sum(e, axis=0, keepdims=True)

    onehots = [eio == ix for ix in idxs]
    chosen = functools.reduce(lambda a, c: a + c, [jnp.where(oh, 1.0, 0.0) for oh in onehots])
    earlier = (lax.broadcasted_iota(jnp.int32, (TM, TM), 0) < lax.broadcasted_iota(jnp.int32, (TM, TM), 1))
    within = _dot(chosen, jnp.where(earlier, 1.0, 0.0))
    tile_cnt = jnp.broadcast_to(jnp.sum(chosen, axis=1, keepdims=True), (N_EXPERTS, LANES))
    lower = (lax.broadcasted_iota(jnp.int32, (N_EXPERTS, N_EXPERTS), 1)
             < lax.broadcasted_iota(jnp.int32, (N_EXPERTS, N_EXPERTS), 0))
    place = within + _dot(jnp.where(lower, 1.0, 0.0), tile_cnt)[:, 0:1]
    slot_ref[...] = jnp.concatenate(
        [jnp.sum(jnp.where(oh, place, 0.0), axis=0, keepdims=True) for oh in onehots], axis=0).astype(jnp.int32)
    tcnt_ref[...] = tile_cnt.astype(jnp.int32)
    tbase_ref[...] = cnt_scr[...].astype(jnp.int32)
    cnt_scr[...] = cnt_scr[...] + tile_cnt


def _merge_call(l, alpha, x, mab, gc, yc, w_up_c, w_o, ln_g, ln_b, w_router_t, b_router):
    n = x.shape[0]
    row_spec = lambda w: pl.BlockSpec((TM, w), lambda i: (i, 0))
    vec_spec = pl.BlockSpec((None, 1, D_MODEL), lambda i: (l, 0, 0))
    return pl.pallas_call(
        functools.partial(_merge_kernel, alpha),
        grid=(n // TM,),
        in_specs=[
            row_spec(D_MODEL), row_spec(D_MODEL), row_spec(D_MODEL), row_spec(ATT_OUT),
            pl.BlockSpec((None, ATT_OUT, D_MODEL), lambda i: (l, 0, 0)),
            pl.BlockSpec((None, D_MODEL, D_MODEL), lambda i: (l, 0, 0)),
            vec_spec, vec_spec,
            pl.BlockSpec((None, N_EXPERTS, D_MODEL), lambda i: (l, 0, 0)),
            pl.BlockSpec((None, N_EXPERTS, 1), lambda i: (l, 0, 0)),
        ],
        out_specs=[
            row_spec(D_MODEL),
            pl.BlockSpec((TM * LANE_CHUNKS, LANES), lambda i: (i, 0)),
            pl.BlockSpec((TOP_K, TM), lambda i: (0, i)),
            pl.BlockSpec((TOP_K, TM), lambda i: (0, i)),
            pl.BlockSpec((None, N_EXPERTS, LANES), lambda i: (i, 0, 0)),
            pl.BlockSpec((None, N_EXPERTS, LANES), lambda i: (i, 0, 0)),
        ],
        out_shape=[
            jax.ShapeDtypeStruct((n, D_MODEL), F32),
            jax.ShapeDtypeStruct((n * LANE_CHUNKS, LANES), F32),
            jax.ShapeDtypeStruct((TOP_K, n), F32),
            jax.ShapeDtypeStruct((TOP_K, n), jnp.int32),
            jax.ShapeDtypeStruct((n // TM, N_EXPERTS, LANES), jnp.int32),
            jax.ShapeDtypeStruct((n // TM, N_EXPERTS, LANES), jnp.int32),
        ],
        scratch_shapes=[pltpu.VMEM((N_EXPERTS, LANES), F32)],
        compiler_params=_params(1),
        name="merge_ln1_router",
    )(x, mab, gc, yc, w_up_c, w_o, ln_g, ln_b, w_router_t, b_router)


def _destride(ref, rows):
    return jnp.concatenate([ref[pl.ds(c, rows, stride=LANE_CHUNKS), :] for c in range(LANE_CHUNKS)], axis=1)


def _slab(ref, index, count=1):
    return ref.at[pl.ds(pl.multiple_of(index * LANE_CHUNKS, LANE_CHUNKS), count * LANE_CHUNKS), :]


def _start_expert_blocks(tile, tcnt_ref, hstart_ref, make_copy):
    def per_expert(e, staged):
        cnt = tcnt_ref[tile * N_EXPERTS + e]
        first = hstart_ref[tile * N_EXPERTS + e]
        for bit in reversed(range(TM.bit_length())):
            size = 1 << bit
            done = (cnt >> (bit + 1)) << (bit + 1)

            @pl.when((cnt & size) != 0)
            def _(size=size, done=done):
                make_copy(staged + done, first + done, size).start()
        return staged + cnt

    lax.fori_loop(0, N_EXPERTS, per_expert, 0)


def _dispatch_kernel(n_tiles, lt_ref, has_ref, nv_ref, tcnt_ref, hstart_ref, slot_ref, x1g_ref, xs_hbm,
                     zbuf, stg, zsem, sem):
    tile_rows = TM_E * LANE_CHUNKS
    i = pl.program_id(0)
    n_steps = pl.num_programs(0)
    buf = i % 2
    staged_all = lambda b: pltpu.make_async_copy(stg.at[b], xs_hbm.at[pl.ds(0, TM * TOP_K * LANE_CHUNKS), :],
                                                 sem.at[b])

    @pl.when(i == 0)
    def _():
        zbuf[...] = jnp.zeros_like(zbuf)
        zero_copy = lambda t: pltpu.make_async_copy(zbuf, xs_hbm.at[pl.ds(t * tile_rows, tile_rows), :], zsem)
        for e in range(N_EXPERTS):
            @pl.when(has_ref[e] > 0)
            def _(e=e):
                zero_copy(lt_ref[e]).start()
        lax.fori_loop(nv_ref[0], n_tiles, lambda t, c: (zero_copy(t).start(), c)[1], 0)
        for e in range(N_EXPERTS):
            @pl.when(has_ref[e] > 0)
            def _(e=e):
                zero_copy(lt_ref[e]).wait()
        lax.fori_loop(nv_ref[0], n_tiles, lambda t, c: (zero_copy(t).wait(), c)[1], 0)

    @pl.when(i >= 2)
    def _():
        staged_all(buf).wait()

    def place(t, carry):
        slab = _slab(x1g_ref, t)[...]
        for kk in range(TOP_K):
            _slab(stg.at[buf], slot_ref[kk, t])[...] = slab
        return carry

    lax.fori_loop(0, TM, place, 0, unroll=DMA_UNROLL)
    _start_expert_blocks(i, tcnt_ref, hstart_ref, lambda staged, row, size: pltpu.make_async_copy(
        _slab(stg.at[buf], staged, size), _slab(xs_hbm, row, size), sem.at[buf]))

    @pl.when(i == n_steps - 1)
    def _():
        staged_all(buf).wait()

    @pl.when(jnp.logical_and(i == n_steps - 1, i >= 1))
    def _():
        staged_all(1 - buf).wait()


def _dispatch_call(last_tile, has_rows, n_valid, tcnt, hstart, slot, x1g, n_tiles):
    n = slot.shape[1]
    grid_spec = pltpu.PrefetchScalarGridSpec(
        num_scalar_prefetch=5,
        grid=(n // TM,),
        in_specs=[
            pl.BlockSpec((TOP_K, TM), lambda i, *_: (0, i), memory_space=pltpu.SMEM),
            pl.BlockSpec((TM * LANE_CHUNKS, LANES), lambda i, *_: (i, 0)),
        ],
        out_specs=pl.BlockSpec(memory_space=pl.ANY),
        scratch_shapes=[pltpu.VMEM((TM_E * LANE_CHUNKS, LANES), F32),
                        pltpu.VMEM((2, TM * TOP_K * LANE_CHUNKS, LANES), F32),
                        pltpu.SemaphoreType.DMA(()), pltpu.SemaphoreType.DMA((2,))],
    )
    return pl.pallas_call(
        functools.partial(_dispatch_kernel, n_tiles),
        grid_spec=grid_spec,
        out_shape=jax.ShapeDtypeStruct((n_tiles * TM_E * LANE_CHUNKS, LANES), F32),
        compiler_params=_params(1),
        name="moe_dispatch",
    )(last_tile, has_rows, n_valid, tcnt, hstart, slot, x1g)


def _moe_ffn_kernel(te_ref, nv_ref, xs_ref, wg_ref, bg_ref, wu_ref, bu_ref, wd_ref, bd_ref,
                    ys_ref, wg_b, wu_b, wd_b):
    t = pl.program_id(0)

    @pl.when(t < nv_ref[0])
    def _():
        @pl.when(jnp.logical_or(t == 0, te_ref[t] != te_ref[jnp.maximum(t - 1, 0)]))
        def _():
            wg_b[...] = wg_ref[...].astype(BF16)
            wu_b[...] = wu_ref[...].astype(BF16)
            wd_b[...] = wd_ref[...].astype(BF16)

        x = _destride(xs_ref, TM_E).astype(BF16)
        gl = jnp.minimum(jnp.dot(x, wg_b[...], preferred_element_type=F32) + bg_ref[...], SWIGLU_LIMIT)
        ul = jnp.clip(jnp.dot(x, wu_b[...], preferred_element_type=F32) + bu_ref[...], -SWIGLU_LIMIT, SWIGLU_LIMIT)
        hid = gl * _sigmoid(SWIGLU_ALPHA * gl) * (ul + 1.0)
        y = jnp.dot(hid.astype(BF16), wd_b[...], preferred_element_type=F32) + bd_ref[...]
        for c in range(LANE_CHUNKS):
            ys_ref[pl.ds(c, TM_E, stride=LANE_CHUNKS), :] = y[:, c * LANES:(c + 1) * LANES]

    @pl.when(t >= nv_ref[0])
    def _():
        ys_ref[...] = jnp.zeros_like(ys_ref)


def _moe_ffn_call(l, tile_expert, n_valid, xs, w_gate, b_gate, w_up, b_up, w_down, b_down):
    n_tiles = tile_expert.shape[0]
    mat_spec = pl.BlockSpec((None, None, D_MODEL, D_MODEL), lambda t, te, nv: (l, te[t], 0, 0))
    vec_spec = pl.BlockSpec((None, None, 1, D_MODEL), lambda t, te, nv: (l, te[t], 0, 0))
    in_slab = pl.BlockSpec((TM_E * LANE_CHUNKS, LANES), lambda t, te, nv: (jnp.minimum(t, nv[0] - 1), 0))
    grid_spec = pltpu.PrefetchScalarGridSpec(
        num_scalar_prefetch=2,
        grid=(n_tiles,),
        in_specs=[in_slab, mat_spec, vec_spec, mat_spec, vec_spec, mat_spec, vec_spec],
        out_specs=pl.BlockSpec((TM_E * LANE_CHUNKS, LANES), lambda t, te, nv: (t, 0)),
        scratch_shapes=[pltpu.VMEM((D_MODEL, D_MODEL), BF16)] * 3,
    )
    return pl.pallas_call(
        _moe_ffn_kernel,
        grid_spec=grid_spec,
        out_shape=jax.ShapeDtypeStruct((n_tiles * TM_E * LANE_CHUNKS, LANES), F32),
        compiler_params=_params(1),
        name="moe_ffn",
    )(tile_expert, n_valid, xs, w_gate, b_gate, w_up, b_up, w_down, b_down)


def _combine_kernel(alpha, tcnt_ref, hstart_ref, slot_ref, w_ref, x1_ref, ys_hbm, g_ref, b_ref, x2_ref,
                    stg, mixed, sem):
    i = pl.program_id(0)
    n_steps = pl.num_programs(0)
    buf = i % 2

    def fetch(tile, b):
        _start_expert_blocks(tile, tcnt_ref, hstart_ref, lambda staged, row, size: pltpu.make_async_copy(
            _slab(ys_hbm, row, size), _slab(stg.at[b], staged, size), sem.at[b]))

    @pl.when(i == 0)
    def _():
        fetch(0, 0)

    @pl.when(i + 1 < n_steps)
    def _():
        fetch(i + 1, 1 - buf)

    pltpu.make_async_copy(ys_hbm.at[pl.ds(0, TM * TOP_K * LANE_CHUNKS), :], stg.at[buf], sem.at[buf]).wait()

    def mix(t, carry):
        acc = w_ref[0, t] * _slab(stg.at[buf], slot_ref[0, t])[...]
        for kk in range(1, TOP_K):
            acc = acc + w_ref[kk, t] * _slab(stg.at[buf], slot_ref[kk, t])[...]
        _slab(mixed, t)[...] = acc
        return carry

    lax.fori_loop(0, TM, mix, 0, unroll=DMA_UNROLL)
    x2_ref[...] = _layer_norm(alpha * x1_ref[...] + _destride(mixed, TM), g_ref[...], b_ref[...])


def _combine_call(l, alpha, tcnt, hstart, slot, top_w, x1, ys, ln_g, ln_b):
    n = x1.shape[0]
    vec_spec = pl.BlockSpec((None, 1, D_MODEL), lambda i, *_: (l, 0, 0))
    choice_spec = pl.BlockSpec((TOP_K, TM), lambda i, *_: (0, i), memory_space=pltpu.SMEM)
    grid_spec = pltpu.PrefetchScalarGridSpec(
        num_scalar_prefetch=2,
        grid=(n // TM,),
        in_specs=[
            choice_spec, choice_spec,
            pl.BlockSpec((TM, D_MODEL), lambda i, *_: (i, 0)),
            pl.BlockSpec(memory_space=pl.ANY),
            vec_spec, vec_spec,
        ],
        out_specs=pl.BlockSpec((TM, D_MODEL), lambda i, *_: (i, 0)),
        scratch_shapes=[pltpu.VMEM((2, TM * TOP_K * LANE_CHUNKS, LANES), F32),
                        pltpu.VMEM((TM * LANE_CHUNKS, LANES), F32),
                        pltpu.SemaphoreType.DMA((2,))],
    )
    return pl.pallas_call(
        functools.partial(_combine_kernel, alpha),
        grid_spec=grid_spec,
        out_shape=jax.ShapeDtypeStruct((n, D_MODEL), F32),
        compiler_params=_params(1),
        name="moe_combine_ln2",
    )(tcnt, hstart, slot, top_w, x1, ys, ln_g, ln_b)


def _route(tcnt, tbase, n_tiles):
    experts = jnp.arange(N_EXPERTS, dtype=jnp.int32)
    counts = tbase[-1] + tcnt[-1]
    padded = ((counts + TM_E - 1) // TM_E) * TM_E
    ends = jnp.cumsum(padded)
    hstart = (ends - padded)[None, :] + tbase
    n_valid = ends[-1] // TM_E
    last_expert = jnp.max(jnp.where(counts > 0, experts, 0))
    tile_start = jnp.arange(n_tiles, dtype=jnp.int32) * TM_E
    tile_expert = jnp.minimum(jnp.sum(ends[None, :] <= tile_start[:, None], axis=1), last_expert)
    return (tile_expert.astype(jnp.int32), n_valid.astype(jnp.int32).reshape(1),
            tcnt.reshape(-1), hstart.astype(jnp.int32).reshape(-1),
            (ends // TM_E - 1).astype(jnp.int32), (counts > 0).astype(jnp.int32))


def _t5_bucket(dist):
    max_exact = NUM_BUCKETS // 2
    distf = jnp.maximum(dist, 1).astype(F32)
    large = max_exact + (jnp.log(distf / max_exact) / math.log(MAX_DISTANCE / max_exact)
                         * (NUM_BUCKETS - max_exact)).astype(jnp.int32)
    large = jnp.minimum(large, NUM_BUCKETS - 1)
    return jnp.where(dist < max_exact, dist, large)


def _group_bias(rel_bias, gi, dil):
    j = jnp.arange(N_KEYS, dtype=jnp.int32)
    return rel_bias[_t5_bucket(dil * j)][:, gi * HPG:(gi + 1) * HPG].T.astype(F32)


def _step_bias(rel_bias):
    return jnp.stack([_group_bias(rel_bias, gi, dil) for gi, (_, dil) in enumerate(ATT_GROUPS)], axis=0).reshape(-1)


def _spatial_tables(w_sp, b_sp, dec_seq):
    depth = w_sp.shape[0]
    tril = jnp.tril(jnp.ones((CHUNK, CHUNK), bool))
    wp = jnp.where(tril, w_sp, 0.0)
    ws_small = jnp.where(tril[:dec_seq, :dec_seq], w_sp[:, :, :dec_seq, :dec_seq], 0.0)
    eye = jnp.eye(CHUNK // dec_seq, dtype=w_sp.dtype)
    ws = jnp.einsum("ab,lgij->lgaibj", eye, ws_small).reshape(depth, SG_GROUPS, CHUNK, CHUNK)
    bp = jnp.repeat(jnp.swapaxes(b_sp, 1, 2), SG_HEAD, axis=2)
    bs = jnp.tile(bp[:, :dec_seq], (1, CHUNK // dec_seq, 1))
    return jnp.stack([wp, ws], axis=1).astype(BF16), jnp.stack([bp, bs], axis=1)


def kernel(x_prompt, x_sample, state_conv, cache_k_w128, cache_v_w128, cache_k_w512, cache_v_w512,
           cache_k_w2048, cache_v_w2048, w_in, conv_w, sg_ln_g, sg_ln_b, w_sp, b_sp, rel_bias,
           w_up_a, w_up_b, w_up_c, w_o, ln1_g, ln1_b, w_router, b_router, w_gate, b_gate, w_up, b_up,
           w_down, b_down, ln2_g, ln2_b):
    batch, seq, _ = x_prompt.shape
    dec_batch, dec_seq, _ = x_sample.shape
    depth = w_in.shape[0]
    n_prompt, n_sample = batch * seq, dec_batch * dec_seq
    n = n_prompt + n_sample
    alpha = (2 * depth) ** 0.25
    assert seq % TM == 0 and n_sample % TM == 0 and dec_seq & (dec_seq - 1) == 0 and CHUNK % dec_seq == 0
    assert all(seq % (dil * Q_BLK) == 0 and dil & (dil - 1) == 0 for _, dil in ATT_GROUPS)
    assert dec_batch % SEQ_BLK == 0
    n_tiles = (n * TOP_K) // TM_E + N_EXPERTS

    caches_t = [jnp.transpose(c, (0, 1, 3, 4, 2)).reshape(c.shape[0], c.shape[1], LANES, c.shape[2]) for c in
                (cache_k_w128, cache_v_w128, cache_k_w512, cache_v_w512, cache_k_w2048, cache_v_w2048)]
    bias_flat = _step_bias(rel_bias)
    wsp2, bsp2 = _spatial_tables(w_sp, b_sp, dec_seq)
    w_in_b, w_up_a_b, w_up_b_b, w_up_c_b, w_o_b = (w.astype(BF16) for w in (w_in, w_up_a, w_up_b, w_up_c, w_o))
    vec = lambda a: a.reshape(depth, 1, a.shape[-1])
    w_router_t = jnp.swapaxes(w_router, 1, 2)
    b_router_c = b_router.reshape(depth, N_EXPERTS, 1)
    expert_vec = lambda a: a.reshape(depth, N_EXPERTS, 1, a.shape[-1])

    x = jnp.concatenate([x_prompt.reshape(n_prompt, D_MODEL), x_sample.reshape(n_sample, D_MODEL)], axis=0)
    outs = {name: [] for name in ("conv_p", "conv_s", "sgv", "kp", "vp", "ks", "vs")}
    for l in range(depth):
        s1 = jnp.zeros((dec_batch, dec_seq, CONV_DIM), F32).at[:, 0].set(state_conv[l, :, 1])
        s2 = jnp.zeros((dec_batch, dec_seq, CONV_DIM), F32).at[:, 0].set(state_conv[l, :, 0])
        s2 = s2.at[:, 1].set(state_conv[l, :, 1])
        q, k, v, mab, gc, tail, z_s, v_rows = _inproj_call(
            l, x, w_in_b, conv_w, vec(sg_ln_g), vec(sg_ln_b), wsp2, bsp2, w_up_a_b, w_up_b_b,
            s1.reshape(n_sample, CONV_DIM), s2.reshape(n_sample, CONV_DIM), n_prompt, seq, n_sample, dec_seq)
        yc_p = _attn_prompt_call(q, k, v, bias_flat, batch, seq)
        yc_s = _attn_sample_call(l, q, k, v, caches_t, bias_flat, n_prompt, dec_batch, dec_seq)
        yc = jnp.concatenate([yc_p, yc_s], axis=0)
        x1, x1g, top_w, slot, tcnt, tbase = _merge_call(
            l, alpha, x, mab, gc, yc, w_up_c_b, w_o_b, vec(ln1_g), vec(ln1_b), w_router_t, b_router_c)
        tile_expert, n_valid, tcnt, hstart, last_tile, has_rows = _route(tcnt[:, :, 0], tbase[:, :, 0], n_tiles)
        xs = _dispatch_call(last_tile, has_rows, n_valid, tcnt, hstart, slot, x1g, n_tiles)
        ys = _moe_ffn_call(l, tile_expert, n_valid, xs, w_gate, expert_vec(b_gate),
                           w_up, expert_vec(b_up), w_down, expert_vec(b_down))
        x = _combine_call(l, alpha, tcnt, hstart, slot, top_w, x1, ys, vec(ln2_g), vec(ln2_b))

        outs["conv_p"].append(tail[:, SUBLANES - (CONV_W - 1):])
        outs["conv_s"].append(z_s.reshape(dec_batch, dec_seq, CONV_DIM)[:, dec_seq - (CONV_W - 1):])
        outs["sgv"].append(v_rows.reshape(dec_batch, dec_seq, SG_DIM))
        kp4 = k[:n_prompt].reshape(batch, seq, N_ATT, HPG, HEAD_DIM)
        vp4 = v[:n_prompt].reshape(batch, seq, N_ATT, HPG, HEAD_DIM)
        ks4 = k[n_prompt:].reshape(dec_batch, dec_seq, N_ATT, HPG, HEAD_DIM)
        vs4 = v[n_prompt:].reshape(dec_batch, dec_seq, N_ATT, HPG, HEAD_DIM)
        outs["kp"].append([kp4[:, seq - min(win, seq):, gi] for gi, (win, _) in enumerate(ATT_GROUPS)])
        outs["vp"].append([vp4[:, seq - min(win, seq):, gi] for gi, (win, _) in enumerate(ATT_GROUPS)])
        outs["ks"].append([ks4[:, dec_seq - min(win, dec_seq):, gi] for gi, (win, _) in enumerate(ATT_GROUPS)])
        outs["vs"].append([vs4[:, dec_seq - min(win, dec_seq):, gi] for gi, (win, _) in enumerate(ATT_GROUPS)])

    stack = lambda name: jnp.stack(outs[name], axis=0)
    per_group = lambda name, gi: jnp.stack([layer[gi] for layer in outs[name]], axis=0)
    result = [x[:n_prompt].reshape(batch, seq, D_MODEL), x[n_prompt:].reshape(dec_batch, dec_seq, D_MODEL),
              stack("conv_p"), stack("conv_s")]
    for gi in range(N_ATT):
        result += [per_group("kp", gi), per_group("vp", gi)]
    for gi in range(N_ATT):
        result += [per_group("ks", gi), per_group("vs", gi)]
    result.append(stack("sgv"))
    return tuple(result)
```

```python
import functools
import math

import jax
import jax.numpy as jnp
from jax import lax
from jax.experimental import pallas as pl
from jax.experimental.pallas import tpu as pltpu

D_MODEL = 1024
CONV_DIM = 384
CONV_W = 3
SG_DIM = 256
SG_GROUPS = 4
SG_HEAD = SG_DIM // SG_GROUPS
CHUNK = 128
ATT_GROUPS = ((128, 1), (512, 4), (2048, 16))
N_ATT = len(ATT_GROUPS)
HPG = 4
HEAD_DIM = 32
ATT_DIM = HPG * N_ATT * HEAD_DIM
ATT_OUT = HPG * HEAD_DIM
N_KEYS = 129
NUM_BUCKETS = 32
MAX_DISTANCE = 2048
N_EXPERTS = 32
TOP_K = 4
SWIGLU_LIMIT = 7.0
SWIGLU_ALPHA = 1.702
LN_EPS = 1e-5
OFF_A = 0
OFF_B = OFF_A + 3 * CONV_DIM
OFF_C = OFF_B + 2 * SG_DIM
OFF_G = OFF_C + 3 * ATT_DIM
IN_DIM = OFF_G + 3 * D_MODEL
NEG = -1e30

LANES = 128
SUBLANES = 8
LANE_CHUNKS = D_MODEL // LANES
VMEM_LIMIT = 56 * 1024 * 1024

TM = 256
TM_E = 256
Q_BLK = 128
SEQ_BLK = 4
DMA_UNROLL = 8
BIG_RUN = 64

F32 = jnp.float32
BF16 = jnp.bfloat16


def _dot(a, b):
    return jnp.dot(a.astype(BF16), b.astype(BF16), preferred_element_type=F32)


def _dot_nt(a, b):
    return lax.dot_general(a.astype(BF16), b.astype(BF16), (((1,), (1,)), ((), ())),
                           preferred_element_type=F32)


def _layer_norm(x, g, b):
    mu = jnp.mean(x, axis=-1, keepdims=True)
    xc = x - mu
    var = jnp.mean(xc * xc, axis=-1, keepdims=True)
    return xc * lax.rsqrt(var + LN_EPS) * g + b


def _gelu(x):
    return 0.5 * x * (1.0 + lax.erf(x * (2.0 ** -0.5)))


def _sigmoid(x):
    return 1.0 / (1.0 + jnp.exp(-x))


def _params(n_axes):
    return pltpu.CompilerParams(dimension_semantics=("arbitrary",) * n_axes,
                                vmem_limit_bytes=VMEM_LIMIT)


def _inproj_kernel(n_ptiles, tiles_per_seq, dec_seq,
                   x_ref, w_in_ref, convw_ref, lng_ref, lnb_ref, wsp_ref, bsp_ref, wua_ref, wub_ref,
                   s1_ref, s2_ref,
                   q_ref, k_ref, v_ref, mab_ref, gc_ref, tail_ref, zs_ref, vrows_ref, carry_ref):
    i = pl.program_id(0)
    is_prompt = i < n_ptiles

    @pl.when(jnp.logical_and(is_prompt, i % tiles_per_seq == 0))
    def _():
        carry_ref[...] = jnp.zeros_like(carry_ref)

    x = x_ref[...].astype(BF16)

    pa = jnp.dot(x, w_in_ref[:, OFF_A:OFF_B], preferred_element_type=F32)
    bg, cg, h = pa[:, :CONV_DIM], pa[:, CONV_DIM:2 * CONV_DIM], pa[:, 2 * CONV_DIM:]
    z = cg * h
    row = lax.broadcasted_iota(jnp.int32, (TM, CONV_DIM), 0)
    pos = jnp.where(is_prompt, row, row & (dec_seq - 1))
    prev1 = jnp.where(is_prompt, jnp.broadcast_to(carry_ref[7:8, :], (TM, CONV_DIM)), s1_ref[...])
    prev2 = jnp.where(is_prompt,
                      jnp.where(row == 0, jnp.broadcast_to(carry_ref[6:7, :], (TM, CONV_DIM)),
                                jnp.broadcast_to(carry_ref[7:8, :], (TM, CONV_DIM))),
                      s2_ref[...])
    z1 = jnp.where(pos >= 1, pltpu.roll(z, 1, axis=0), prev1)
    z2 = jnp.where(pos >= 2, pltpu.roll(z, 2, axis=0), prev2)
    cw = convw_ref[...]
    ya = bg * (cw[0:1, :] * z2 + cw[1:2, :] * z1 + cw[2:3, :] * z)

    @pl.when(is_prompt)
    def _():
        carry_ref[...] = z[TM - SUBLANES:, :]
        tail_ref[0] = z[TM - SUBLANES:, :]

    @pl.when(jnp.logical_not(is_prompt))
    def _():
        zs_ref[...] = z

    pb = jnp.dot(x, w_in_ref[:, OFF_B:OFF_C], preferred_element_type=F32)
    uv = _gelu(pb)
    u = uv[:, :SG_DIM]
    vn = _layer_norm(uv[:, SG_DIM:], lng_ref[...], lnb_ref[...])

    @pl.when(jnp.logical_not(is_prompt))
    def _():
        vrows_ref[...] = vn

    vb = vn.astype(BF16)
    low_half = lax.broadcasted_iota(jnp.int32, (CHUNK, LANES), 1) < SG_HEAD
    s_rows = []
    for c in range(TM // CHUNK):
        s_cols = []
        for p in range(SG_DIM // LANES):
            vcp = vb[c * CHUNK:(c + 1) * CHUNK, p * LANES:(p + 1) * LANES]
            sa = jnp.dot(wsp_ref[2 * p], vcp, preferred_element_type=F32)
            sb = jnp.dot(wsp_ref[2 * p + 1], vcp, preferred_element_type=F32)
            s_cols.append(jnp.where(low_half, sa, sb))
        s_rows.append(jnp.concatenate(s_cols, axis=1) + bsp_ref[...])
    yb = u * jnp.concatenate(s_rows, axis=0)

    pc = jnp.dot(x, w_in_ref[:, OFF_C:OFF_G], preferred_element_type=F32)
    q_ref[...] = pc[:, :ATT_DIM]
    k_ref[...] = pc[:, ATT_DIM:2 * ATT_DIM]
    v_ref[...] = pc[:, 2 * ATT_DIM:]

    ga = _sigmoid(jnp.dot(x, w_in_ref[:, OFF_G:OFF_G + D_MODEL], preferred_element_type=F32))
    mab = ga * _dot(ya, wua_ref[...])
    gb = _sigmoid(jnp.dot(x, w_in_ref[:, OFF_G + D_MODEL:OFF_G + 2 * D_MODEL], preferred_element_type=F32))
    mab_ref[...] = mab + gb * _dot(yb, wub_ref[...])
    gc_ref[...] = _sigmoid(jnp.dot(x, w_in_ref[:, OFF_G + 2 * D_MODEL:], preferred_element_type=F32))


def _inproj_call(l, x, w_in, conv_w, sg_ln_g, sg_ln_b, wsp2, bsp2, w_up_a, w_up_b, s1, s2,
                 n_prompt, seq, n_sample, dec_seq):
    n = n_prompt + n_sample
    n_ptiles = n_prompt // TM
    n_stiles = n_sample // TM
    tiles_per_seq = seq // TM
    batch = n_prompt // seq

    def stile(i):
        return jnp.maximum(i - n_ptiles, 0)

    def mode(i):
        return jnp.where(i < n_ptiles, 0, 1)

    row_spec = lambda w: pl.BlockSpec((TM, w), lambda i: (i, 0))
    srow_spec = lambda w: pl.BlockSpec((TM, w), lambda i: (stile(i), 0))
    kern = functools.partial(_inproj_kernel, n_ptiles, tiles_per_seq, dec_seq)
    return pl.pallas_call(
        kern,
        grid=(n_ptiles + n_stiles,),
        in_specs=[
            row_spec(D_MODEL),
            pl.BlockSpec((None, D_MODEL, IN_DIM), lambda i: (l, 0, 0), pipeline_mode=pl.Buffered(1)),
            pl.BlockSpec((None, CONV_W, CONV_DIM), lambda i: (l, 0, 0)),
            pl.BlockSpec((None, 1, SG_DIM), lambda i: (l, 0, 0)),
            pl.BlockSpec((None, 1, SG_DIM), lambda i: (l, 0, 0)),
            pl.BlockSpec((None, None, SG_GROUPS, CHUNK, CHUNK), lambda i: (l, mode(i), 0, 0, 0)),
            pl.BlockSpec((None, None, CHUNK, SG_DIM), lambda i: (l, mode(i), 0, 0)),
            pl.BlockSpec((None, CONV_DIM, D_MODEL), lambda i: (l, 0, 0)),
            pl.BlockSpec((None, SG_DIM, D_MODEL), lambda i: (l, 0, 0)),
            srow_spec(CONV_DIM),
            srow_spec(CONV_DIM),
        ],
        out_specs=[
            row_spec(ATT_DIM), row_spec(ATT_DIM), row_spec(ATT_DIM),
            row_spec(D_MODEL), row_spec(D_MODEL),
            pl.BlockSpec((1, SUBLANES, CONV_DIM),
                         lambda i: (jnp.minimum(i // tiles_per_seq, batch - 1), 0, 0)),
            srow_spec(CONV_DIM),
            srow_spec(SG_DIM),
        ],
        out_shape=[
            jax.ShapeDtypeStruct((n, ATT_DIM), F32), jax.ShapeDtypeStruct((n, ATT_DIM), F32),
            jax.ShapeDtypeStruct((n, ATT_DIM), F32),
            jax.ShapeDtypeStruct((n, D_MODEL), F32), jax.ShapeDtypeStruct((n, D_MODEL), F32),
            jax.ShapeDtypeStruct((batch, SUBLANES, CONV_DIM), F32),
            jax.ShapeDtypeStruct((n_sample, CONV_DIM), F32),
            jax.ShapeDtypeStruct((n_sample, SG_DIM), F32),
        ],
        scratch_shapes=[pltpu.VMEM((SUBLANES, CONV_DIM), F32)],
        compiler_params=_params(1),
        name="inproj_mix",
    )(x, w_in, conv_w, sg_ln_g, sg_ln_b, wsp2, bsp2, w_up_a, w_up_b, s1, s2)


def _head_masks(rows):
    lane_head = lax.broadcasted_iota(jnp.int32, (rows, LANES), 1) // HEAD_DIM
    return [lane_head == h for h in range(HPG)]


def _stack_heads(qb, masks):
    return jnp.concatenate([jnp.where(m, qb, 0.0) for m in masks], axis=0)


def _unstack_heads(stacked, masks, r):
    out = None
    for h, m in enumerate(masks):
        part = jnp.where(m, stacked[h * r:(h + 1) * r], 0.0)
        out = part if out is None else out + part
    return out


def _mix_groups(outs, lses):
    mx = functools.reduce(jnp.maximum, lses)
    ws = [jnp.exp(ls - mx) for ls in lses]
    num = functools.reduce(lambda a, b: a + b, [w * o for w, o in zip(ws, outs)])
    return num / functools.reduce(lambda a, b: a + b, ws)


def _bias_table(bias_ref, base, steps):
    def body(j, tab):
        return jnp.where(steps == j, bias_ref[base + j], tab)

    return lax.fori_loop(0, N_KEYS, body, jnp.full(steps.shape, NEG, F32))


def _attn_prompt_kernel(seq, bias_ref, q_ref, k_ref, v_ref, o_ref, tab_scr, *scr):
    b = pl.program_id(0)
    g = pl.program_id(1)
    o_scr, l_scr = scr[:N_ATT], scr[N_ATT:]
    masks = _head_masks(Q_BLK)
    scale = HEAD_DIM ** -0.5
    col = lax.broadcasted_iota(jnp.int32, (HPG * Q_BLK, 2 * Q_BLK), 1)

    def build_table(gi):
        qi = lax.broadcasted_iota(jnp.int32, (Q_BLK, 2 * Q_BLK), 0)
        kc = lax.broadcasted_iota(jnp.int32, (Q_BLK, 2 * Q_BLK), 1)
        for h in range(HPG):
            tab_scr[gi, h * Q_BLK:(h + 1) * Q_BLK, :] = _bias_table(bias_ref, (gi * HPG + h) * N_KEYS,
                                                                    qi + Q_BLK - kc)

    def run_group(gi, dil):
        rows_per_class = seq // dil
        n_blk = rows_per_class // Q_BLK

        def ld(ref, start):
            if dil > 1:
                return ref[pl.ds(start, Q_BLK, stride=dil), :]
            return ref[pl.ds(start, Q_BLK), :]

        def st(ref, start, val):
            if dil > 1:
                ref[pl.ds(start, Q_BLK, stride=dil), :] = val
            else:
                ref[pl.ds(start, Q_BLK), :] = val

        def block(it, carry):
            r = it // n_blk
            mb = it % n_blk
            cur = r + dil * Q_BLK * mb
            prev = r + dil * Q_BLK * jnp.maximum(mb - 1, 0)
            qb = ld(q_ref, cur) * scale
            kw = jnp.concatenate([ld(k_ref, prev), ld(k_ref, cur)], axis=0)
            vw = jnp.concatenate([ld(v_ref, prev), ld(v_ref, cur)], axis=0)
            s = _dot_nt(_stack_heads(qb, masks), kw) + tab_scr[gi]
            s = jnp.where(jnp.logical_or(col >= Q_BLK, mb > 0), s, NEG)
            m = jnp.max(s, axis=1, keepdims=True)
            p = jnp.exp(s - m)
            den = jnp.sum(p, axis=1, keepdims=True)
            pv = _dot(p, vw)
            st(o_scr[gi], cur, _unstack_heads(pv / den, masks, Q_BLK))
            st(l_scr[gi], cur, _unstack_heads(m + jnp.log(den), masks, Q_BLK))
            return carry

        lax.fori_loop(0, dil * n_blk, block, 0, unroll=4)

    for gi, (_, dil) in enumerate(ATT_GROUPS):
        @pl.when(jnp.logical_and(g == gi, b == 0))
        def _(gi=gi):
            build_table(gi)

        @pl.when(g == gi)
        def _(gi=gi, dil=dil):
            run_group(gi, dil)

    @pl.when(g == N_ATT - 1)
    def _():
        o_ref[...] = _mix_groups([s[...] for s in o_scr], [s[...] for s in l_scr])


def _attn_prompt_call(q, k, v, bias_flat, batch, seq):
    blk = lambda: pl.BlockSpec((seq, LANES), lambda b, g: (b, g))
    return pl.pallas_call(
        functools.partial(_attn_prompt_kernel, seq),
        grid=(batch, N_ATT),
        in_specs=[pl.BlockSpec(memory_space=pltpu.SMEM), blk(), blk(), blk()],
        out_specs=pl.BlockSpec((seq, ATT_OUT), lambda b, g: (b, 0)),
        out_shape=jax.ShapeDtypeStruct((batch * seq, ATT_OUT), F32),
        scratch_shapes=[pltpu.VMEM((N_ATT, HPG * Q_BLK, 2 * Q_BLK), F32)]
                       + [pltpu.VMEM((seq, LANES), F32)] * (2 * N_ATT),
        compiler_params=_params(2),
        name="attn_prompt",
    )(bias_flat, q, k, v)


def _attn_sample_kernel(dec_seq, pasts, bias_ref, q_ref, k_ref, v_ref, kc0, vc0, kc1, vc1, kc2, vc2,
                        o_ref, tc0, tc1, tc2, tn_scr):
    kcs, vcs, tcs = (kc0, kc1, kc2), (vc0, vc1, vc2), (tc0, tc1, tc2)
    masks = _head_masks(dec_seq)
    scale = HEAD_DIM ** -0.5

    @pl.when(pl.program_id(0) == 0)
    def _():
        for gi, (_, dil) in enumerate(ATT_GROUPS):
            past = pasts[gi]
            shift = dil.bit_length() - 1

            def steps(dist, dil=dil, shift=shift):
                return jnp.where((dist & (dil - 1)) == 0, dist >> shift, -1)

            qi_c = lax.broadcasted_iota(jnp.int32, (dec_seq, past), 0)
            row_c = lax.broadcasted_iota(jnp.int32, (dec_seq, past), 1)
            qi_n = lax.broadcasted_iota(jnp.int32, (dec_seq, dec_seq), 0)
            row_n = lax.broadcasted_iota(jnp.int32, (dec_seq, dec_seq), 1)
            for h in range(HPG):
                base = (gi * HPG + h) * N_KEYS
                tcs[gi][h * dec_seq:(h + 1) * dec_seq, :] = _bias_table(bias_ref, base, steps(past + qi_c - row_c))
                tn_scr[gi, h * dec_seq:(h + 1) * dec_seq, :] = _bias_table(bias_ref, base, steps(qi_n - row_n))

    for s_i in range(SEQ_BLK):
        rows = slice(s_i * dec_seq, (s_i + 1) * dec_seq)
        outs, lses = [], []
        for gi in range(N_ATT):
            cols = slice(gi * LANES, (gi + 1) * LANES)
            qs = _stack_heads(q_ref[rows, cols] * scale, masks)
            k_new, v_new = k_ref[rows, cols], v_ref[rows, cols]
            s_c = _dot(qs, kcs[gi][s_i]) + tcs[gi][...]
            s_n = _dot_nt(qs, k_new) + tn_scr[gi]
            m = jnp.maximum(jnp.max(s_c, axis=1, keepdims=True), jnp.max(s_n, axis=1, keepdims=True))
            p_c = jnp.exp(s_c - m)
            p_n = jnp.exp(s_n - m)
            den = jnp.sum(p_c, axis=1, keepdims=True) + jnp.sum(p_n, axis=1, keepdims=True)
            pv = _dot_nt(p_c, vcs[gi][s_i]) + _dot(p_n, v_new)
            outs.append(_unstack_heads(pv / den, masks, dec_seq))
            lses.append(_unstack_heads(m + jnp.log(den), masks, dec_seq))
        o_ref[rows, :] = _mix_groups(outs, lses)


def _attn_sample_call(l, q, k, v, caches_t, bias_flat, n_prompt, dec_batch, dec_seq):
    rows = SEQ_BLK * dec_seq
    off = n_prompt // rows
    pasts = tuple(caches_t[2 * gi].shape[3] for gi in range(N_ATT))
    new_spec = pl.BlockSpec((rows, ATT_DIM), lambda i: (off + i, 0))
    cache_specs = [pl.BlockSpec((None, SEQ_BLK, LANES, c.shape[3]), lambda i: (l, i, 0, 0)) for c in caches_t]
    return pl.pallas_call(
        functools.partial(_attn_sample_kernel, dec_seq, pasts),
        grid=(dec_batch // SEQ_BLK,),
        in_specs=[pl.BlockSpec(memory_space=pltpu.SMEM), new_spec, new_spec, new_spec] + cache_specs,
        out_specs=pl.BlockSpec((rows, ATT_OUT), lambda i: (i, 0)),
        out_shape=jax.ShapeDtypeStruct((dec_batch * dec_seq, ATT_OUT), F32),
        scratch_shapes=[pltpu.VMEM((HPG * dec_seq, p), F32) for p in pasts]
                       + [pltpu.VMEM((N_ATT, HPG * dec_seq, dec_seq), F32)],
        compiler_params=_params(1),
        name="attn_sample",
    )(bias_flat, q, k, v, *caches_t)


def _merge_kernel(alpha, x_ref, mab_ref, gc_ref, yc_ref, wuc_ref, wo_ref, g_ref, b_ref, wrt_ref, br_ref,
                  x1_ref, x1g_ref, w_ref, slot_ref, tcnt_ref, tbase_ref, cnt_scr):
    @pl.when(pl.program_id(0) == 0)
    def _():
        cnt_scr[...] = jnp.zeros_like(cnt_scr)

    merged = mab_ref[...] + gc_ref[...] * _dot(yc_ref[...], wuc_ref[...])
    x1 = _layer_norm(alpha * x_ref[...] + _dot(merged, wo_ref[...]), g_ref[...], b_ref[...])
    x1_ref[...] = x1
    for c in range(LANE_CHUNKS):
        x1g_ref[pl.ds(c, TM, stride=LANE_CHUNKS), :] = x1[:, c * LANES:(c + 1) * LANES]

    logits = lax.dot_general(wrt_ref[...], x1, (((1,), (1,)), ((), ())),
                             precision=lax.Precision.HIGHEST, preferred_element_type=F32) + br_ref[...]
    eio = lax.broadcasted_iota(jnp.int32, (N_EXPERTS, TM), 0)
    vals, idxs = [], []
    for _ in range(TOP_K):
        mv = jnp.max(logits, axis=0, keepdims=True)
        ix = jnp.min(jnp.where(logits == mv, eio, N_EXPERTS), axis=0, keepdims=True)
        vals.append(mv)
        idxs.append(ix)
        logits = jnp.where(eio == ix, -jnp.inf, logits)
    tv = jnp.concatenate(vals, axis=0)
    e = jnp.exp(tv - tv[0:1, :])
    w_ref[...] = e / jnp.sum(e, axis=0, keepdims=True)

    onehots = [eio == ix for ix in idxs]
    chosen = functools.reduce(lambda a, c: a + c, [jnp.where(oh, 1.0, 0.0) for oh in onehots])
    earlier = (lax.broadcasted_iota(jnp.int32, (TM, TM), 0) < lax.broadcasted_iota(jnp.int32, (TM, TM), 1))
    within = _dot(chosen, jnp.where(earlier, 1.0, 0.0))
    tile_cnt = jnp.broadcast_to(jnp.sum(chosen, axis=1, keepdims=True), (N_EXPERTS, LANES))
    lower = (lax.broadcasted_iota(jnp.int32, (N_EXPERTS, N_EXPERTS), 1)
             < lax.broadcasted_iota(jnp.int32, (N_EXPERTS, N_EXPERTS), 0))
    place = within + _dot(jnp.where(lower, 1.0, 0.0), tile_cnt)[:, 0:1]
    slot_ref[...] = jnp.concatenate(
        [jnp.sum(jnp.where(oh, place, 0.0), axis=0, keepdims=True) for oh in onehots],
        axis=0).astype(jnp.int32) * LANE_CHUNKS
    tcnt_ref[...] = tile_cnt.astype(jnp.int32)
    tbase_ref[...] = cnt_scr[...].astype(jnp.int32)
    cnt_scr[...] = cnt_scr[...] + tile_cnt


def _merge_call(l, alpha, x, mab, gc, yc, w_up_c, w_o, ln_g, ln_b, w_router_t, b_router):
    n = x.shape[0]
    row_spec = lambda w: pl.BlockSpec((TM, w), lambda i: (i, 0))
    vec_spec = pl.BlockSpec((None, 1, D_MODEL), lambda i: (l, 0, 0))
    return pl.pallas_call(
        functools.partial(_merge_kernel, alpha),
        grid=(n // TM,),
        in_specs=[
            row_spec(D_MODEL), row_spec(D_MODEL), row_spec(D_MODEL), row_spec(ATT_OUT),
            pl.BlockSpec((None, ATT_OUT, D_MODEL), lambda i: (l, 0, 0)),
            pl.BlockSpec((None, D_MODEL, D_MODEL), lambda i: (l, 0, 0)),
            vec_spec, vec_spec,
            pl.BlockSpec((None, N_EXPERTS, D_MODEL), lambda i: (l, 0, 0)),
            pl.BlockSpec((None, N_EXPERTS, 1), lambda i: (l, 0, 0)),
        ],
        out_specs=[
            row_spec(D_MODEL),
            pl.BlockSpec((TM * LANE_CHUNKS, LANES), lambda i: (i, 0)),
            pl.BlockSpec((TOP_K, TM), lambda i: (0, i)),
            pl.BlockSpec((TOP_K, TM), lambda i: (0, i)),
            pl.BlockSpec((None, N_EXPERTS, LANES), lambda i: (i, 0, 0)),
            pl.BlockSpec((None, N_EXPERTS, LANES), lambda i: (i, 0, 0)),
        ],
        out_shape=[
            jax.ShapeDtypeStruct((n, D_MODEL), F32),
            jax.ShapeDtypeStruct((n * LANE_CHUNKS, LANES), F32),
            jax.ShapeDtypeStruct((TOP_K, n), F32),
            jax.ShapeDtypeStruct((TOP_K, n), jnp.int32),
            jax.ShapeDtypeStruct((n // TM, N_EXPERTS, LANES), jnp.int32),
            jax.ShapeDtypeStruct((n // TM, N_EXPERTS, LANES), jnp.int32),
        ],
        scratch_shapes=[pltpu.VMEM((N_EXPERTS, LANES), F32)],
        compiler_params=_params(1),
        name="merge_ln1_router",
    )(x, mab, gc, yc, w_up_c, w_o, ln_g, ln_b, w_router_t, b_router)


def _destride(ref, rows):
    return jnp.concatenate([ref[pl.ds(c, rows, stride=LANE_CHUNKS), :] for c in range(LANE_CHUNKS)], axis=1)


def _slab_rows(ref, row, count=1):
    return ref.at[pl.ds(pl.multiple_of(row, LANE_CHUNKS), count * LANE_CHUNKS), :]


def _start_expert_blocks(tile, tcnt_ref, hstart_ref, make_copy):
    big_bits = BIG_RUN.bit_length() - 1

    def per_expert(e, staged):
        cnt = tcnt_ref[tile * N_EXPERTS + e]
        first = hstart_ref[tile * N_EXPERTS + e]
        n_big = cnt >> big_bits

        def big(j, carry):
            make_copy(staged + j * BIG_RUN, first + j * BIG_RUN, BIG_RUN).start()
            return carry

        lax.fori_loop(0, n_big, big, 0)
        for bit in reversed(range(big_bits)):
            size = 1 << bit
            done = (cnt >> (bit + 1)) << (bit + 1)

            @pl.when((cnt & size) != 0)
            def _(size=size, done=done):
                make_copy(staged + done, first + done, size).start()
        return staged + cnt

    lax.fori_loop(0, N_EXPERTS, per_expert, 0)


def _dispatch_kernel(n_tiles, lt_ref, has_ref, nv_ref, tcnt_ref, hstart_ref, slot_ref, x1g_ref, xs_hbm,
                     zbuf, stg, zsem, sem):
    tile_rows = TM_E * LANE_CHUNKS
    i = pl.program_id(0)
    n_steps = pl.num_programs(0)
    buf = i % 2
    staged_all = lambda b: pltpu.make_async_copy(stg.at[b], xs_hbm.at[pl.ds(0, TM * TOP_K * LANE_CHUNKS), :],
                                                 sem.at[b])

    @pl.when(i == 0)
    def _():
        zbuf[...] = jnp.zeros_like(zbuf)
        zero_copy = lambda t: pltpu.make_async_copy(zbuf, xs_hbm.at[pl.ds(t * tile_rows, tile_rows), :], zsem)
        for e in range(N_EXPERTS):
            @pl.when(has_ref[e] > 0)
            def _(e=e):
                zero_copy(lt_ref[e]).start()
        lax.fori_loop(nv_ref[0], n_tiles, lambda t, c: (zero_copy(t).start(), c)[1], 0)
        for e in range(N_EXPERTS):
            @pl.when(has_ref[e] > 0)
            def _(e=e):
                zero_copy(lt_ref[e]).wait()
        lax.fori_loop(nv_ref[0], n_tiles, lambda t, c: (zero_copy(t).wait(), c)[1], 0)

    def step(b):
        @pl.when(i >= 2)
        def _():
            staged_all(b).wait()

        def place(t, carry):
            slab = _slab_rows(x1g_ref, t * LANE_CHUNKS)[...]
            for kk in range(TOP_K):
                _slab_rows(stg.at[b], slot_ref[0, 0, t * TOP_K + kk])[...] = slab
            return carry

        lax.fori_loop(0, TM, place, 0, unroll=DMA_UNROLL)
        _start_expert_blocks(i, tcnt_ref, hstart_ref, lambda staged, row, size: pltpu.make_async_copy(
            _slab_rows(stg.at[b], staged * LANE_CHUNKS, size), _slab_rows(xs_hbm, row * LANE_CHUNKS, size),
            sem.at[b]))

        @pl.when(i == n_steps - 1)
        def _():
            staged_all(b).wait()

        @pl.when(jnp.logical_and(i == n_steps - 1, i >= 1))
        def _():
            staged_all(1 - b).wait()

    for b in range(2):
        @pl.when(buf == b)
        def _(b=b):
            step(b)


def _dispatch_call(last_tile, has_rows, n_valid, tcnt, hstart, slot, x1g, n_tiles):
    n = slot.shape[0] * TM
    grid_spec = pltpu.PrefetchScalarGridSpec(
        num_scalar_prefetch=5,
        grid=(n // TM,),
        in_specs=[
            pl.BlockSpec((1, 1, TM * TOP_K), lambda i, *_: (i, 0, 0), memory_space=pltpu.SMEM),
            pl.BlockSpec((TM * LANE_CHUNKS, LANES), lambda i, *_: (i, 0)),
        ],
        out_specs=pl.BlockSpec(memory_space=pl.ANY),
        scratch_shapes=[pltpu.VMEM((TM_E * LANE_CHUNKS, LANES), F32),
                        pltpu.VMEM((2, TM * TOP_K * LANE_CHUNKS, LANES), F32),
                        pltpu.SemaphoreType.DMA(()), pltpu.SemaphoreType.DMA((2,))],
    )
    return pl.pallas_call(
        functools.partial(_dispatch_kernel, n_tiles),
        grid_spec=grid_spec,
        out_shape=jax.ShapeDtypeStruct((n_tiles * TM_E * LANE_CHUNKS, LANES), F32),
        compiler_params=_params(1),
        name="moe_dispatch",
    )(last_tile, has_rows, n_valid, tcnt, hstart, slot, x1g)


def _moe_ffn_kernel(te_ref, nv_ref, xs_ref, wg_ref, bg_ref, wu_ref, bu_ref, wd_ref, bd_ref, ys_ref, w_b16):
    t = pl.program_id(0)

    @pl.when(t < nv_ref[0])
    def _():
        @pl.when(jnp.logical_or(t == 0, te_ref[t] != te_ref[jnp.maximum(t - 1, 0)]))
        def _():
            for j, w_ref in enumerate((wg_ref, wu_ref, wd_ref)):
                w_b16[j] = w_ref[...].astype(BF16)

        x = _destride(xs_ref, TM_E).astype(BF16)
        gl = jnp.minimum(jnp.dot(x, w_b16[0], preferred_element_type=F32) + bg_ref[...], SWIGLU_LIMIT)
        ul = jnp.clip(jnp.dot(x, w_b16[1], preferred_element_type=F32) + bu_ref[...], -SWIGLU_LIMIT, SWIGLU_LIMIT)
        hid = gl * _sigmoid(SWIGLU_ALPHA * gl) * (ul + 1.0)
        y = jnp.dot(hid.astype(BF16), w_b16[2], preferred_element_type=F32) + bd_ref[...]
        for c in range(LANE_CHUNKS):
            ys_ref[pl.ds(c, TM_E, stride=LANE_CHUNKS), :] = y[:, c * LANES:(c + 1) * LANES]

    @pl.when(t >= nv_ref[0])
    def _():
        ys_ref[...] = jnp.zeros_like(ys_ref)


def _moe_ffn_call(l, tile_expert, n_valid, xs, w_gate, b_gate, w_up, b_up, w_down, b_down):
    n_tiles = tile_expert.shape[0]
    mat_spec = pl.BlockSpec((None, None, D_MODEL, D_MODEL), lambda t, te, nv: (l, te[t], 0, 0))
    vec_spec = pl.BlockSpec((None, None, 1, D_MODEL), lambda t, te, nv: (l, te[t], 0, 0))
    in_slab = pl.BlockSpec((TM_E * LANE_CHUNKS, LANES), lambda t, te, nv: (jnp.minimum(t, nv[0] - 1), 0))
    grid_spec = pltpu.PrefetchScalarGridSpec(
        num_scalar_prefetch=2,
        grid=(n_tiles,),
        in_specs=[in_slab, mat_spec, vec_spec, mat_spec, vec_spec, mat_spec, vec_spec],
        out_specs=pl.BlockSpec((TM_E * LANE_CHUNKS, LANES), lambda t, te, nv: (t, 0)),
        scratch_shapes=[pltpu.VMEM((3, D_MODEL, D_MODEL), BF16)],
    )
    return pl.pallas_call(
        _moe_ffn_kernel,
        grid_spec=grid_spec,
        out_shape=jax.ShapeDtypeStruct((n_tiles * TM_E * LANE_CHUNKS, LANES), F32),
        compiler_params=_params(1),
        name="moe_ffn",
    )(tile_expert, n_valid, xs, w_gate, b_gate, w_up, b_up, w_down, b_down)


def _combine_kernel(alpha, tcnt_ref, hstart_ref, slot_ref, w_ref, x1_ref, ys_hbm, g_ref, b_ref, x2_ref,
                    stg, mixed, sem):
    i = pl.program_id(0)
    n_steps = pl.num_programs(0)
    buf = i % 2

    def fetch(tile, b):
        _start_expert_blocks(tile, tcnt_ref, hstart_ref, lambda staged, row, size: pltpu.make_async_copy(
            _slab_rows(ys_hbm, row * LANE_CHUNKS, size), _slab_rows(stg.at[b], staged * LANE_CHUNKS, size),
            sem.at[b]))

    def step(b):
        @pl.when(i == 0)
        def _():
            fetch(0, b)

        @pl.when(i + 1 < n_steps)
        def _():
            fetch(i + 1, 1 - b)

        pltpu.make_async_copy(ys_hbm.at[pl.ds(0, TM * TOP_K * LANE_CHUNKS), :], stg.at[b], sem.at[b]).wait()

        def mix(t, carry):
            c0 = t * TOP_K
            acc = w_ref[0, 0, c0] * _slab_rows(stg.at[b], slot_ref[0, 0, c0])[...]
            for kk in range(1, TOP_K):
                acc = acc + w_ref[0, 0, c0 + kk] * _slab_rows(stg.at[b], slot_ref[0, 0, c0 + kk])[...]
            _slab_rows(mixed, t * LANE_CHUNKS)[...] = acc
            return carry

        lax.fori_loop(0, TM, mix, 0, unroll=DMA_UNROLL)

    for b in range(2):
        @pl.when(buf == b)
        def _(b=b):
            step(b)

    x2_ref[...] = _layer_norm(alpha * x1_ref[...] + _destride(mixed, TM), g_ref[...], b_ref[...])


def _combine_call(l, alpha, tcnt, hstart, slot, top_w, x1, ys, ln_g, ln_b):
    n = x1.shape[0]
    vec_spec = pl.BlockSpec((None, 1, D_MODEL), lambda i, *_: (l, 0, 0))
    choice_spec = pl.BlockSpec((1, 1, TM * TOP_K), lambda i, *_: (i, 0, 0), memory_space=pltpu.SMEM)
    grid_spec = pltpu.PrefetchScalarGridSpec(
        num_scalar_prefetch=2,
        grid=(n // TM,),
        in_specs=[
            choice_spec, choice_spec,
            pl.BlockSpec((TM, D_MODEL), lambda i, *_: (i, 0)),
            pl.BlockSpec(memory_space=pl.ANY),
            vec_spec, vec_spec,
        ],
        out_specs=pl.BlockSpec((TM, D_MODEL), lambda i, *_: (i, 0)),
        scratch_shapes=[pltpu.VMEM((2, TM * TOP_K * LANE_CHUNKS, LANES), F32),
                        pltpu.VMEM((TM * LANE_CHUNKS, LANES), F32),
                        pltpu.SemaphoreType.DMA((2,))],
    )
    return pl.pallas_call(
        functools.partial(_combine_kernel, alpha),
        grid_spec=grid_spec,
        out_shape=jax.ShapeDtypeStruct((n, D_MODEL), F32),
        compiler_params=_params(1),
        name="moe_combine_ln2",
    )(tcnt, hstart, slot, top_w, x1, ys, ln_g, ln_b)


def _route(tcnt, tbase, n_tiles):
    experts = jnp.arange(N_EXPERTS, dtype=jnp.int32)
    counts = tbase[-1] + tcnt[-1]
    padded = ((counts + TM_E - 1) // TM_E) * TM_E
    ends = jnp.cumsum(padded)
    hstart = (ends - padded)[None, :] + tbase
    n_valid = ends[-1] // TM_E
    last_expert = jnp.max(jnp.where(counts > 0, experts, 0))
    tile_start = jnp.arange(n_tiles, dtype=jnp.int32) * TM_E
    tile_expert = jnp.minimum(jnp.sum(ends[None, :] <= tile_start[:, None], axis=1), last_expert)
    return (tile_expert.astype(jnp.int32), n_valid.astype(jnp.int32).reshape(1),
            tcnt.reshape(-1), hstart.astype(jnp.int32).reshape(-1),
            (ends // TM_E - 1).astype(jnp.int32), (counts > 0).astype(jnp.int32))


def _t5_bucket(dist):
    max_exact = NUM_BUCKETS // 2
    distf = jnp.maximum(dist, 1).astype(F32)
    large = max_exact + (jnp.log(distf / max_exact) / math.log(MAX_DISTANCE / max_exact)
                         * (NUM_BUCKETS - max_exact)).astype(jnp.int32)
    large = jnp.minimum(large, NUM_BUCKETS - 1)
    return jnp.where(dist < max_exact, dist, large)


def _group_bias(rel_bias, gi, dil):
    j = jnp.arange(N_KEYS, dtype=jnp.int32)
    return rel_bias[_t5_bucket(dil * j)][:, gi * HPG:(gi + 1) * HPG].T.astype(F32)


def _step_bias(rel_bias):
    return jnp.stack([_group_bias(rel_bias, gi, dil) for gi, (_, dil) in enumerate(ATT_GROUPS)], axis=0).reshape(-1)


def _spatial_tables(w_sp, b_sp, dec_seq):
    depth = w_sp.shape[0]
    tril = jnp.tril(jnp.ones((CHUNK, CHUNK), bool))
    wp = jnp.where(tril, w_sp, 0.0)
    ws_small = jnp.where(tril[:dec_seq, :dec_seq], w_sp[:, :, :dec_seq, :dec_seq], 0.0)
    eye = jnp.eye(CHUNK // dec_seq, dtype=w_sp.dtype)
    ws = jnp.einsum("ab,lgij->lgaibj", eye, ws_small).reshape(depth, SG_GROUPS, CHUNK, CHUNK)
    bp = jnp.repeat(jnp.swapaxes(b_sp, 1, 2), SG_HEAD, axis=2)
    bs = jnp.tile(bp[:, :dec_seq], (1, CHUNK // dec_seq, 1))
    return jnp.stack([wp, ws], axis=1).astype(BF16), jnp.stack([bp, bs], axis=1)


def kernel(x_prompt, x_sample, state_conv, cache_k_w128, cache_v_w128, cache_k_w512, cache_v_w512,
           cache_k_w2048, cache_v_w2048, w_in, conv_w, sg_ln_g, sg_ln_b, w_sp, b_sp, rel_bias,
           w_up_a, w_up_b, w_up_c, w_o, ln1_g, ln1_b, w_router, b_router, w_gate, b_gate, w_up, b_up,
           w_down, b_down, ln2_g, ln2_b):
    batch, seq, _ = x_prompt.shape
    dec_batch, dec_seq, _ = x_sample.shape
    depth = w_in.shape[0]
    n_prompt, n_sample = batch * seq, dec_batch * dec_seq
    n = n_prompt + n_sample
    alpha = (2 * depth) ** 0.25
    assert seq % TM == 0 and n_sample % TM == 0 and dec_seq & (dec_seq - 1) == 0 and CHUNK % dec_seq == 0
    assert all(seq % (dil * Q_BLK) == 0 and dil & (dil - 1) == 0 for _, dil in ATT_GROUPS)
    assert dec_batch % SEQ_BLK == 0
    n_tiles = (n * TOP_K) // TM_E + N_EXPERTS

    caches_t = [jnp.transpose(c, (0, 1, 3, 4, 2)).reshape(c.shape[0], c.shape[1], LANES, c.shape[2]) for c in
                (cache_k_w128, cache_v_w128, cache_k_w512, cache_v_w512, cache_k_w2048, cache_v_w2048)]
    bias_flat = _step_bias(rel_bias)
    wsp2, bsp2 = _spatial_tables(w_sp, b_sp, dec_seq)
    w_in_b, w_up_a_b, w_up_b_b, w_up_c_b, w_o_b = (w.astype(BF16) for w in (w_in, w_up_a, w_up_b, w_up_c, w_o))
    vec = lambda a: a.reshape(depth, 1, a.shape[-1])
    w_router_t = jnp.swapaxes(w_router, 1, 2)
    b_router_c = b_router.reshape(depth, N_EXPERTS, 1)
    expert_vec = lambda a: a.reshape(depth, N_EXPERTS, 1, a.shape[-1])

    x = jnp.concatenate([x_prompt.reshape(n_prompt, D_MODEL), x_sample.reshape(n_sample, D_MODEL)], axis=0)
    outs = {name: [] for name in ("conv_p", "conv_s", "sgv", "kp", "vp", "ks", "vs")}
    for l in range(depth):
        s1 = jnp.zeros((dec_batch, dec_seq, CONV_DIM), F32).at[:, 0].set(state_conv[l, :, 1])
        s2 = jnp.zeros((dec_batch, dec_seq, CONV_DIM), F32).at[:, 0].set(state_conv[l, :, 0])
        s2 = s2.at[:, 1].set(state_conv[l, :, 1])
        q, k, v, mab, gc, tail, z_s, v_rows = _inproj_call(
            l, x, w_in_b, conv_w, vec(sg_ln_g), vec(sg_ln_b), wsp2, bsp2, w_up_a_b, w_up_b_b,
            s1.reshape(n_sample, CONV_DIM), s2.reshape(n_sample, CONV_DIM), n_prompt, seq, n_sample, dec_seq)
        yc_p = _attn_prompt_call(q, k, v, bias_flat, batch, seq)
        yc_s = _attn_sample_call(l, q, k, v, caches_t, bias_flat, n_prompt, dec_batch, dec_seq)
        yc = jnp.concatenate([yc_p, yc_s], axis=0)
        x1, x1g, top_w, slot, tcnt, tbase = _merge_call(
            l, alpha, x, mab, gc, yc, w_up_c_b, w_o_b, vec(ln1_g), vec(ln1_b), w_router_t, b_router_c)
        tile_expert, n_valid, tcnt, hstart, last_tile, has_rows = _route(tcnt[:, :, 0], tbase[:, :, 0], n_tiles)
        by_tile = lambda a: a.reshape(TOP_K, n // TM, TM).transpose(1, 2, 0).reshape(n // TM, 1, TM * TOP_K)
        slot, top_w = by_tile(slot), by_tile(top_w)
        xs = _dispatch_call(last_tile, has_rows, n_valid, tcnt, hstart, slot, x1g, n_tiles)
        ys = _moe_ffn_call(l, tile_expert, n_valid, xs, w_gate, expert_vec(b_gate),
                           w_up, expert_vec(b_up), w_down, expert_vec(b_down))
        x = _combine_call(l, alpha, tcnt, hstart, slot, top_w, x1, ys, vec(ln2_g), vec(ln2_b))

        outs["conv_p"].append(tail[:, SUBLANES - (CONV_W - 1):])
        outs["conv_s"].append(z_s.reshape(dec_batch, dec_seq, CONV_DIM)[:, dec_seq - (CONV_W - 1):])
        outs["sgv"].append(v_rows.reshape(dec_batch, dec_seq, SG_DIM))
        kp4 = k[:n_prompt].reshape(batch, seq, N_ATT, HPG, HEAD_DIM)
        vp4 = v[:n_prompt].reshape(batch, seq, N_ATT, HPG, HEAD_DIM)
        ks4 = k[n_prompt:].reshape(dec_batch, dec_seq, N_ATT, HPG, HEAD_DIM)
        vs4 = v[n_prompt:].reshape(dec_batch, dec_seq, N_ATT, HPG, HEAD_DIM)
        outs["kp"].append([kp4[:, seq - min(win, seq):, gi] for gi, (win, _) in enumerate(ATT_GROUPS)])
        outs["vp"].append([vp4[:, seq - min(win, seq):, gi] for gi, (win, _) in enumerate(ATT_GROUPS)])
        outs["ks"].append([ks4[:, dec_seq - min(win, dec_seq):, gi] for gi, (win, _) in enumerate(ATT_GROUPS)])
        outs["vs"].append([vs4[:, dec_seq - min(win, dec_seq):, gi] for gi, (win, _) in enumerate(ATT_GROUPS)])

    stack = lambda name: jnp.stack(outs[name], axis=0)
    per_group = lambda name, gi: jnp.stack([layer[gi] for layer in outs[name]], axis=0)
    result = [x[:n_prompt].reshape(batch, seq, D_MODEL), x[n_prompt:].reshape(dec_batch, dec_seq, D_MODEL),
              stack("conv_p"), stack("conv_s")]
    for gi in range(N_ATT):
        result += [per_group("kp", gi), per_group("vp", gi)]
    for gi in range(N_ATT):
        result += [per_group("ks", gi), per_group("vs", gi)]
    result.append(stack("sgv"))
    return tuple(result)
```

```python
import functools
import math

import jax
import jax.numpy as jnp
from jax import lax
from jax.experimental import pallas as pl
from jax.experimental.pallas import tpu as pltpu

D_MODEL = 1024
CONV_DIM = 384
CONV_W = 3
SG_DIM = 256
SG_GROUPS = 4
SG_HEAD = SG_DIM // SG_GROUPS
CHUNK = 128
ATT_GROUPS = ((128, 1), (512, 4), (2048, 16))
N_ATT = len(ATT_GROUPS)
HPG = 4
HEAD_DIM = 32
ATT_DIM = HPG * N_ATT * HEAD_DIM
ATT_OUT = HPG * HEAD_DIM
N_KEYS = 129
NUM_BUCKETS = 32
MAX_DISTANCE = 2048
N_EXPERTS = 32
TOP_K = 4
SWIGLU_LIMIT = 7.0
SWIGLU_ALPHA = 1.702
LN_EPS = 1e-5
OFF_A = 0
OFF_B = OFF_A + 3 * CONV_DIM
OFF_C = OFF_B + 2 * SG_DIM
OFF_G = OFF_C + 3 * ATT_DIM
IN_DIM = OFF_G + 3 * D_MODEL
NEG = -1e30

LANES = 128
SUBLANES = 8
LANE_CHUNKS = D_MODEL // LANES
VMEM_LIMIT = 56 * 1024 * 1024

TM = 256
TM_E = 512
Q_BLK = 128
SEQ_BLK = 4
DMA_UNROLL = 8
BIG_RUN = 64

F32 = jnp.float32
BF16 = jnp.bfloat16


def _dot(a, b):
    return jnp.dot(a.astype(BF16), b.astype(BF16), preferred_element_type=F32)


def _dot_nt(a, b):
    return lax.dot_general(a.astype(BF16), b.astype(BF16), (((1,), (1,)), ((), ())),
                           preferred_element_type=F32)


def _layer_norm(x, g, b):
    mu = jnp.mean(x, axis=-1, keepdims=True)
    xc = x - mu
    var = jnp.mean(xc * xc, axis=-1, keepdims=True)
    return xc * lax.rsqrt(var + LN_EPS) * g + b


def _gelu(x):
    return 0.5 * x * (1.0 + lax.erf(x * (2.0 ** -0.5)))


def _sigmoid(x):
    return 1.0 / (1.0 + jnp.exp(-x))


def _params(n_axes):
    return pltpu.CompilerParams(dimension_semantics=("arbitrary",) * n_axes,
                                vmem_limit_bytes=VMEM_LIMIT)


def _inproj_kernel(n_ptiles, tiles_per_seq, dec_seq,
                   x_ref, w_in_ref, convw_ref, lng_ref, lnb_ref, wsp_ref, bsp_ref, wua_ref, wub_ref,
                   s1_ref, s2_ref,
                   q_ref, k_ref, v_ref, mab_ref, gc_ref, tail_ref, zs_ref, vrows_ref, carry_ref):
    i = pl.program_id(0)
    is_prompt = i < n_ptiles

    @pl.when(jnp.logical_and(is_prompt, i % tiles_per_seq == 0))
    def _():
        carry_ref[...] = jnp.zeros_like(carry_ref)

    x = x_ref[...].astype(BF16)

    pa = jnp.dot(x, w_in_ref[:, OFF_A:OFF_B], preferred_element_type=F32)
    bg, cg, h = pa[:, :CONV_DIM], pa[:, CONV_DIM:2 * CONV_DIM], pa[:, 2 * CONV_DIM:]
    z = cg * h
    row = lax.broadcasted_iota(jnp.int32, (TM, CONV_DIM), 0)
    pos = jnp.where(is_prompt, row, row & (dec_seq - 1))
    prev1 = jnp.where(is_prompt, jnp.broadcast_to(carry_ref[7:8, :], (TM, CONV_DIM)), s1_ref[...])
    prev2 = jnp.where(is_prompt,
                      jnp.where(row == 0, jnp.broadcast_to(carry_ref[6:7, :], (TM, CONV_DIM)),
                                jnp.broadcast_to(carry_ref[7:8, :], (TM, CONV_DIM))),
                      s2_ref[...])
    z1 = jnp.where(pos >= 1, pltpu.roll(z, 1, axis=0), prev1)
    z2 = jnp.where(pos >= 2, pltpu.roll(z, 2, axis=0), prev2)
    cw = convw_ref[...]
    ya = bg * (cw[0:1, :] * z2 + cw[1:2, :] * z1 + cw[2:3, :] * z)

    @pl.when(is_prompt)
    def _():
        carry_ref[...] = z[TM - SUBLANES:, :]
        tail_ref[0] = z[TM - SUBLANES:, :]

    @pl.when(jnp.logical_not(is_prompt))
    def _():
        zs_ref[...] = z

    pb = jnp.dot(x, w_in_ref[:, OFF_B:OFF_C], preferred_element_type=F32)
    uv = _gelu(pb)
    u = uv[:, :SG_DIM]
    vn = _layer_norm(uv[:, SG_DIM:], lng_ref[...], lnb_ref[...])

    @pl.when(jnp.logical_not(is_prompt))
    def _():
        vrows_ref[...] = vn

    vb = vn.astype(BF16)
    low_half = lax.broadcasted_iota(jnp.int32, (CHUNK, LANES), 1) < SG_HEAD
    s_rows = []
    for c in range(TM // CHUNK):
        s_cols = []
        for p in range(SG_DIM // LANES):
            vcp = vb[c * CHUNK:(c + 1) * CHUNK, p * LANES:(p + 1) * LANES]
            sa = jnp.dot(wsp_ref[2 * p], vcp, preferred_element_type=F32)
            sb = jnp.dot(wsp_ref[2 * p + 1], vcp, preferred_element_type=F32)
            s_cols.append(jnp.where(low_half, sa, sb))
        s_rows.append(jnp.concatenate(s_cols, axis=1) + bsp_ref[...])
    yb = u * jnp.concatenate(s_rows, axis=0)

    pc = jnp.dot(x, w_in_ref[:, OFF_C:OFF_G], preferred_element_type=F32)
    q_ref[...] = pc[:, :ATT_DIM]
    k_ref[...] = pc[:, ATT_DIM:2 * ATT_DIM]
    v_ref[...] = pc[:, 2 * ATT_DIM:]

    ga = _sigmoid(jnp.dot(x, w_in_ref[:, OFF_G:OFF_G + D_MODEL], preferred_element_type=F32))
    mab = ga * _dot(ya, wua_ref[...])
    gb = _sigmoid(jnp.dot(x, w_in_ref[:, OFF_G + D_MODEL:OFF_G + 2 * D_MODEL], preferred_element_type=F32))
    mab_ref[...] = mab + gb * _dot(yb, wub_ref[...])
    gc_ref[...] = _sigmoid(jnp.dot(x, w_in_ref[:, OFF_G + 2 * D_MODEL:], preferred_element_type=F32))


def _inproj_call(l, x, w_in, conv_w, sg_ln_g, sg_ln_b, wsp2, bsp2, w_up_a, w_up_b, s1, s2,
                 n_prompt, seq, n_sample, dec_seq):
    n = n_prompt + n_sample
    n_ptiles = n_prompt // TM
    n_stiles = n_sample // TM
    tiles_per_seq = seq // TM
    batch = n_prompt // seq

    def stile(i):
        return jnp.maximum(i - n_ptiles, 0)

    def mode(i):
        return jnp.where(i < n_ptiles, 0, 1)

    row_spec = lambda w: pl.BlockSpec((TM, w), lambda i: (i, 0))
    srow_spec = lambda w: pl.BlockSpec((TM, w), lambda i: (stile(i), 0))
    kern = functools.partial(_inproj_kernel, n_ptiles, tiles_per_seq, dec_seq)
    return pl.pallas_call(
        kern,
        grid=(n_ptiles + n_stiles,),
        in_specs=[
            row_spec(D_MODEL),
            pl.BlockSpec((None, D_MODEL, IN_DIM), lambda i: (l, 0, 0), pipeline_mode=pl.Buffered(1)),
            pl.BlockSpec((None, CONV_W, CONV_DIM), lambda i: (l, 0, 0)),
            pl.BlockSpec((None, 1, SG_DIM), lambda i: (l, 0, 0)),
            pl.BlockSpec((None, 1, SG_DIM), lambda i: (l, 0, 0)),
            pl.BlockSpec((None, None, SG_GROUPS, CHUNK, CHUNK), lambda i: (l, mode(i), 0, 0, 0)),
            pl.BlockSpec((None, None, CHUNK, SG_DIM), lambda i: (l, mode(i), 0, 0)),
            pl.BlockSpec((None, CONV_DIM, D_MODEL), lambda i: (l, 0, 0)),
            pl.BlockSpec((None, SG_DIM, D_MODEL), lambda i: (l, 0, 0)),
            srow_spec(CONV_DIM),
            srow_spec(CONV_DIM),
        ],
        out_specs=[
            row_spec(ATT_DIM), row_spec(ATT_DIM), row_spec(ATT_DIM),
            row_spec(D_MODEL), row_spec(D_MODEL),
            pl.BlockSpec((1, SUBLANES, CONV_DIM),
                         lambda i: (jnp.minimum(i // tiles_per_seq, batch - 1), 0, 0)),
            srow_spec(CONV_DIM),
            srow_spec(SG_DIM),
        ],
        out_shape=[
            jax.ShapeDtypeStruct((n, ATT_DIM), F32), jax.ShapeDtypeStruct((n, ATT_DIM), F32),
            jax.ShapeDtypeStruct((n, ATT_DIM), F32),
            jax.ShapeDtypeStruct((n, D_MODEL), F32), jax.ShapeDtypeStruct((n, D_MODEL), F32),
            jax.ShapeDtypeStruct((batch, SUBLANES, CONV_DIM), F32),
            jax.ShapeDtypeStruct((n_sample, CONV_DIM), F32),
            jax.ShapeDtypeStruct((n_sample, SG_DIM), F32),
        ],
        scratch_shapes=[pltpu.VMEM((SUBLANES, CONV_DIM), F32)],
        compiler_params=_params(1),
        name="inproj_mix",
    )(x, w_in, conv_w, sg_ln_g, sg_ln_b, wsp2, bsp2, w_up_a, w_up_b, s1, s2)


def _head_masks(rows):
    lane_head = lax.broadcasted_iota(jnp.int32, (rows, LANES), 1) // HEAD_DIM
    return [lane_head == h for h in range(HPG)]


def _stack_heads(qb, masks):
    return jnp.concatenate([jnp.where(m, qb, 0.0) for m in masks], axis=0)


def _unstack_heads(stacked, masks, r):
    out = None
    for h, m in enumerate(masks):
        part = jnp.where(m, stacked[h * r:(h + 1) * r], 0.0)
        out = part if out is None else out + part
    return out


def _mix_groups(outs, lses):
    mx = functools.reduce(jnp.maximum, lses)
    ws = [jnp.exp(ls - mx) for ls in lses]
    num = functools.reduce(lambda a, b: a + b, [w * o for w, o in zip(ws, outs)])
    return num / functools.reduce(lambda a, b: a + b, ws)


def _bias_table(bias_ref, base, steps):
    def body(j, tab):
        return jnp.where(steps == j, bias_ref[base + j], tab)

    return lax.fori_loop(0, N_KEYS, body, jnp.full(steps.shape, NEG, F32))


def _attn_prompt_kernel(seq, bias_ref, q_ref, k_ref, v_ref, o_ref, tab_scr, *scr):
    b = pl.program_id(0)
    g = pl.program_id(1)
    o_scr, l_scr = scr[:N_ATT], scr[N_ATT:]
    masks = _head_masks(Q_BLK)
    scale = HEAD_DIM ** -0.5
    col = lax.broadcasted_iota(jnp.int32, (HPG * Q_BLK, 2 * Q_BLK), 1)

    def build_table(gi):
        qi = lax.broadcasted_iota(jnp.int32, (Q_BLK, 2 * Q_BLK), 0)
        kc = lax.broadcasted_iota(jnp.int32, (Q_BLK, 2 * Q_BLK), 1)
        for h in range(HPG):
            tab_scr[gi, h * Q_BLK:(h + 1) * Q_BLK, :] = _bias_table(bias_ref, (gi * HPG + h) * N_KEYS,
                                                                    qi + Q_BLK - kc)

    def run_group(gi, dil):
        rows_per_class = seq // dil
        n_blk = rows_per_class // Q_BLK

        def ld(ref, start):
            if dil > 1:
                return ref[pl.ds(start, Q_BLK, stride=dil), :]
            return ref[pl.ds(start, Q_BLK), :]

        def st(ref, start, val):
            if dil > 1:
                ref[pl.ds(start, Q_BLK, stride=dil), :] = val
            else:
                ref[pl.ds(start, Q_BLK), :] = val

        def block(it, carry):
            r = it // n_blk
            mb = it % n_blk
            cur = r + dil * Q_BLK * mb
            prev = r + dil * Q_BLK * jnp.maximum(mb - 1, 0)
            qb = ld(q_ref, cur) * scale
            kw = jnp.concatenate([ld(k_ref, prev), ld(k_ref, cur)], axis=0)
            vw = jnp.concatenate([ld(v_ref, prev), ld(v_ref, cur)], axis=0)
            s = _dot_nt(_stack_heads(qb, masks), kw) + tab_scr[gi]
            s = jnp.where(jnp.logical_or(col >= Q_BLK, mb > 0), s, NEG)
            m = jnp.max(s, axis=1, keepdims=True)
            p = jnp.exp(s - m)
            den = jnp.sum(p, axis=1, keepdims=True)
            pv = _dot(p, vw)
            st(o_scr[gi], cur, _unstack_heads(pv / den, masks, Q_BLK))
            st(l_scr[gi], cur, _unstack_heads(m + jnp.log(den), masks, Q_BLK))
            return carry

        lax.fori_loop(0, dil * n_blk, block, 0, unroll=4)

    for gi, (_, dil) in enumerate(ATT_GROUPS):
        @pl.when(jnp.logical_and(g == gi, b == 0))
        def _(gi=gi):
            build_table(gi)

        @pl.when(g == gi)
        def _(gi=gi, dil=dil):
            run_group(gi, dil)

    @pl.when(g == N_ATT - 1)
    def _():
        o_ref[...] = _mix_groups([s[...] for s in o_scr], [s[...] for s in l_scr])


def _attn_prompt_call(q, k, v, bias_flat, batch, seq):
    blk = lambda: pl.BlockSpec((seq, LANES), lambda b, g: (b, g))
    return pl.pallas_call(
        functools.partial(_attn_prompt_kernel, seq),
        grid=(batch, N_ATT),
        in_specs=[pl.BlockSpec(memory_space=pltpu.SMEM), blk(), blk(), blk()],
        out_specs=pl.BlockSpec((seq, ATT_OUT), lambda b, g: (b, 0)),
        out_shape=jax.ShapeDtypeStruct((batch * seq, ATT_OUT), F32),
        scratch_shapes=[pltpu.VMEM((N_ATT, HPG * Q_BLK, 2 * Q_BLK), F32)]
                       + [pltpu.VMEM((seq, LANES), F32)] * (2 * N_ATT),
        compiler_params=_params(2),
        name="attn_prompt",
    )(bias_flat, q, k, v)


def _attn_sample_kernel(dec_seq, pasts, bias_ref, q_ref, k_ref, v_ref, kc0, vc0, kc1, vc1, kc2, vc2,
                        o_ref, tc0, tc1, tc2, tn_scr):
    kcs, vcs, tcs = (kc0, kc1, kc2), (vc0, vc1, vc2), (tc0, tc1, tc2)
    masks = _head_masks(dec_seq)
    scale = HEAD_DIM ** -0.5

    @pl.when(pl.program_id(0) == 0)
    def _():
        for gi, (_, dil) in enumerate(ATT_GROUPS):
            past = pasts[gi]
            shift = dil.bit_length() - 1

            def steps(dist, dil=dil, shift=shift):
                return jnp.where((dist & (dil - 1)) == 0, dist >> shift, -1)

            qi_c = lax.broadcasted_iota(jnp.int32, (dec_seq, past), 0)
            row_c = lax.broadcasted_iota(jnp.int32, (dec_seq, past), 1)
            qi_n = lax.broadcasted_iota(jnp.int32, (dec_seq, dec_seq), 0)
            row_n = lax.broadcasted_iota(jnp.int32, (dec_seq, dec_seq), 1)
            for h in range(HPG):
                base = (gi * HPG + h) * N_KEYS
                tcs[gi][h * dec_seq:(h + 1) * dec_seq, :] = _bias_table(bias_ref, base, steps(past + qi_c - row_c))
                tn_scr[gi, h * dec_seq:(h + 1) * dec_seq, :] = _bias_table(bias_ref, base, steps(qi_n - row_n))

    for s_i in range(SEQ_BLK):
        rows = slice(s_i * dec_seq, (s_i + 1) * dec_seq)
        outs, lses = [], []
        for gi in range(N_ATT):
            cols = slice(gi * LANES, (gi + 1) * LANES)
            qs = _stack_heads(q_ref[rows, cols] * scale, masks)
            k_new, v_new = k_ref[rows, cols], v_ref[rows, cols]
            s_c = _dot(qs, kcs[gi][s_i]) + tcs[gi][...]
            s_n = _dot_nt(qs, k_new) + tn_scr[gi]
            m = jnp.maximum(jnp.max(s_c, axis=1, keepdims=True), jnp.max(s_n, axis=1, keepdims=True))
            p_c = jnp.exp(s_c - m)
            p_n = jnp.exp(s_n - m)
            den = jnp.sum(p_c, axis=1, keepdims=True) + jnp.sum(p_n, axis=1, keepdims=True)
            pv = _dot_nt(p_c, vcs[gi][s_i]) + _dot(p_n, v_new)
            outs.append(_unstack_heads(pv / den, masks, dec_seq))
            lses.append(_unstack_heads(m + jnp.log(den), masks, dec_seq))
        o_ref[rows, :] = _mix_groups(outs, lses)


def _attn_sample_call(l, q, k, v, caches_t, bias_flat, n_prompt, dec_batch, dec_seq):
    rows = SEQ_BLK * dec_seq
    off = n_prompt // rows
    pasts = tuple(caches_t[2 * gi].shape[3] for gi in range(N_ATT))
    new_spec = pl.BlockSpec((rows, ATT_DIM), lambda i: (off + i, 0))
    cache_specs = [pl.BlockSpec((None, SEQ_BLK, LANES, c.shape[3]), lambda i: (l, i, 0, 0)) for c in caches_t]
    return pl.pallas_call(
        functools.partial(_attn_sample_kernel, dec_seq, pasts),
        grid=(dec_batch // SEQ_BLK,),
        in_specs=[pl.BlockSpec(memory_space=pltpu.SMEM), new_spec, new_spec, new_spec] + cache_specs,
        out_specs=pl.BlockSpec((rows, ATT_OUT), lambda i: (i, 0)),
        out_shape=jax.ShapeDtypeStruct((dec_batch * dec_seq, ATT_OUT), F32),
        scratch_shapes=[pltpu.VMEM((HPG * dec_seq, p), F32) for p in pasts]
                       + [pltpu.VMEM((N_ATT, HPG * dec_seq, dec_seq), F32)],
        compiler_params=_params(1),
        name="attn_sample",
    )(bias_flat, q, k, v, *caches_t)


def _merge_kernel(alpha, x_ref, mab_ref, gc_ref, yc_ref, wuc_ref, wo_ref, g_ref, b_ref, wrt_ref, br_ref,
                  x1_ref, x1g_ref, w_ref, slot_ref, tcnt_ref, tbase_ref, cnt_scr):
    @pl.when(pl.program_id(0) == 0)
    def _():
        cnt_scr[...] = jnp.zeros_like(cnt_scr)

    merged = mab_ref[...] + gc_ref[...] * _dot(yc_ref[...], wuc_ref[...])
    x1 = _layer_norm(alpha * x_ref[...] + _dot(merged, wo_ref[...]), g_ref[...], b_ref[...])
    x1_ref[...] = x1
    for c in range(LANE_CHUNKS):
        x1g_ref[pl.ds(c, TM, stride=LANE_CHUNKS), :] = x1[:, c * LANES:(c + 1) * LANES]

    logits = lax.dot_general(wrt_ref[...], x1, (((1,), (1,)), ((), ())),
                             precision=lax.Precision.HIGHEST, preferred_element_type=F32) + br_ref[...]
    eio = lax.broadcasted_iota(jnp.int32, (N_EXPERTS, TM), 0)
    vals, idxs = [], []
    for _ in range(TOP_K):
        mv = jnp.max(logits, axis=0, keepdims=True)
        ix = jnp.min(jnp.where(logits == mv, eio, N_EXPERTS), axis=0, keepdims=True)
        vals.append(mv)
        idxs.append(ix)
        logits = jnp.where(eio == ix, -jnp.inf, logits)
    tv = jnp.concatenate(vals, axis=0)
    e = jnp.exp(tv - tv[0:1, :])
    w_ref[...] = e / jnp.sum(e, axis=0, keepdims=True)

    onehots = [eio == ix for ix in idxs]
    chosen = functools.reduce(lambda a, c: a + c, [jnp.where(oh, 1.0, 0.0) for oh in onehots])
    earlier = (lax.broadcasted_iota(jnp.int32, (TM, TM), 0) < lax.broadcasted_iota(jnp.int32, (TM, TM), 1))
    within = _dot(chosen, jnp.where(earlier, 1.0, 0.0))
    tile_cnt = jnp.broadcast_to(jnp.sum(chosen, axis=1, keepdims=True), (N_EXPERTS, LANES))
    lower = (lax.broadcasted_iota(jnp.int32, (N_EXPERTS, N_EXPERTS), 1)
             < lax.broadcasted_iota(jnp.int32, (N_EXPERTS, N_EXPERTS), 0))
    place = within + _dot(jnp.where(lower, 1.0, 0.0), tile_cnt)[:, 0:1]
    slot_ref[...] = jnp.concatenate(
        [jnp.sum(jnp.where(oh, place, 0.0), axis=0, keepdims=True) for oh in onehots],
        axis=0).astype(jnp.int32) * LANE_CHUNKS
    tcnt_ref[...] = tile_cnt.astype(jnp.int32)
    tbase_ref[...] = cnt_scr[...].astype(jnp.int32)
    cnt_scr[...] = cnt_scr[...] + tile_cnt


def _merge_call(l, alpha, x, mab, gc, yc, w_up_c, w_o, ln_g, ln_b, w_router_t, b_router):
    n = x.shape[0]
    row_spec = lambda w: pl.BlockSpec((TM, w), lambda i: (i, 0))
    vec_spec = pl.BlockSpec((None, 1, D_MODEL), lambda i: (l, 0, 0))
    return pl.pallas_call(
        functools.partial(_merge_kernel, alpha),
        grid=(n // TM,),
        in_specs=[
            row_spec(D_MODEL), row_spec(D_MODEL), row_spec(D_MODEL), row_spec(ATT_OUT),
            pl.BlockSpec((None, ATT_OUT, D_MODEL), lambda i: (l, 0, 0)),
            pl.BlockSpec((None, D_MODEL, D_MODEL), lambda i: (l, 0, 0)),
            vec_spec, vec_spec,
            pl.BlockSpec((None, N_EXPERTS, D_MODEL), lambda i: (l, 0, 0)),
            pl.BlockSpec((None, N_EXPERTS, 1), lambda i: (l, 0, 0)),
        ],
        out_specs=[
            row_spec(D_MODEL),
            pl.BlockSpec((TM * LANE_CHUNKS, LANES), lambda i: (i, 0)),
            pl.BlockSpec((TOP_K, TM), lambda i: (0, i)),
            pl.BlockSpec((TOP_K, TM), lambda i: (0, i)),
            pl.BlockSpec((None, N_EXPERTS, LANES), lambda i: (i, 0, 0)),
            pl.BlockSpec((None, N_EXPERTS, LANES), lambda i: (i, 0, 0)),
        ],
        out_shape=[
            jax.ShapeDtypeStruct((n, D_MODEL), F32),
            jax.ShapeDtypeStruct((n * LANE_CHUNKS, LANES), F32),
            jax.ShapeDtypeStruct((TOP_K, n), F32),
            jax.ShapeDtypeStruct((TOP_K, n), jnp.int32),
            jax.ShapeDtypeStruct((n // TM, N_EXPERTS, LANES), jnp.int32),
            jax.ShapeDtypeStruct((n // TM, N_EXPERTS, LANES), jnp.int32),
        ],
        scratch_shapes=[pltpu.VMEM((N_EXPERTS, LANES), F32)],
        compiler_params=_params(1),
        name="merge_ln1_router",
    )(x, mab, gc, yc, w_up_c, w_o, ln_g, ln_b, w_router_t, b_router)


def _destride(ref, rows):
    return jnp.concatenate([ref[pl.ds(c, rows, stride=LANE_CHUNKS), :] for c in range(LANE_CHUNKS)], axis=1)


def _slab_rows(ref, row, count=1):
    return ref.at[pl.ds(pl.multiple_of(row, LANE_CHUNKS), count * LANE_CHUNKS), :]


def _start_expert_blocks(tile, tcnt_ref, hstart_ref, make_copy):
    big_bits = BIG_RUN.bit_length() - 1

    def per_expert(e, staged):
        cnt = tcnt_ref[tile * N_EXPERTS + e]
        first = hstart_ref[tile * N_EXPERTS + e]
        n_big = cnt >> big_bits

        def big(j, carry):
            make_copy(staged + j * BIG_RUN, first + j * BIG_RUN, BIG_RUN).start()
            return carry

        lax.fori_loop(0, n_big, big, 0)
        for bit in reversed(range(big_bits)):
            size = 1 << bit
            done = (cnt >> (bit + 1)) << (bit + 1)

            @pl.when((cnt & size) != 0)
            def _(size=size, done=done):
                make_copy(staged + done, first + done, size).start()
        return staged + cnt

    lax.fori_loop(0, N_EXPERTS, per_expert, 0)


def _dispatch_kernel(n_tiles, lt_ref, has_ref, nv_ref, tcnt_ref, hstart_ref, slot_ref, x1g_ref, xs_hbm,
                     zbuf, stg, zsem, sem):
    tile_rows = TM_E * LANE_CHUNKS
    i = pl.program_id(0)
    n_steps = pl.num_programs(0)
    buf = i % 2
    staged_all = lambda b: pltpu.make_async_copy(stg.at[b], xs_hbm.at[pl.ds(0, TM * TOP_K * LANE_CHUNKS), :],
                                                 sem.at[b])

    @pl.when(i == 0)
    def _():
        zbuf[...] = jnp.zeros_like(zbuf)
        zero_copy = lambda t: pltpu.make_async_copy(zbuf, xs_hbm.at[pl.ds(t * tile_rows, tile_rows), :], zsem)
        for e in range(N_EXPERTS):
            @pl.when(has_ref[e] > 0)
            def _(e=e):
                zero_copy(lt_ref[e]).start()
        lax.fori_loop(nv_ref[0], n_tiles, lambda t, c: (zero_copy(t).start(), c)[1], 0)
        for e in range(N_EXPERTS):
            @pl.when(has_ref[e] > 0)
            def _(e=e):
                zero_copy(lt_ref[e]).wait()
        lax.fori_loop(nv_ref[0], n_tiles, lambda t, c: (zero_copy(t).wait(), c)[1], 0)

    def step(b):
        @pl.when(i >= 2)
        def _():
            staged_all(b).wait()

        def place(t, carry):
            slab = _slab_rows(x1g_ref, t * LANE_CHUNKS)[...]
            for kk in range(TOP_K):
                _slab_rows(stg.at[b], slot_ref[0, 0, t * TOP_K + kk])[...] = slab
            return carry

        lax.fori_loop(0, TM, place, 0, unroll=DMA_UNROLL)
        _start_expert_blocks(i, tcnt_ref, hstart_ref, lambda staged, row, size: pltpu.make_async_copy(
            _slab_rows(stg.at[b], staged * LANE_CHUNKS, size), _slab_rows(xs_hbm, row * LANE_CHUNKS, size),
            sem.at[b]))

        @pl.when(i == n_steps - 1)
        def _():
            staged_all(b).wait()

        @pl.when(jnp.logical_and(i == n_steps - 1, i >= 1))
        def _():
            staged_all(1 - b).wait()

    for b in range(2):
        @pl.when(buf == b)
        def _(b=b):
            step(b)


def _dispatch_call(last_tile, has_rows, n_valid, tcnt, hstart, slot, x1g, n_tiles):
    n = slot.shape[0] * TM
    grid_spec = pltpu.PrefetchScalarGridSpec(
        num_scalar_prefetch=5,
        grid=(n // TM,),
        in_specs=[
            pl.BlockSpec((1, 1, TM * TOP_K), lambda i, *_: (i, 0, 0), memory_space=pltpu.SMEM),
            pl.BlockSpec((TM * LANE_CHUNKS, LANES), lambda i, *_: (i, 0)),
        ],
        out_specs=pl.BlockSpec(memory_space=pl.ANY),
        scratch_shapes=[pltpu.VMEM((TM_E * LANE_CHUNKS, LANES), F32),
                        pltpu.VMEM((2, TM * TOP_K * LANE_CHUNKS, LANES), F32),
                        pltpu.SemaphoreType.DMA(()), pltpu.SemaphoreType.DMA((2,))],
    )
    return pl.pallas_call(
        functools.partial(_dispatch_kernel, n_tiles),
        grid_spec=grid_spec,
        out_shape=jax.ShapeDtypeStruct((n_tiles * TM_E * LANE_CHUNKS, LANES), F32),
        compiler_params=_params(1),
        name="moe_dispatch",
    )(last_tile, has_rows, n_valid, tcnt, hstart, slot, x1g)


def _moe_ffn_kernel(te_ref, nv_ref, xs_ref, wg_ref, bg_ref, wu_ref, bu_ref, wd_ref, bd_ref, ys_ref, w_b16):
    t = pl.program_id(0)

    @pl.when(t < nv_ref[0])
    def _():
        @pl.when(jnp.logical_or(t == 0, te_ref[t] != te_ref[jnp.maximum(t - 1, 0)]))
        def _():
            for j, w_ref in enumerate((wg_ref, wu_ref, wd_ref)):
                w_b16[j] = w_ref[...].astype(BF16)

        x = _destride(xs_ref, TM_E).astype(BF16)
        gl = jnp.minimum(jnp.dot(x, w_b16[0], preferred_element_type=F32) + bg_ref[...], SWIGLU_LIMIT)
        ul = jnp.clip(jnp.dot(x, w_b16[1], preferred_element_type=F32) + bu_ref[...], -SWIGLU_LIMIT, SWIGLU_LIMIT)
        hid = gl * _sigmoid(SWIGLU_ALPHA * gl) * (ul + 1.0)
        y = jnp.dot(hid.astype(BF16), w_b16[2], preferred_element_type=F32) + bd_ref[...]
        for c in range(LANE_CHUNKS):
            ys_ref[pl.ds(c, TM_E, stride=LANE_CHUNKS), :] = y[:, c * LANES:(c + 1) * LANES]

    @pl.when(t >= nv_ref[0])
    def _():
        ys_ref[...] = jnp.zeros_like(ys_ref)


def _moe_ffn_call(l, tile_expert, n_valid, xs, w_gate, b_gate, w_up, b_up, w_down, b_down):
    n_tiles = tile_expert.shape[0]
    mat_spec = pl.BlockSpec((None, None, D_MODEL, D_MODEL), lambda t, te, nv: (l, te[t], 0, 0))
    vec_spec = pl.BlockSpec((None, None, 1, D_MODEL), lambda t, te, nv: (l, te[t], 0, 0))
    in_slab = pl.BlockSpec((TM_E * LANE_CHUNKS, LANES), lambda t, te, nv: (jnp.minimum(t, nv[0] - 1), 0))
    grid_spec = pltpu.PrefetchScalarGridSpec(
        num_scalar_prefetch=2,
        grid=(n_tiles,),
        in_specs=[in_slab, mat_spec, vec_spec, mat_spec, vec_spec, mat_spec, vec_spec],
        out_specs=pl.BlockSpec((TM_E * LANE_CHUNKS, LANES), lambda t, te, nv: (t, 0)),
        scratch_shapes=[pltpu.VMEM((3, D_MODEL, D_MODEL), BF16)],
    )
    return pl.pallas_call(
        _moe_ffn_kernel,
        grid_spec=grid_spec,
        out_shape=jax.ShapeDtypeStruct((n_tiles * TM_E * LANE_CHUNKS, LANES), F32),
        compiler_params=_params(1),
        name="moe_ffn",
    )(tile_expert, n_valid, xs, w_gate, b_gate, w_up, b_up, w_down, b_down)


def _combine_kernel(alpha, tcnt_ref, hstart_ref, slot_ref, w_ref, x1_ref, ys_hbm, g_ref, b_ref, x2_ref,
                    stg, mixed, sem):
    i = pl.program_id(0)
    n_steps = pl.num_programs(0)
    buf = i % 2

    def fetch(tile, b):
        _start_expert_blocks(tile, tcnt_ref, hstart_ref, lambda staged, row, size: pltpu.make_async_copy(
            _slab_rows(ys_hbm, row * LANE_CHUNKS, size), _slab_rows(stg.at[b], staged * LANE_CHUNKS, size),
            sem.at[b]))

    def step(b):
        @pl.when(i == 0)
        def _():
            fetch(0, b)

        @pl.when(i + 1 < n_steps)
        def _():
            fetch(i + 1, 1 - b)

        pltpu.make_async_copy(ys_hbm.at[pl.ds(0, TM * TOP_K * LANE_CHUNKS), :], stg.at[b], sem.at[b]).wait()

        def mix(t, carry):
            c0 = t * TOP_K
            acc = w_ref[0, 0, c0] * _slab_rows(stg.at[b], slot_ref[0, 0, c0])[...]
            for kk in range(1, TOP_K):
                acc = acc + w_ref[0, 0, c0 + kk] * _slab_rows(stg.at[b], slot_ref[0, 0, c0 + kk])[...]
            _slab_rows(mixed, t * LANE_CHUNKS)[...] = acc
            return carry

        lax.fori_loop(0, TM, mix, 0, unroll=DMA_UNROLL)

    for b in range(2):
        @pl.when(buf == b)
        def _(b=b):
            step(b)

    x2_ref[...] = _layer_norm(alpha * x1_ref[...] + _destride(mixed, TM), g_ref[...], b_ref[...])


def _combine_call(l, alpha, tcnt, hstart, slot, top_w, x1, ys, ln_g, ln_b):
    n = x1.shape[0]
    vec_spec = pl.BlockSpec((None, 1, D_MODEL), lambda i, *_: (l, 0, 0))
    choice_spec = pl.BlockSpec((1, 1, TM * TOP_K), lambda i, *_: (i, 0, 0), memory_space=pltpu.SMEM)
    grid_spec = pltpu.PrefetchScalarGridSpec(
        num_scalar_prefetch=2,
        grid=(n // TM,),
        in_specs=[
            choice_spec, choice_spec,
            pl.BlockSpec((TM, D_MODEL), lambda i, *_: (i, 0)),
            pl.BlockSpec(memory_space=pl.ANY),
            vec_spec, vec_spec,
        ],
        out_specs=pl.BlockSpec((TM, D_MODEL), lambda i, *_: (i, 0)),
        scratch_shapes=[pltpu.VMEM((2, TM * TOP_K * LANE_CHUNKS, LANES), F32),
                        pltpu.VMEM((TM * LANE_CHUNKS, LANES), F32),
                        pltpu.SemaphoreType.DMA((2,))],
    )
    return pl.pallas_call(
        functools.partial(_combine_kernel, alpha),
        grid_spec=grid_spec,
        out_shape=jax.ShapeDtypeStruct((n, D_MODEL), F32),
        compiler_params=_params(1),
        name="moe_combine_ln2",
    )(tcnt, hstart, slot, top_w, x1, ys, ln_g, ln_b)


def _route(tcnt, tbase, n_tiles):
    experts = jnp.arange(N_EXPERTS, dtype=jnp.int32)
    counts = tbase[-1] + tcnt[-1]
    padded = ((counts + TM_E - 1) // TM_E) * TM_E
    ends = jnp.cumsum(padded)
    hstart = (ends - padded)[None, :] + tbase
    n_valid = ends[-1] // TM_E
    last_expert = jnp.max(jnp.where(counts > 0, experts, 0))
    tile_start = jnp.arange(n_tiles, dtype=jnp.int32) * TM_E
    tile_expert = jnp.minimum(jnp.sum(ends[None, :] <= tile_start[:, None], axis=1), last_expert)
    return (tile_expert.astype(jnp.int32), n_valid.astype(jnp.int32).reshape(1),
            tcnt.reshape(-1), hstart.astype(jnp.int32).reshape(-1),
            (ends // TM_E - 1).astype(jnp.int32), (counts > 0).astype(jnp.int32))


def _t5_bucket(dist):
    max_exact = NUM_BUCKETS // 2
    distf = jnp.maximum(dist, 1).astype(F32)
    large = max_exact + (jnp.log(distf / max_exact) / math.log(MAX_DISTANCE / max_exact)
                         * (NUM_BUCKETS - max_exact)).astype(jnp.int32)
    large = jnp.minimum(large, NUM_BUCKETS - 1)
    return jnp.where(dist < max_exact, dist, large)


def _group_bias(rel_bias, gi, dil):
    j = jnp.arange(N_KEYS, dtype=jnp.int32)
    return rel_bias[_t5_bucket(dil * j)][:, gi * HPG:(gi + 1) * HPG].T.astype(F32)


def _step_bias(rel_bias):
    return jnp.stack([_group_bias(rel_bias, gi, dil) for gi, (_, dil) in enumerate(ATT_GROUPS)], axis=0).reshape(-1)


def _spatial_tables(w_sp, b_sp, dec_seq):
    depth = w_sp.shape[0]
    tril = jnp.tril(jnp.ones((CHUNK, CHUNK), bool))
    wp = jnp.where(tril, w_sp, 0.0)
    ws_small = jnp.where(tril[:dec_seq, :dec_seq], w_sp[:, :, :dec_seq, :dec_seq], 0.0)
    eye = jnp.eye(CHUNK // dec_seq, dtype=w_sp.dtype)
    ws = jnp.einsum("ab,lgij->lgaibj", eye, ws_small).reshape(depth, SG_GROUPS, CHUNK, CHUNK)
    bp = jnp.repeat(jnp.swapaxes(b_sp, 1, 2), SG_HEAD, axis=2)
    bs = jnp.tile(bp[:, :dec_seq], (1, CHUNK // dec_seq, 1))
    return jnp.stack([wp, ws], axis=1).astype(BF16), jnp.stack([bp, bs], axis=1)


def kernel(x_prompt, x_sample, state_conv, cache_k_w128, cache_v_w128, cache_k_w512, cache_v_w512,
           cache_k_w2048, cache_v_w2048, w_in, conv_w, sg_ln_g, sg_ln_b, w_sp, b_sp, rel_bias,
           w_up_a, w_up_b, w_up_c, w_o, ln1_g, ln1_b, w_router, b_router, w_gate, b_gate, w_up, b_up,
           w_down, b_down, ln2_g, ln2_b):
    batch, seq, _ = x_prompt.shape
    dec_batch, dec_seq, _ = x_sample.shape
    depth = w_in.shape[0]
    n_prompt, n_sample = batch * seq, dec_batch * dec_seq
    n = n_prompt + n_sample
    alpha = (2 * depth) ** 0.25
    assert seq % TM == 0 and n_sample % TM == 0 and dec_seq & (dec_seq - 1) == 0 and CHUNK % dec_seq == 0
    assert all(seq % (dil * Q_BLK) == 0 and dil & (dil - 1) == 0 for _, dil in ATT_GROUPS)
    assert dec_batch % SEQ_BLK == 0
    n_tiles = (n * TOP_K) // TM_E + N_EXPERTS

    caches_t = [jnp.transpose(c, (0, 1, 3, 4, 2)).reshape(c.shape[0], c.shape[1], LANES, c.shape[2]) for c in
                (cache_k_w128, cache_v_w128, cache_k_w512, cache_v_w512, cache_k_w2048, cache_v_w2048)]
    bias_flat = _step_bias(rel_bias)
    wsp2, bsp2 = _spatial_tables(w_sp, b_sp, dec_seq)
    w_in_b, w_up_a_b, w_up_b_b, w_up_c_b, w_o_b = (w.astype(BF16) for w in (w_in, w_up_a, w_up_b, w_up_c, w_o))
    vec = lambda a: a.reshape(depth, 1, a.shape[-1])
    w_router_t = jnp.swapaxes(w_router, 1, 2)
    b_router_c = b_router.reshape(depth, N_EXPERTS, 1)
    expert_vec = lambda a: a.reshape(depth, N_EXPERTS, 1, a.shape[-1])

    x = jnp.concatenate([x_prompt.reshape(n_prompt, D_MODEL), x_sample.reshape(n_sample, D_MODEL)], axis=0)
    outs = {name: [] for name in ("conv_p", "conv_s", "sgv", "kp", "vp", "ks", "vs")}
    for l in range(depth):
        s1 = jnp.zeros((dec_batch, dec_seq, CONV_DIM), F32).at[:, 0].set(state_conv[l, :, 1])
        s2 = jnp.zeros((dec_batch, dec_seq, CONV_DIM), F32).at[:, 0].set(state_conv[l, :, 0])
        s2 = s2.at[:, 1].set(state_conv[l, :, 1])
        q, k, v, mab, gc, tail, z_s, v_rows = _inproj_call(
            l, x, w_in_b, conv_w, vec(sg_ln_g), vec(sg_ln_b), wsp2, bsp2, w_up_a_b, w_up_b_b,
            s1.reshape(n_sample, CONV_DIM), s2.reshape(n_sample, CONV_DIM), n_prompt, seq, n_sample, dec_seq)
        yc_p = _attn_prompt_call(q, k, v, bias_flat, batch, seq)
        yc_s = _attn_sample_call(l, q, k, v, caches_t, bias_flat, n_prompt, dec_batch, dec_seq)
        yc = jnp.concatenate([yc_p, yc_s], axis=0)
        x1, x1g, top_w, slot, tcnt, tbase = _merge_call(
            l, alpha, x, mab, gc, yc, w_up_c_b, w_o_b, vec(ln1_g), vec(ln1_b), w_router_t, b_router_c)
        tile_expert, n_valid, tcnt, hstart, last_tile, has_rows = _route(tcnt[:, :, 0], tbase[:, :, 0], n_tiles)
        by_tile = lambda a: a.reshape(TOP_K, n // TM, TM).transpose(1, 2, 0).reshape(n // TM, 1, TM * TOP_K)
        slot, top_w = by_tile(slot), by_tile(top_w)
        xs = _dispatch_call(last_tile, has_rows, n_valid, tcnt, hstart, slot, x1g, n_tiles)
        ys = _moe_ffn_call(l, tile_expert, n_valid, xs, w_gate, expert_vec(b_gate),
                           w_up, expert_vec(b_up), w_down, expert_vec(b_down))
        x = _combine_call(l, alpha, tcnt, hstart, slot, top_w, x1, ys, vec(ln2_g), vec(ln2_b))

        outs["conv_p"].append(tail[:, SUBLANES - (CONV_W - 1):])
        outs["conv_s"].append(z_s.reshape(dec_batch, dec_seq, CONV_DIM)[:, dec_seq - (CONV_W - 1):])
        outs["sgv"].append(v_rows.reshape(dec_batch, dec_seq, SG_DIM))
        kp4 = k[:n_prompt].reshape(batch, seq, N_ATT, HPG, HEAD_DIM)
        vp4 = v[:n_prompt].reshape(batch, seq, N_ATT, HPG, HEAD_DIM)
        ks4 = k[n_prompt:].reshape(dec_batch, dec_seq, N_ATT, HPG, HEAD_DIM)
        vs4 = v[n_prompt:].reshape(dec_batch, dec_seq, N_ATT, HPG, HEAD_DIM)
        outs["kp"].append([kp4[:, seq - min(win, seq):, gi] for gi, (win, _) in enumerate(ATT_GROUPS)])
        outs["vp"].append([vp4[:, seq - min(win, seq):, gi] for gi, (win, _) in enumerate(ATT_GROUPS)])
        outs["ks"].append([ks4[:, dec_seq - min(win, dec_seq):, gi] for gi, (win, _) in enumerate(ATT_GROUPS)])
        outs["vs"].append([vs4[:, dec_seq - min(win, dec_seq):, gi] for gi, (win, _) in enumerate(ATT_GROUPS)])

    stack = lambda name: jnp.stack(outs[name], axis=0)
    per_group = lambda name, gi: jnp.stack([layer[gi] for layer in outs[name]], axis=0)
    result = [x[:n_prompt].reshape(batch, seq, D_MODEL), x[n_prompt:].reshape(dec_batch, dec_seq, D_MODEL),
              stack("conv_p"), stack("conv_s")]
    for gi in range(N_ATT):
        result += [per_group("kp", gi), per_group("vp", gi)]
    for gi in range(N_ATT):
        result += [per_group("ks", gi), per_group("vs", gi)]
    result.append(stack("sgv"))
    return tuple(result)
```

```python
import functools
import math

import jax
import jax.numpy as jnp
from jax import lax
from jax.experimental import pallas as pl
from jax.experimental.pallas import tpu as pltpu

D_MODEL = 1024
CONV_DIM = 384
CONV_W = 3
SG_DIM = 256
SG_GROUPS = 4
SG_HEAD = SG_DIM // SG_GROUPS
CHUNK = 128
ATT_GROUPS = ((128, 1), (512, 4), (2048, 16))
N_ATT = len(ATT_GROUPS)
HPG = 4
HEAD_DIM = 32
ATT_DIM = HPG * N_ATT * HEAD_DIM
ATT_OUT = HPG * HEAD_DIM
N_KEYS = 129
NUM_BUCKETS = 32
MAX_DISTANCE = 2048
N_EXPERTS = 32
TOP_K = 4
SWIGLU_LIMIT = 7.0
SWIGLU_ALPHA = 1.702
LN_EPS = 1e-5
OFF_A = 0
OFF_B = OFF_A + 3 * CONV_DIM
OFF_C = OFF_B + 2 * SG_DIM
OFF_G = OFF_C + 3 * ATT_DIM
IN_DIM = OFF_G + 3 * D_MODEL
NEG = -1e30

LANES = 128
SUBLANES = 8
LANE_CHUNKS = D_MODEL // LANES
VMEM_LIMIT = 56 * 1024 * 1024

TM = 256
TM_R = 512
TM_E = 512
Q_BLK = 128
SEQ_BLK = 4
DMA_UNROLL = 8
BIG_RUN = 64

F32 = jnp.float32
BF16 = jnp.bfloat16


def _dot(a, b):
    return jnp.dot(a.astype(BF16), b.astype(BF16), preferred_element_type=F32)


def _dot_nt(a, b):
    return lax.dot_general(a.astype(BF16), b.astype(BF16), (((1,), (1,)), ((), ())),
                           preferred_element_type=F32)


def _layer_norm(x, g, b):
    mu = jnp.mean(x, axis=-1, keepdims=True)
    xc = x - mu
    var = jnp.mean(xc * xc, axis=-1, keepdims=True)
    return xc * lax.rsqrt(var + LN_EPS) * g + b


def _gelu(x):
    return 0.5 * x * (1.0 + lax.erf(x * (2.0 ** -0.5)))


def _sigmoid(x):
    return 1.0 / (1.0 + jnp.exp(-x))


def _params(n_axes):
    return pltpu.CompilerParams(dimension_semantics=("arbitrary",) * n_axes,
                                vmem_limit_bytes=VMEM_LIMIT)


def _inproj_kernel(n_ptiles, tiles_per_seq, dec_seq,
                   xp_ref, xs_ref, w_in_ref, convw_ref, lng_ref, lnb_ref, wsp_ref, bsp_ref, wua_ref, wub_ref,
                   s1_ref, s2_ref,
                   qp_ref, kp_ref, vp_ref, qs_ref, ks_ref, vs_ref, mab_ref, gc_ref, tail_ref, zs_ref, vrows_ref,
                   carry_ref):
    i = pl.program_id(0)
    is_prompt = i < n_ptiles

    @pl.when(jnp.logical_and(is_prompt, i % tiles_per_seq == 0))
    def _():
        carry_ref[...] = jnp.zeros_like(carry_ref)

    x = jnp.where(is_prompt, xp_ref[...], xs_ref[...]).astype(BF16)

    pa = jnp.dot(x, w_in_ref[:, OFF_A:OFF_B], preferred_element_type=F32)
    bg, cg, h = pa[:, :CONV_DIM], pa[:, CONV_DIM:2 * CONV_DIM], pa[:, 2 * CONV_DIM:]
    z = cg * h
    row = lax.broadcasted_iota(jnp.int32, (TM, CONV_DIM), 0)
    pos = jnp.where(is_prompt, row, row & (dec_seq - 1))
    prev1 = jnp.where(is_prompt, jnp.broadcast_to(carry_ref[7:8, :], (TM, CONV_DIM)), s1_ref[...])
    prev2 = jnp.where(is_prompt,
                      jnp.where(row == 0, jnp.broadcast_to(carry_ref[6:7, :], (TM, CONV_DIM)),
                                jnp.broadcast_to(carry_ref[7:8, :], (TM, CONV_DIM))),
                      s2_ref[...])
    z1 = jnp.where(pos >= 1, pltpu.roll(z, 1, axis=0), prev1)
    z2 = jnp.where(pos >= 2, pltpu.roll(z, 2, axis=0), prev2)
    cw = convw_ref[...]
    ya = bg * (cw[0:1, :] * z2 + cw[1:2, :] * z1 + cw[2:3, :] * z)

    @pl.when(is_prompt)
    def _():
        carry_ref[...] = z[TM - SUBLANES:, :]
        tail_ref[0] = z[TM - SUBLANES:, :]

    @pl.when(jnp.logical_not(is_prompt))
    def _():
        zs_ref[...] = z

    pb = jnp.dot(x, w_in_ref[:, OFF_B:OFF_C], preferred_element_type=F32)
    uv = _gelu(pb)
    u = uv[:, :SG_DIM]
    vn = _layer_norm(uv[:, SG_DIM:], lng_ref[...], lnb_ref[...])

    @pl.when(jnp.logical_not(is_prompt))
    def _():
        vrows_ref[...] = vn

    vb = vn.astype(BF16)
    low_half = lax.broadcasted_iota(jnp.int32, (CHUNK, LANES), 1) < SG_HEAD
    s_rows = []
    for c in range(TM // CHUNK):
        s_cols = []
        for p in range(SG_DIM // LANES):
            vcp = vb[c * CHUNK:(c + 1) * CHUNK, p * LANES:(p + 1) * LANES]
            sa = jnp.dot(wsp_ref[2 * p], vcp, preferred_element_type=F32)
            sb = jnp.dot(wsp_ref[2 * p + 1], vcp, preferred_element_type=F32)
            s_cols.append(jnp.where(low_half, sa, sb))
        s_rows.append(jnp.concatenate(s_cols, axis=1) + bsp_ref[...])
    yb = u * jnp.concatenate(s_rows, axis=0)

    pc = jnp.dot(x, w_in_ref[:, OFF_C:OFF_G], preferred_element_type=F32)
    for group_refs, active in (((qp_ref, kp_ref, vp_ref), is_prompt),
                               ((qs_ref, ks_ref, vs_ref), jnp.logical_not(is_prompt))):
        @pl.when(active)
        def _(group_refs=group_refs):
            for j, ref in enumerate(group_refs):
                ref[...] = pc[:, j * ATT_DIM:(j + 1) * ATT_DIM]

    ga = _sigmoid(jnp.dot(x, w_in_ref[:, OFF_G:OFF_G + D_MODEL], preferred_element_type=F32))
    mab = ga * _dot(ya, wua_ref[...])
    gb = _sigmoid(jnp.dot(x, w_in_ref[:, OFF_G + D_MODEL:OFF_G + 2 * D_MODEL], preferred_element_type=F32))
    mab_ref[...] = mab + gb * _dot(yb, wub_ref[...])
    gc_ref[...] = _sigmoid(jnp.dot(x, w_in_ref[:, OFF_G + 2 * D_MODEL:], preferred_element_type=F32))


def _inproj_call(l, x_p, x_s, w_in, conv_w, sg_ln_g, sg_ln_b, wsp2, bsp2, w_up_a, w_up_b, s1, s2, seq, dec_seq):
    n_prompt, n_sample = x_p.shape[0], x_s.shape[0]
    n = n_prompt + n_sample
    n_ptiles = n_prompt // TM
    n_stiles = n_sample // TM
    tiles_per_seq = seq // TM
    batch = n_prompt // seq

    def stile(i):
        return jnp.maximum(i - n_ptiles, 0)

    def mode(i):
        return jnp.where(i < n_ptiles, 0, 1)

    row_spec = lambda w: pl.BlockSpec((TM, w), lambda i: (i, 0))
    srow_spec = lambda w: pl.BlockSpec((TM, w), lambda i: (stile(i), 0))
    prow_spec = lambda w: pl.BlockSpec((TM, w), lambda i: (jnp.minimum(i, n_ptiles - 1), 0))
    kern = functools.partial(_inproj_kernel, n_ptiles, tiles_per_seq, dec_seq)
    return pl.pallas_call(
        kern,
        grid=(n_ptiles + n_stiles,),
        in_specs=[
            prow_spec(D_MODEL), srow_spec(D_MODEL),
            pl.BlockSpec((None, D_MODEL, IN_DIM), lambda i: (l, 0, 0), pipeline_mode=pl.Buffered(1)),
            pl.BlockSpec((None, CONV_W, CONV_DIM), lambda i: (l, 0, 0)),
            pl.BlockSpec((None, 1, SG_DIM), lambda i: (l, 0, 0)),
            pl.BlockSpec((None, 1, SG_DIM), lambda i: (l, 0, 0)),
            pl.BlockSpec((None, None, SG_GROUPS, CHUNK, CHUNK), lambda i: (l, mode(i), 0, 0, 0)),
            pl.BlockSpec((None, None, CHUNK, SG_DIM), lambda i: (l, mode(i), 0, 0)),
            pl.BlockSpec((None, CONV_DIM, D_MODEL), lambda i: (l, 0, 0)),
            pl.BlockSpec((None, SG_DIM, D_MODEL), lambda i: (l, 0, 0)),
            srow_spec(CONV_DIM),
            srow_spec(CONV_DIM),
        ],
        out_specs=[
            prow_spec(ATT_DIM), prow_spec(ATT_DIM), prow_spec(ATT_DIM),
            srow_spec(ATT_DIM), srow_spec(ATT_DIM), srow_spec(ATT_DIM),
            row_spec(D_MODEL), row_spec(D_MODEL),
            pl.BlockSpec((1, SUBLANES, CONV_DIM),
                         lambda i: (jnp.minimum(i // tiles_per_seq, batch - 1), 0, 0)),
            srow_spec(CONV_DIM),
            srow_spec(SG_DIM),
        ],
        out_shape=[jax.ShapeDtypeStruct((n_prompt, ATT_DIM), F32)] * 3 + [jax.ShapeDtypeStruct((n_sample, ATT_DIM), F32)] * 3 + [
            jax.ShapeDtypeStruct((n, D_MODEL), F32), jax.ShapeDtypeStruct((n, D_MODEL), F32),
            jax.ShapeDtypeStruct((batch, SUBLANES, CONV_DIM), F32),
            jax.ShapeDtypeStruct((n_sample, CONV_DIM), F32),
            jax.ShapeDtypeStruct((n_sample, SG_DIM), F32),
        ],
        scratch_shapes=[pltpu.VMEM((SUBLANES, CONV_DIM), F32)],
        compiler_params=_params(1),
        name="inproj_mix",
    )(x_p, x_s, w_in, conv_w, sg_ln_g, sg_ln_b, wsp2, bsp2, w_up_a, w_up_b, s1, s2)


def _head_masks(rows):
    lane_head = lax.broadcasted_iota(jnp.int32, (rows, LANES), 1) // HEAD_DIM
    return [lane_head == h for h in range(HPG)]


def _stack_heads(qb, masks):
    return jnp.concatenate([jnp.where(m, qb, 0.0) for m in masks], axis=0)


def _unstack_heads(stacked, masks, r):
    out = None
    for h, m in enumerate(masks):
        part = jnp.where(m, stacked[h * r:(h + 1) * r], 0.0)
        out = part if out is None else out + part
    return out


def _mix_groups(outs, lses):
    mx = functools.reduce(jnp.maximum, lses)
    ws = [jnp.exp(ls - mx) for ls in lses]
    num = functools.reduce(lambda a, b: a + b, [w * o for w, o in zip(ws, outs)])
    return num / functools.reduce(lambda a, b: a + b, ws)


def _bias_table(bias_ref, base, steps):
    def body(j, tab):
        return jnp.where(steps == j, bias_ref[base + j], tab)

    return lax.fori_loop(0, N_KEYS, body, jnp.full(steps.shape, NEG, F32))


def _attn_prompt_kernel(seq, bias_ref, q_ref, k_ref, v_ref, o_ref, tab_scr, *scr):
    b = pl.program_id(0)
    g = pl.program_id(1)
    o_scr, l_scr = scr[:N_ATT], scr[N_ATT:]
    masks = _head_masks(Q_BLK)
    scale = HEAD_DIM ** -0.5
    col = lax.broadcasted_iota(jnp.int32, (HPG * Q_BLK, 2 * Q_BLK), 1)

    def build_table(gi):
        qi = lax.broadcasted_iota(jnp.int32, (Q_BLK, 2 * Q_BLK), 0)
        kc = lax.broadcasted_iota(jnp.int32, (Q_BLK, 2 * Q_BLK), 1)
        for h in range(HPG):
            tab_scr[gi, h * Q_BLK:(h + 1) * Q_BLK, :] = _bias_table(bias_ref, (gi * HPG + h) * N_KEYS,
                                                                    qi + Q_BLK - kc)

    def run_group(gi, dil):
        rows_per_class = seq // dil
        n_blk = rows_per_class // Q_BLK

        def ld(ref, start):
            if dil > 1:
                return ref[pl.ds(start, Q_BLK, stride=dil), :]
            return ref[pl.ds(start, Q_BLK), :]

        def st(ref, start, val):
            if dil > 1:
                ref[pl.ds(start, Q_BLK, stride=dil), :] = val
            else:
                ref[pl.ds(start, Q_BLK), :] = val

        def block(it, carry):
            r = it // n_blk
            mb = it % n_blk
            cur = r + dil * Q_BLK * mb
            prev = r + dil * Q_BLK * jnp.maximum(mb - 1, 0)
            qb = ld(q_ref, cur) * scale
            kw = jnp.concatenate([ld(k_ref, prev), ld(k_ref, cur)], axis=0)
            vw = jnp.concatenate([ld(v_ref, prev), ld(v_ref, cur)], axis=0)
            s = _dot_nt(_stack_heads(qb, masks), kw) + tab_scr[gi]
            s = jnp.where(jnp.logical_or(col >= Q_BLK, mb > 0), s, NEG)
            m = jnp.max(s, axis=1, keepdims=True)
            p = jnp.exp(s - m)
            den = jnp.sum(p, axis=1, keepdims=True)
            pv = _dot(p, vw)
            st(o_scr[gi], cur, _unstack_heads(pv / den, masks, Q_BLK))
            st(l_scr[gi], cur, _unstack_heads(m + jnp.log(den), masks, Q_BLK))
            return carry

        lax.fori_loop(0, dil * n_blk, block, 0, unroll=4)

    for gi, (_, dil) in enumerate(ATT_GROUPS):
        @pl.when(jnp.logical_and(g == gi, b == 0))
        def _(gi=gi):
            build_table(gi)

        @pl.when(g == gi)
        def _(gi=gi, dil=dil):
            run_group(gi, dil)

    @pl.when(g == N_ATT - 1)
    def _():
        o_ref[...] = _mix_groups([s[...] for s in o_scr], [s[...] for s in l_scr])


def _attn_prompt_call(q, k, v, bias_flat, batch, seq):
    blk = lambda: pl.BlockSpec((seq, LANES), lambda b, g: (b, g))
    return pl.pallas_call(
        functools.partial(_attn_prompt_kernel, seq),
        grid=(batch, N_ATT),
        in_specs=[pl.BlockSpec(memory_space=pltpu.SMEM), blk(), blk(), blk()],
        out_specs=pl.BlockSpec((seq, ATT_OUT), lambda b, g: (b, 0)),
        out_shape=jax.ShapeDtypeStruct((batch * seq, ATT_OUT), F32),
        scratch_shapes=[pltpu.VMEM((N_ATT, HPG * Q_BLK, 2 * Q_BLK), F32)]
                       + [pltpu.VMEM((seq, LANES), F32)] * (2 * N_ATT),
        compiler_params=_params(2),
        name="attn_prompt",
    )(bias_flat, q, k, v)


def _attn_sample_kernel(dec_seq, pasts, bias_ref, q_ref, k_ref, v_ref, kc0, vc0, kc1, vc1, kc2, vc2,
                        o_ref, tc0, tc1, tc2, tn_scr):
    kcs, vcs, tcs = (kc0, kc1, kc2), (vc0, vc1, vc2), (tc0, tc1, tc2)
    masks = _head_masks(dec_seq)
    scale = HEAD_DIM ** -0.5

    @pl.when(pl.program_id(0) == 0)
    def _():
        for gi, (_, dil) in enumerate(ATT_GROUPS):
            past = pasts[gi]
            shift = dil.bit_length() - 1

            def steps(dist, dil=dil, shift=shift):
                return jnp.where((dist & (dil - 1)) == 0, dist >> shift, -1)

            qi_c = lax.broadcasted_iota(jnp.int32, (dec_seq, past), 0)
            row_c = lax.broadcasted_iota(jnp.int32, (dec_seq, past), 1)
            qi_n = lax.broadcasted_iota(jnp.int32, (dec_seq, dec_seq), 0)
            row_n = lax.broadcasted_iota(jnp.int32, (dec_seq, dec_seq), 1)
            for h in range(HPG):
                base = (gi * HPG + h) * N_KEYS
                tcs[gi][h * dec_seq:(h + 1) * dec_seq, :] = _bias_table(bias_ref, base, steps(past + qi_c - row_c))
                tn_scr[gi, h * dec_seq:(h + 1) * dec_seq, :] = _bias_table(bias_ref, base, steps(qi_n - row_n))

    for s_i in range(SEQ_BLK):
        rows = slice(s_i * dec_seq, (s_i + 1) * dec_seq)
        outs, lses = [], []
        for gi in range(N_ATT):
            cols = slice(gi * LANES, (gi + 1) * LANES)
            qs = _stack_heads(q_ref[rows, cols] * scale, masks)
            k_new, v_new = k_ref[rows, cols], v_ref[rows, cols]
            s_c = _dot(qs, kcs[gi][s_i]) + tcs[gi][...]
            s_n = _dot_nt(qs, k_new) + tn_scr[gi]
            m = jnp.maximum(jnp.max(s_c, axis=1, keepdims=True), jnp.max(s_n, axis=1, keepdims=True))
            p_c = jnp.exp(s_c - m)
            p_n = jnp.exp(s_n - m)
            den = jnp.sum(p_c, axis=1, keepdims=True) + jnp.sum(p_n, axis=1, keepdims=True)
            pv = _dot_nt(p_c, vcs[gi][s_i]) + _dot(p_n, v_new)
            outs.append(_unstack_heads(pv / den, masks, dec_seq))
            lses.append(_unstack_heads(m + jnp.log(den), masks, dec_seq))
        o_ref[rows, :] = _mix_groups(outs, lses)


def _attn_sample_call(l, q, k, v, caches_t, bias_flat, dec_batch, dec_seq):
    rows = SEQ_BLK * dec_seq
    pasts = tuple(caches_t[2 * gi].shape[3] for gi in range(N_ATT))
    new_spec = pl.BlockSpec((rows, ATT_DIM), lambda i: (i, 0))
    cache_specs = [pl.BlockSpec((None, SEQ_BLK, LANES, c.shape[3]), lambda i: (l, i, 0, 0)) for c in caches_t]
    return pl.pallas_call(
        functools.partial(_attn_sample_kernel, dec_seq, pasts),
        grid=(dec_batch // SEQ_BLK,),
        in_specs=[pl.BlockSpec(memory_space=pltpu.SMEM), new_spec, new_spec, new_spec] + cache_specs,
        out_specs=pl.BlockSpec((rows, ATT_OUT), lambda i: (i, 0)),
        out_shape=jax.ShapeDtypeStruct((dec_batch * dec_seq, ATT_OUT), F32),
        scratch_shapes=[pltpu.VMEM((HPG * dec_seq, p), F32) for p in pasts]
                       + [pltpu.VMEM((N_ATT, HPG * dec_seq, dec_seq), F32)],
        compiler_params=_params(1),
        name="attn_sample",
    )(bias_flat, q, k, v, *caches_t)


def _merge_kernel(alpha, n_ptiles, xp_ref, xs_ref, mab_ref, gc_ref, ycp_ref, ycs_ref, wuc_ref, wo_ref, g_ref, b_ref,
                  wrt_ref, br_ref, x1_ref, x1g_ref, w_ref, slot_ref, tcnt_ref, tbase_ref, cnt_scr):
    @pl.when(pl.program_id(0) == 0)
    def _():
        cnt_scr[...] = jnp.zeros_like(cnt_scr)

    is_prompt = pl.program_id(0) < n_ptiles
    x = jnp.where(is_prompt, xp_ref[...], xs_ref[...])
    yc = jnp.where(is_prompt, ycp_ref[...], ycs_ref[...])
    merged = mab_ref[...] + gc_ref[...] * _dot(yc, wuc_ref[...])
    x1 = _layer_norm(alpha * x + _dot(merged, wo_ref[...]), g_ref[...], b_ref[...])
    x1_ref[...] = x1
    for c in range(LANE_CHUNKS):
        x1g_ref[pl.ds(c, TM_R, stride=LANE_CHUNKS), :] = x1[:, c * LANES:(c + 1) * LANES]

    logits = lax.dot_general(wrt_ref[...], x1, (((1,), (1,)), ((), ())),
                             precision=lax.Precision.HIGHEST, preferred_element_type=F32) + br_ref[...]
    eio = lax.broadcasted_iota(jnp.int32, (N_EXPERTS, TM_R), 0)
    vals, idxs = [], []
    for _ in range(TOP_K):
        mv = jnp.max(logits, axis=0, keepdims=True)
        ix = jnp.min(jnp.where(logits == mv, eio, N_EXPERTS), axis=0, keepdims=True)
        vals.append(mv)
        idxs.append(ix)
        logits = jnp.where(eio == ix, -jnp.inf, logits)
    tv = jnp.concatenate(vals, axis=0)
    e = jnp.exp(tv - tv[0:1, :])
    w_ref[...] = e / jnp.sum(e, axis=0, keepdims=True)

    onehots = [eio == ix for ix in idxs]
    chosen = functools.reduce(lambda a, c: a + c, [jnp.where(oh, 1.0, 0.0) for oh in onehots])
    earlier = (lax.broadcasted_iota(jnp.int32, (TM_R, TM_R), 0) < lax.broadcasted_iota(jnp.int32, (TM_R, TM_R), 1))
    within = _dot(chosen, jnp.where(earlier, 1.0, 0.0))
    tile_cnt = jnp.broadcast_to(jnp.sum(chosen, axis=1, keepdims=True), (N_EXPERTS, LANES))
    lower = (lax.broadcasted_iota(jnp.int32, (N_EXPERTS, N_EXPERTS), 1)
             < lax.broadcasted_iota(jnp.int32, (N_EXPERTS, N_EXPERTS), 0))
    place = within + jnp.dot(jnp.where(lower, 1.0, 0.0), tile_cnt, precision=lax.Precision.HIGHEST,
                             preferred_element_type=F32)[:, 0:1]
    slot_ref[...] = jnp.concatenate(
        [jnp.sum(jnp.where(oh, place, 0.0), axis=0, keepdims=True) for oh in onehots],
        axis=0).astype(jnp.int32) * LANE_CHUNKS
    tcnt_ref[...] = tile_cnt.astype(jnp.int32)
    tbase_ref[...] = cnt_scr[...].astype(jnp.int32)
    cnt_scr[...] = cnt_scr[...] + tile_cnt


def _merge_call(l, alpha, x_p, x_s, mab, gc, yc_p, yc_s, w_up_c, w_o, ln_g, ln_b, w_router_t, b_router):
    n_ptiles = x_p.shape[0] // TM_R
    n = x_p.shape[0] + x_s.shape[0]
    row_spec = lambda w: pl.BlockSpec((TM_R, w), lambda i: (i, 0))
    prow_spec = lambda w: pl.BlockSpec((TM_R, w), lambda i: (jnp.minimum(i, n_ptiles - 1), 0))
    srow_spec = lambda w: pl.BlockSpec((TM_R, w), lambda i: (jnp.maximum(i - n_ptiles, 0), 0))
    vec_spec = pl.BlockSpec((None, 1, D_MODEL), lambda i: (l, 0, 0))
    return pl.pallas_call(
        functools.partial(_merge_kernel, alpha, n_ptiles),
        grid=(n // TM_R,),
        in_specs=[
            prow_spec(D_MODEL), srow_spec(D_MODEL), row_spec(D_MODEL), row_spec(D_MODEL),
            prow_spec(ATT_OUT), srow_spec(ATT_OUT),
            pl.BlockSpec((None, ATT_OUT, D_MODEL), lambda i: (l, 0, 0)),
            pl.BlockSpec((None, D_MODEL, D_MODEL), lambda i: (l, 0, 0)),
            vec_spec, vec_spec,
            pl.BlockSpec((None, N_EXPERTS, D_MODEL), lambda i: (l, 0, 0)),
            pl.BlockSpec((None, N_EXPERTS, 1), lambda i: (l, 0, 0)),
        ],
        out_specs=[
            row_spec(D_MODEL),
            pl.BlockSpec((TM_R * LANE_CHUNKS, LANES), lambda i: (i, 0)),
            pl.BlockSpec((TOP_K, TM_R), lambda i: (0, i)),
            pl.BlockSpec((TOP_K, TM_R), lambda i: (0, i)),
            pl.BlockSpec((None, N_EXPERTS, LANES), lambda i: (i, 0, 0)),
            pl.BlockSpec((None, N_EXPERTS, LANES), lambda i: (i, 0, 0)),
        ],
        out_shape=[
            jax.ShapeDtypeStruct((n, D_MODEL), F32),
            jax.ShapeDtypeStruct((n * LANE_CHUNKS, LANES), F32),
            jax.ShapeDtypeStruct((TOP_K, n), F32),
            jax.ShapeDtypeStruct((TOP_K, n), jnp.int32),
            jax.ShapeDtypeStruct((n // TM_R, N_EXPERTS, LANES), jnp.int32),
            jax.ShapeDtypeStruct((n // TM_R, N_EXPERTS, LANES), jnp.int32),
        ],
        scratch_shapes=[pltpu.VMEM((N_EXPERTS, LANES), F32)],
        compiler_params=_params(1),
        name="merge_ln1_router",
    )(x_p, x_s, mab, gc, yc_p, yc_s, w_up_c, w_o, ln_g, ln_b, w_router_t, b_router)


def _destride(ref, rows):
    return jnp.concatenate([ref[pl.ds(c, rows, stride=LANE_CHUNKS), :] for c in range(LANE_CHUNKS)], axis=1)


def _slab_rows(ref, row, count=1):
    return ref.at[pl.ds(pl.multiple_of(row, LANE_CHUNKS), count * LANE_CHUNKS), :]


def _start_expert_blocks(tile, tcnt_ref, hstart_ref, make_copy):
    big_bits = BIG_RUN.bit_length() - 1

    def per_expert(e, staged):
        cnt = tcnt_ref[tile * N_EXPERTS + e]
        first = hstart_ref[tile * N_EXPERTS + e]
        n_big = cnt >> big_bits

        def big(j, carry):
            make_copy(staged + j * BIG_RUN, first + j * BIG_RUN, BIG_RUN).start()
            return carry

        lax.fori_loop(0, n_big, big, 0)
        for bit in reversed(range(big_bits)):
            size = 1 << bit
            done = (cnt >> (bit + 1)) << (bit + 1)

            @pl.when((cnt & size) != 0)
            def _(size=size, done=done):
                make_copy(staged + done, first + done, size).start()
        return staged + cnt

    lax.fori_loop(0, N_EXPERTS, per_expert, 0)


def _dispatch_kernel(n_tiles, lt_ref, has_ref, nv_ref, tcnt_ref, hstart_ref, slot_ref, x1g_ref, xs_hbm,
                     zbuf, stg, zsem, sem):
    tile_rows = TM_E * LANE_CHUNKS
    i = pl.program_id(0)
    n_steps = pl.num_programs(0)
    buf = i % 2
    staged_all = lambda b: pltpu.make_async_copy(stg.at[b], xs_hbm.at[pl.ds(0, TM_R * TOP_K * LANE_CHUNKS), :],
                                                 sem.at[b])

    @pl.when(i == 0)
    def _():
        zbuf[...] = jnp.zeros_like(zbuf)
        zero_copy = lambda t: pltpu.make_async_copy(zbuf, xs_hbm.at[pl.ds(t * tile_rows, tile_rows), :], zsem)
        for e in range(N_EXPERTS):
            @pl.when(has_ref[e] > 0)
            def _(e=e):
                zero_copy(lt_ref[e]).start()
        lax.fori_loop(nv_ref[0], n_tiles, lambda t, c: (zero_copy(t).start(), c)[1], 0)
        for e in range(N_EXPERTS):
            @pl.when(has_ref[e] > 0)
            def _(e=e):
                zero_copy(lt_ref[e]).wait()
        lax.fori_loop(nv_ref[0], n_tiles, lambda t, c: (zero_copy(t).wait(), c)[1], 0)

    def step(b):
        @pl.when(i >= 2)
        def _():
            staged_all(b).wait()

        def place(t, carry):
            slab = _slab_rows(x1g_ref, t * LANE_CHUNKS)[...]
            for kk in range(TOP_K):
                _slab_rows(stg.at[b], slot_ref[0, 0, t * TOP_K + kk])[...] = slab
            return carry

        lax.fori_loop(0, TM_R, place, 0, unroll=DMA_UNROLL)
        _start_expert_blocks(i, tcnt_ref, hstart_ref, lambda staged, row, size: pltpu.make_async_copy(
            _slab_rows(stg.at[b], staged * LANE_CHUNKS, size), _slab_rows(xs_hbm, row * LANE_CHUNKS, size),
            sem.at[b]))

        @pl.when(i == n_steps - 1)
        def _():
            staged_all(b).wait()

        @pl.when(jnp.logical_and(i == n_steps - 1, i >= 1))
        def _():
            staged_all(1 - b).wait()

    for b in range(2):
        @pl.when(buf == b)
        def _(b=b):
            step(b)


def _dispatch_call(last_tile, has_rows, n_valid, tcnt, hstart, slot, x1g, n_tiles):
    n = slot.shape[0] * TM_R
    grid_spec = pltpu.PrefetchScalarGridSpec(
        num_scalar_prefetch=5,
        grid=(n // TM_R,),
        in_specs=[
            pl.BlockSpec((1, 1, TM_R * TOP_K), lambda i, *_: (i, 0, 0), memory_space=pltpu.SMEM),
            pl.BlockSpec((TM_R * LANE_CHUNKS, LANES), lambda i, *_: (i, 0)),
        ],
        out_specs=pl.BlockSpec(memory_space=pl.ANY),
        scratch_shapes=[pltpu.VMEM((TM_E * LANE_CHUNKS, LANES), F32),
                        pltpu.VMEM((2, TM_R * TOP_K * LANE_CHUNKS, LANES), F32),
                        pltpu.SemaphoreType.DMA(()), pltpu.SemaphoreType.DMA((2,))],
    )
    return pl.pallas_call(
        functools.partial(_dispatch_kernel, n_tiles),
        grid_spec=grid_spec,
        out_shape=jax.ShapeDtypeStruct((n_tiles * TM_E * LANE_CHUNKS, LANES), F32),
        compiler_params=_params(1),
        name="moe_dispatch",
    )(last_tile, has_rows, n_valid, tcnt, hstart, slot, x1g)


def _moe_ffn_kernel(te_ref, nv_ref, xs_ref, wg_ref, bg_ref, wu_ref, bu_ref, wd_ref, bd_ref, ys_ref, w_b16):
    t = pl.program_id(0)

    @pl.when(t < nv_ref[0])
    def _():
        @pl.when(jnp.logical_or(t == 0, te_ref[t] != te_ref[jnp.maximum(t - 1, 0)]))
        def _():
            for j, w_ref in enumerate((wg_ref, wu_ref, wd_ref)):
                w_b16[j] = w_ref[...].astype(BF16)

        x = _destride(xs_ref, TM_E).astype(BF16)
        gl = jnp.minimum(jnp.dot(x, w_b16[0], preferred_element_type=F32) + bg_ref[...], SWIGLU_LIMIT)
        ul = jnp.clip(jnp.dot(x, w_b16[1], preferred_element_type=F32) + bu_ref[...], -SWIGLU_LIMIT, SWIGLU_LIMIT)
        hid = gl * _sigmoid(SWIGLU_ALPHA * gl) * (ul + 1.0)
        y = jnp.dot(hid.astype(BF16), w_b16[2], preferred_element_type=F32) + bd_ref[...]
        for c in range(LANE_CHUNKS):
            ys_ref[pl.ds(c, TM_E, stride=LANE_CHUNKS), :] = y[:, c * LANES:(c + 1) * LANES]

    @pl.when(t >= nv_ref[0])
    def _():
        ys_ref[...] = jnp.zeros_like(ys_ref)


def _moe_ffn_call(l, tile_expert, n_valid, xs, w_gate, b_gate, w_up, b_up, w_down, b_down):
    n_tiles = tile_expert.shape[0]
    mat_spec = pl.BlockSpec((None, None, D_MODEL, D_MODEL), lambda t, te, nv: (l, te[t], 0, 0))
    vec_spec = pl.BlockSpec((None, None, 1, D_MODEL), lambda t, te, nv: (l, te[t], 0, 0))
    in_slab = pl.BlockSpec((TM_E * LANE_CHUNKS, LANES), lambda t, te, nv: (jnp.minimum(t, nv[0] - 1), 0))
    grid_spec = pltpu.PrefetchScalarGridSpec(
        num_scalar_prefetch=2,
        grid=(n_tiles,),
        in_specs=[in_slab, mat_spec, vec_spec, mat_spec, vec_spec, mat_spec, vec_spec],
        out_specs=pl.BlockSpec((TM_E * LANE_CHUNKS, LANES), lambda t, te, nv: (t, 0)),
        scratch_shapes=[pltpu.VMEM((3, D_MODEL, D_MODEL), BF16)],
    )
    return pl.pallas_call(
        _moe_ffn_kernel,
        grid_spec=grid_spec,
        out_shape=jax.ShapeDtypeStruct((n_tiles * TM_E * LANE_CHUNKS, LANES), F32),
        compiler_params=_params(1),
        name="moe_ffn",
    )(tile_expert, n_valid, xs, w_gate, b_gate, w_up, b_up, w_down, b_down)


def _combine_kernel(alpha, n_ptiles, tcnt_ref, hstart_ref, slot_ref, w_ref, x1_ref, ys_hbm, g_ref, b_ref,
                    x2p_ref, x2s_ref, stg, mixed, sem):
    i = pl.program_id(0)
    n_steps = pl.num_programs(0)
    buf = i % 2

    def fetch(tile, b):
        _start_expert_blocks(tile, tcnt_ref, hstart_ref, lambda staged, row, size: pltpu.make_async_copy(
            _slab_rows(ys_hbm, row * LANE_CHUNKS, size), _slab_rows(stg.at[b], staged * LANE_CHUNKS, size),
            sem.at[b]))

    def step(b):
        @pl.when(i == 0)
        def _():
            fetch(0, b)

        @pl.when(i + 1 < n_steps)
        def _():
            fetch(i + 1, 1 - b)

        pltpu.make_async_copy(ys_hbm.at[pl.ds(0, TM_R * TOP_K * LANE_CHUNKS), :], stg.at[b], sem.at[b]).wait()

        def mix(t, carry):
            c0 = t * TOP_K
            acc = w_ref[0, 0, c0] * _slab_rows(stg.at[b], slot_ref[0, 0, c0])[...]
            for kk in range(1, TOP_K):
                acc = acc + w_ref[0, 0, c0 + kk] * _slab_rows(stg.at[b], slot_ref[0, 0, c0 + kk])[...]
            _slab_rows(mixed, t * LANE_CHUNKS)[...] = acc
            return carry

        lax.fori_loop(0, TM_R, mix, 0, unroll=DMA_UNROLL)

    for b in range(2):
        @pl.when(buf == b)
        def _(b=b):
            step(b)

    x2 = _layer_norm(alpha * x1_ref[...] + _destride(mixed, TM_R), g_ref[...], b_ref[...])

    @pl.when(i < n_ptiles)
    def _():
        x2p_ref[...] = x2

    @pl.when(i >= n_ptiles)
    def _():
        x2s_ref[...] = x2


def _combine_call(l, alpha, tcnt, hstart, slot, top_w, x1, ys, ln_g, ln_b, n_prompt):
    n = x1.shape[0]
    n_ptiles = n_prompt // TM_R
    vec_spec = pl.BlockSpec((None, 1, D_MODEL), lambda i, *_: (l, 0, 0))
    choice_spec = pl.BlockSpec((1, 1, TM_R * TOP_K), lambda i, *_: (i, 0, 0), memory_space=pltpu.SMEM)
    grid_spec = pltpu.PrefetchScalarGridSpec(
        num_scalar_prefetch=2,
        grid=(n // TM_R,),
        in_specs=[
            choice_spec, choice_spec,
            pl.BlockSpec((TM_R, D_MODEL), lambda i, *_: (i, 0)),
            pl.BlockSpec(memory_space=pl.ANY),
            vec_spec, vec_spec,
        ],
        out_specs=[pl.BlockSpec((TM_R, D_MODEL), lambda i, *_: (jnp.minimum(i, n_ptiles - 1), 0)),
                   pl.BlockSpec((TM_R, D_MODEL), lambda i, *_: (jnp.maximum(i - n_ptiles, 0), 0))],
        scratch_shapes=[pltpu.VMEM((2, TM_R * TOP_K * LANE_CHUNKS, LANES), F32),
                        pltpu.VMEM((TM_R * LANE_CHUNKS, LANES), F32),
                        pltpu.SemaphoreType.DMA((2,))],
    )
    return pl.pallas_call(
        functools.partial(_combine_kernel, alpha, n_ptiles),
        grid_spec=grid_spec,
        out_shape=[jax.ShapeDtypeStruct((n_prompt, D_MODEL), F32),
                   jax.ShapeDtypeStruct((n - n_prompt, D_MODEL), F32)],
        compiler_params=_params(1),
        name="moe_combine_ln2",
    )(tcnt, hstart, slot, top_w, x1, ys, ln_g, ln_b)


def _route(tcnt, tbase, n_tiles):
    experts = jnp.arange(N_EXPERTS, dtype=jnp.int32)
    counts = tbase[-1] + tcnt[-1]
    padded = ((counts + TM_E - 1) // TM_E) * TM_E
    ends = jnp.cumsum(padded)
    hstart = (ends - padded)[None, :] + tbase
    n_valid = ends[-1] // TM_E
    last_expert = jnp.max(jnp.where(counts > 0, experts, 0))
    tile_start = jnp.arange(n_tiles, dtype=jnp.int32) * TM_E
    tile_expert = jnp.minimum(jnp.sum(ends[None, :] <= tile_start[:, None], axis=1), last_expert)
    return (tile_expert.astype(jnp.int32), n_valid.astype(jnp.int32).reshape(1),
            tcnt.reshape(-1), hstart.astype(jnp.int32).reshape(-1),
            (ends // TM_E - 1).astype(jnp.int32), (counts > 0).astype(jnp.int32))


def _t5_bucket(dist):
    max_exact = NUM_BUCKETS // 2
    distf = jnp.maximum(dist, 1).astype(F32)
    large = max_exact + (jnp.log(distf / max_exact) / math.log(MAX_DISTANCE / max_exact)
                         * (NUM_BUCKETS - max_exact)).astype(jnp.int32)
    large = jnp.minimum(large, NUM_BUCKETS - 1)
    return jnp.where(dist < max_exact, dist, large)


def _group_bias(rel_bias, gi, dil):
    j = jnp.arange(N_KEYS, dtype=jnp.int32)
    return rel_bias[_t5_bucket(dil * j)][:, gi * HPG:(gi + 1) * HPG].T.astype(F32)


def _step_bias(rel_bias):
    return jnp.stack([_group_bias(rel_bias, gi, dil) for gi, (_, dil) in enumerate(ATT_GROUPS)], axis=0).reshape(-1)


def _spatial_tables(w_sp, b_sp, dec_seq):
    depth = w_sp.shape[0]
    tril = jnp.tril(jnp.ones((CHUNK, CHUNK), bool))
    wp = jnp.where(tril, w_sp, 0.0)
    ws_small = jnp.where(tril[:dec_seq, :dec_seq], w_sp[:, :, :dec_seq, :dec_seq], 0.0)
    eye = jnp.eye(CHUNK // dec_seq, dtype=w_sp.dtype)
    ws = jnp.einsum("ab,lgij->lgaibj", eye, ws_small).reshape(depth, SG_GROUPS, CHUNK, CHUNK)
    bp = jnp.repeat(jnp.swapaxes(b_sp, 1, 2), SG_HEAD, axis=2)
    bs = jnp.tile(bp[:, :dec_seq], (1, CHUNK // dec_seq, 1))
    return jnp.stack([wp, ws], axis=1).astype(BF16), jnp.stack([bp, bs], axis=1)


def kernel(x_prompt, x_sample, state_conv, cache_k_w128, cache_v_w128, cache_k_w512, cache_v_w512,
           cache_k_w2048, cache_v_w2048, w_in, conv_w, sg_ln_g, sg_ln_b, w_sp, b_sp, rel_bias,
           w_up_a, w_up_b, w_up_c, w_o, ln1_g, ln1_b, w_router, b_router, w_gate, b_gate, w_up, b_up,
           w_down, b_down, ln2_g, ln2_b):
    batch, seq, _ = x_prompt.shape
    dec_batch, dec_seq, _ = x_sample.shape
    depth = w_in.shape[0]
    n_prompt, n_sample = batch * seq, dec_batch * dec_seq
    n = n_prompt + n_sample
    alpha = (2 * depth) ** 0.25
    assert seq % TM == 0 and n_sample % TM == 0 and dec_seq & (dec_seq - 1) == 0 and CHUNK % dec_seq == 0
    assert n_prompt % TM_R == 0 and n_sample % TM_R == 0
    assert all(seq % (dil * Q_BLK) == 0 and dil & (dil - 1) == 0 for _, dil in ATT_GROUPS)
    assert dec_batch % SEQ_BLK == 0
    n_tiles = (n * TOP_K) // TM_E + N_EXPERTS

    caches_t = [jnp.transpose(c, (0, 1, 3, 4, 2)).reshape(c.shape[0], c.shape[1], LANES, c.shape[2]) for c in
                (cache_k_w128, cache_v_w128, cache_k_w512, cache_v_w512, cache_k_w2048, cache_v_w2048)]
    bias_flat = _step_bias(rel_bias)
    wsp2, bsp2 = _spatial_tables(w_sp, b_sp, dec_seq)
    w_in_b, w_up_a_b, w_up_b_b, w_up_c_b, w_o_b = (w.astype(BF16) for w in (w_in, w_up_a, w_up_b, w_up_c, w_o))
    vec = lambda a: a.reshape(depth, 1, a.shape[-1])
    w_router_t = jnp.swapaxes(w_router, 1, 2)
    b_router_c = b_router.reshape(depth, N_EXPERTS, 1)
    expert_vec = lambda a: a.reshape(depth, N_EXPERTS, 1, a.shape[-1])

    x_p, x_s = x_prompt.reshape(n_prompt, D_MODEL), x_sample.reshape(n_sample, D_MODEL)
    outs ={name: [] for name in ("conv_p", "conv_s", "sgv", "kp", "vp", "ks", "vs")}
    for l in range(depth):
        s1 = jnp.zeros((dec_batch, dec_seq, CONV_DIM), F32).at[:, 0].set(state_conv[l, :, 1])
        s2 = jnp.zeros((dec_batch, dec_seq, CONV_DIM), F32).at[:, 0].set(state_conv[l, :, 0])
        s2 = s2.at[:, 1].set(state_conv[l, :, 1])
        q_p, k_p, v_p, q_s, k_s, v_s, mab, gc, tail, z_s, v_rows = _inproj_call(
            l, x_p, x_s, w_in_b, conv_w, vec(sg_ln_g), vec(sg_ln_b), wsp2, bsp2, w_up_a_b, w_up_b_b,
            s1.reshape(n_sample, CONV_DIM), s2.reshape(n_sample, CONV_DIM), seq, dec_seq)
        yc_p = _attn_prompt_call(q_p, k_p, v_p, bias_flat, batch, seq)
        yc_s = _attn_sample_call(l, q_s, k_s, v_s, caches_t, bias_flat, dec_batch, dec_seq)
        x1, x1g, top_w, slot, tcnt, tbase = _merge_call(
            l, alpha, x_p, x_s, mab, gc, yc_p, yc_s, w_up_c_b, w_o_b, vec(ln1_g), vec(ln1_b), w_router_t, b_router_c)
        tile_expert, n_valid, tcnt, hstart, last_tile, has_rows = _route(tcnt[:, :, 0], tbase[:, :, 0], n_tiles)
        by_tile = lambda a: a.reshape(TOP_K, n // TM_R, TM_R).transpose(1, 2, 0).reshape(n // TM_R, 1, TM_R * TOP_K)
        slot, top_w = by_tile(slot), by_tile(top_w)
        xs = _dispatch_call(last_tile, has_rows, n_valid, tcnt, hstart, slot, x1g, n_tiles)
        ys = _moe_ffn_call(l, tile_expert, n_valid, xs, w_gate, expert_vec(b_gate),
                           w_up, expert_vec(b_up), w_down, expert_vec(b_down))
        x_p, x_s = _combine_call(l, alpha, tcnt, hstart, slot, top_w, x1, ys, vec(ln2_g), vec(ln2_b), n_prompt)

        outs["conv_p"].append(tail[:, SUBLANES - (CONV_W - 1):])
        outs["conv_s"].append(z_s.reshape(dec_batch, dec_seq, CONV_DIM)[:, dec_seq - (CONV_W - 1):])
        outs["sgv"].append(v_rows.reshape(dec_batch, dec_seq, SG_DIM))
        kp4 = k_p.reshape(batch, seq, N_ATT, HPG, HEAD_DIM)
        vp4 = v_p.reshape(batch, seq, N_ATT, HPG, HEAD_DIM)
        ks4 = k_s.reshape(dec_batch, dec_seq, N_ATT, HPG, HEAD_DIM)
        vs4 = v_s.reshape(dec_batch, dec_seq, N_ATT, HPG, HEAD_DIM)
        outs["kp"].append([kp4[:, seq - min(win, seq):, gi] for gi, (win, _) in enumerate(ATT_GROUPS)])
        outs["vp"].append([vp4[:, seq - min(win, seq):, gi] for gi, (win, _) in enumerate(ATT_GROUPS)])
        outs["ks"].append([ks4[:, dec_seq - min(win, dec_seq):, gi] for gi, (win, _) in enumerate(ATT_GROUPS)])
        outs["vs"].append([vs4[:, dec_seq - min(win, dec_seq):, gi] for gi, (win, _) in enumerate(ATT_GROUPS)])

    stack = lambda name: jnp.stack(outs[name], axis=0)
    per_group = lambda name, gi: jnp.stack([layer[gi] for layer in outs[name]], axis=0)
    result = [x_p.reshape(batch, seq, D_MODEL), x_s.reshape(dec_batch, dec_seq, D_MODEL),
              stack("conv_p"), stack("conv_s")]
    for gi in range(N_ATT):
        result += [per_group("kp", gi), per_group("vp", gi)]
    for gi in range(N_ATT):
        result += [per_group("ks", gi), per_group("vs", gi)]
    result.append(stack("sgv"))
    return tuple(result)
```

```python
import functools
import math

import jax
import jax.numpy as jnp
from jax import lax
from jax.experimental import pallas as pl
from jax.experimental.pallas import tpu as pltpu

D_MODEL = 1024
CONV_DIM = 384
CONV_W = 3
SG_DIM = 256
SG_GROUPS = 4
SG_HEAD = SG_DIM // SG_GROUPS
CHUNK = 128
ATT_GROUPS = ((128, 1), (512, 4), (2048, 16))
N_ATT = len(ATT_GROUPS)
HPG = 4
HEAD_DIM = 32
ATT_DIM = HPG * N_ATT * HEAD_DIM
ATT_OUT = HPG * HEAD_DIM
N_KEYS = 129
NUM_BUCKETS = 32
MAX_DISTANCE = 2048
N_EXPERTS = 32
TOP_K = 4
SWIGLU_LIMIT = 7.0
SWIGLU_ALPHA = 1.702
LN_EPS = 1e-5
OFF_A = 0
OFF_B = OFF_A + 3 * CONV_DIM
OFF_C = OFF_B + 2 * SG_DIM
OFF_G = OFF_C + 3 * ATT_DIM
IN_DIM = OFF_G + 3 * D_MODEL
NEG = -1e30

LANES = 128
SUBLANES = 8
LANE_CHUNKS = D_MODEL // LANES
VMEM_LIMIT = 56 * 1024 * 1024

TM = 256
TM_R = 512
TM_E = 384
Q_BLK = 128
SEQ_BLK = 4
DMA_UNROLL = 8
BIG_RUN = 64

F32 = jnp.float32
BF16 = jnp.bfloat16


def _dot(a, b):
    return jnp.dot(a.astype(BF16), b.astype(BF16), preferred_element_type=F32)


def _dot_nt(a, b):
    return lax.dot_general(a.astype(BF16), b.astype(BF16), (((1,), (1,)), ((), ())),
                           preferred_element_type=F32)


def _layer_norm(x, g, b):
    mu = jnp.mean(x, axis=-1, keepdims=True)
    xc = x - mu
    var = jnp.mean(xc * xc, axis=-1, keepdims=True)
    return xc * lax.rsqrt(var + LN_EPS) * g + b


def _gelu(x):
    return 0.5 * x * (1.0 + lax.erf(x * (2.0 ** -0.5)))


def _sigmoid(x):
    return 1.0 / (1.0 + jnp.exp(-x))


def _params(n_axes):
    return pltpu.CompilerParams(dimension_semantics=("arbitrary",) * n_axes,
                                vmem_limit_bytes=VMEM_LIMIT)


def _inproj_kernel(n_ptiles, tiles_per_seq, dec_seq,
                   xp_ref, xs_ref, w_in_ref, convw_ref, lng_ref, lnb_ref, wsp_ref, bsp_ref, wua_ref, wub_ref,
                   s1_ref, s2_ref,
                   qp_ref, kp_ref, vp_ref, qs_ref, ks_ref, vs_ref, mab_ref, gc_ref, tail_ref, zs_ref, vrows_ref,
                   carry_ref):
    i = pl.program_id(0)
    is_prompt = i < n_ptiles

    @pl.when(jnp.logical_and(is_prompt, i % tiles_per_seq == 0))
    def _():
        carry_ref[...] = jnp.zeros_like(carry_ref)

    x = jnp.where(is_prompt, xp_ref[...], xs_ref[...]).astype(BF16)

    pa = jnp.dot(x, w_in_ref[:, OFF_A:OFF_B], preferred_element_type=F32)
    bg, cg, h = pa[:, :CONV_DIM], pa[:, CONV_DIM:2 * CONV_DIM], pa[:, 2 * CONV_DIM:]
    z = cg * h
    row = lax.broadcasted_iota(jnp.int32, (TM, CONV_DIM), 0)
    pos = jnp.where(is_prompt, row, row & (dec_seq - 1))
    prev1 = jnp.where(is_prompt, jnp.broadcast_to(carry_ref[7:8, :], (TM, CONV_DIM)), s1_ref[...])
    prev2 = jnp.where(is_prompt,
                      jnp.where(row == 0, jnp.broadcast_to(carry_ref[6:7, :], (TM, CONV_DIM)),
                                jnp.broadcast_to(carry_ref[7:8, :], (TM, CONV_DIM))),
                      s2_ref[...])
    z1 = jnp.where(pos >= 1, pltpu.roll(z, 1, axis=0), prev1)
    z2 = jnp.where(pos >= 2, pltpu.roll(z, 2, axis=0), prev2)
    cw = convw_ref[...]
    ya = bg * (cw[0:1, :] * z2 + cw[1:2, :] * z1 + cw[2:3, :] * z)

    @pl.when(is_prompt)
    def _():
        carry_ref[...] = z[TM - SUBLANES:, :]
        tail_ref[0] = z[TM - SUBLANES:, :]

    @pl.when(jnp.logical_not(is_prompt))
    def _():
        zs_ref[...] = z

    pb = jnp.dot(x, w_in_ref[:, OFF_B:OFF_C], preferred_element_type=F32)
    uv = _gelu(pb)
    u = uv[:, :SG_DIM]
    vn = _layer_norm(uv[:, SG_DIM:], lng_ref[...], lnb_ref[...])

    @pl.when(jnp.logical_not(is_prompt))
    def _():
        vrows_ref[...] = vn

    vb = vn.astype(BF16)
    low_half = lax.broadcasted_iota(jnp.int32, (CHUNK, LANES), 1) < SG_HEAD
    s_rows = []
    for c in range(TM // CHUNK):
        s_cols = []
        for p in range(SG_DIM // LANES):
            vcp = vb[c * CHUNK:(c + 1) * CHUNK, p * LANES:(p + 1) * LANES]
            sa = jnp.dot(wsp_ref[2 * p], vcp, preferred_element_type=F32)
            sb = jnp.dot(wsp_ref[2 * p + 1], vcp, preferred_element_type=F32)
            s_cols.append(jnp.where(low_half, sa, sb))
        s_rows.append(jnp.concatenate(s_cols, axis=1) + bsp_ref[...])
    yb = u * jnp.concatenate(s_rows, axis=0)

    pc = jnp.dot(x, w_in_ref[:, OFF_C:OFF_G], preferred_element_type=F32)
    for group_refs, active in (((qp_ref, kp_ref, vp_ref), is_prompt),
                               ((qs_ref, ks_ref, vs_ref), jnp.logical_not(is_prompt))):
        @pl.when(active)
        def _(group_refs=group_refs):
            for j, ref in enumerate(group_refs):
                ref[...] = pc[:, j * ATT_DIM:(j + 1) * ATT_DIM]

    ga = _sigmoid(jnp.dot(x, w_in_ref[:, OFF_G:OFF_G + D_MODEL], preferred_element_type=F32))
    mab = ga * _dot(ya, wua_ref[...])
    gb = _sigmoid(jnp.dot(x, w_in_ref[:, OFF_G + D_MODEL:OFF_G + 2 * D_MODEL], preferred_element_type=F32))
    mab_ref[...] = (mab + gb * _dot(yb, wub_ref[...])).astype(BF16)
    gc_ref[...] = _sigmoid(jnp.dot(x, w_in_ref[:, OFF_G + 2 * D_MODEL:], preferred_element_type=F32)).astype(BF16)


def _inproj_call(l, x_p, x_s, w_in, conv_w, sg_ln_g, sg_ln_b, wsp2, bsp2, w_up_a, w_up_b, s1, s2, seq, dec_seq):
    n_prompt, n_sample = x_p.shape[0], x_s.shape[0]
    n = n_prompt + n_sample
    n_ptiles = n_prompt // TM
    n_stiles = n_sample // TM
    tiles_per_seq = seq // TM
    batch = n_prompt // seq

    def stile(i):
        return jnp.maximum(i - n_ptiles, 0)

    def mode(i):
        return jnp.where(i < n_ptiles, 0, 1)

    row_spec = lambda w: pl.BlockSpec((TM, w), lambda i: (i, 0))
    srow_spec = lambda w: pl.BlockSpec((TM, w), lambda i: (stile(i), 0))
    prow_spec = lambda w: pl.BlockSpec((TM, w), lambda i: (jnp.minimum(i, n_ptiles - 1), 0))
    kern = functools.partial(_inproj_kernel, n_ptiles, tiles_per_seq, dec_seq)
    return pl.pallas_call(
        kern,
        grid=(n_ptiles + n_stiles,),
        in_specs=[
            prow_spec(D_MODEL), srow_spec(D_MODEL),
            pl.BlockSpec((None, D_MODEL, IN_DIM), lambda i: (l, 0, 0), pipeline_mode=pl.Buffered(1)),
            pl.BlockSpec((None, CONV_W, CONV_DIM), lambda i: (l, 0, 0)),
            pl.BlockSpec((None, 1, SG_DIM), lambda i: (l, 0, 0)),
            pl.BlockSpec((None, 1, SG_DIM), lambda i: (l, 0, 0)),
            pl.BlockSpec((None, None, SG_GROUPS, CHUNK, CHUNK), lambda i: (l, mode(i), 0, 0, 0)),
            pl.BlockSpec((None, None, CHUNK, SG_DIM), lambda i: (l, mode(i), 0, 0)),
            pl.BlockSpec((None, CONV_DIM, D_MODEL), lambda i: (l, 0, 0)),
            pl.BlockSpec((None, SG_DIM, D_MODEL), lambda i: (l, 0, 0)),
            srow_spec(CONV_DIM),
            srow_spec(CONV_DIM),
        ],
        out_specs=[
            prow_spec(ATT_DIM), prow_spec(ATT_DIM), prow_spec(ATT_DIM),
            srow_spec(ATT_DIM), srow_spec(ATT_DIM), srow_spec(ATT_DIM),
            row_spec(D_MODEL), row_spec(D_MODEL),
            pl.BlockSpec((1, SUBLANES, CONV_DIM),
                         lambda i: (jnp.minimum(i // tiles_per_seq, batch - 1), 0, 0)),
            srow_spec(CONV_DIM),
            srow_spec(SG_DIM),
        ],
        out_shape=[jax.ShapeDtypeStruct((n_prompt, ATT_DIM), F32)] * 3 + [jax.ShapeDtypeStruct((n_sample, ATT_DIM), F32)] * 3 + [
            jax.ShapeDtypeStruct((n, D_MODEL), BF16), jax.ShapeDtypeStruct((n, D_MODEL), BF16),
            jax.ShapeDtypeStruct((batch, SUBLANES, CONV_DIM), F32),
            jax.ShapeDtypeStruct((n_sample, CONV_DIM), F32),
            jax.ShapeDtypeStruct((n_sample, SG_DIM), F32),
        ],
        scratch_shapes=[pltpu.VMEM((SUBLANES, CONV_DIM), F32)],
        compiler_params=_params(1),
        name="inproj_mix",
    )(x_p, x_s, w_in, conv_w, sg_ln_g, sg_ln_b, wsp2, bsp2, w_up_a, w_up_b, s1, s2)


def _head_masks(rows):
    lane_head = lax.broadcasted_iota(jnp.int32, (rows, LANES), 1) // HEAD_DIM
    return [lane_head == h for h in range(HPG)]


def _stack_heads(qb, masks):
    return jnp.concatenate([jnp.where(m, qb, 0.0) for m in masks], axis=0)


def _unstack_heads(stacked, masks, r):
    out = None
    for h, m in enumerate(masks):
        part = jnp.where(m, stacked[h * r:(h + 1) * r], 0.0)
        out = part if out is None else out + part
    return out


def _mix_groups(outs, lses):
    mx = functools.reduce(jnp.maximum, lses)
    ws = [jnp.exp(ls - mx) for ls in lses]
    num = functools.reduce(lambda a, b: a + b, [w * o for w, o in zip(ws, outs)])
    return num / functools.reduce(lambda a, b: a + b, ws)


def _bias_table(bias_ref, base, steps):
    def body(j, tab):
        return jnp.where(steps == j, bias_ref[base + j], tab)

    return lax.fori_loop(0, N_KEYS, body, jnp.full(steps.shape, NEG, F32))


def _attn_prompt_kernel(seq, bias_ref, q_ref, k_ref, v_ref, o_ref, tab_scr, *scr):
    b = pl.program_id(0)
    g = pl.program_id(1)
    o_scr, l_scr = scr[:N_ATT], scr[N_ATT:]
    masks = _head_masks(Q_BLK)
    scale = HEAD_DIM ** -0.5
    col = lax.broadcasted_iota(jnp.int32, (HPG * Q_BLK, 2 * Q_BLK), 1)

    def build_table(gi):
        qi = lax.broadcasted_iota(jnp.int32, (Q_BLK, 2 * Q_BLK), 0)
        kc = lax.broadcasted_iota(jnp.int32, (Q_BLK, 2 * Q_BLK), 1)
        for h in range(HPG):
            tab_scr[gi, h * Q_BLK:(h + 1) * Q_BLK, :] = _bias_table(bias_ref, (gi * HPG + h) * N_KEYS,
                                                                    qi + Q_BLK - kc)

    def run_group(gi, dil):
        rows_per_class = seq // dil
        n_blk = rows_per_class // Q_BLK

        def ld(ref, start):
            if dil > 1:
                return ref[pl.ds(start, Q_BLK, stride=dil), :]
            return ref[pl.ds(start, Q_BLK), :]

        def st(ref, start, val):
            if dil > 1:
                ref[pl.ds(start, Q_BLK, stride=dil), :] = val
            else:
                ref[pl.ds(start, Q_BLK), :] = val

        def block(it, carry):
            r = it // n_blk
            mb = it % n_blk
            cur = r + dil * Q_BLK * mb
            prev = r + dil * Q_BLK * jnp.maximum(mb - 1, 0)
            qb = ld(q_ref, cur) * scale
            kw = jnp.concatenate([ld(k_ref, prev), ld(k_ref, cur)], axis=0)
            vw = jnp.concatenate([ld(v_ref, prev), ld(v_ref, cur)], axis=0)
            s = _dot_nt(_stack_heads(qb, masks), kw) + tab_scr[gi]
            s = jnp.where(jnp.logical_or(col >= Q_BLK, mb > 0), s, NEG)
            m = jnp.max(s, axis=1, keepdims=True)
            p = jnp.exp(s - m)
            den = jnp.sum(p, axis=1, keepdims=True)
            pv = _dot(p, vw)
            st(o_scr[gi], cur, _unstack_heads(pv / den, masks, Q_BLK))
            st(l_scr[gi], cur, _unstack_heads(m + jnp.log(den), masks, Q_BLK))
            return carry

        lax.fori_loop(0, dil * n_blk, block, 0, unroll=4)

    for gi, (_, dil) in enumerate(ATT_GROUPS):
        @pl.when(jnp.logical_and(g == gi, b == 0))
        def _(gi=gi):
            build_table(gi)

        @pl.when(g == gi)
        def _(gi=gi, dil=dil):
            run_group(gi, dil)

    @pl.when(g == N_ATT - 1)
    def _():
        o_ref[...] = _mix_groups([s[...] for s in o_scr], [s[...] for s in l_scr])


def _attn_prompt_call(q, k, v, bias_flat, batch, seq):
    blk = lambda: pl.BlockSpec((seq, LANES), lambda b, g: (b, g))
    return pl.pallas_call(
        functools.partial(_attn_prompt_kernel, seq),
        grid=(batch, N_ATT),
        in_specs=[pl.BlockSpec(memory_space=pltpu.SMEM), blk(), blk(), blk()],
        out_specs=pl.BlockSpec((seq, ATT_OUT), lambda b, g: (b, 0)),
        out_shape=jax.ShapeDtypeStruct((batch * seq, ATT_OUT), F32),
        scratch_shapes=[pltpu.VMEM((N_ATT, HPG * Q_BLK, 2 * Q_BLK), F32)]
                       + [pltpu.VMEM((seq, LANES), F32)] * (2 * N_ATT),
        compiler_params=_params(2),
        name="attn_prompt",
    )(bias_flat, q, k, v)


def _attn_sample_kernel(dec_seq, pasts, bias_ref, q_ref, k_ref, v_ref, kc0, vc0, kc1, vc1, kc2, vc2,
                        o_ref, tc0, tc1, tc2, tn_scr):
    kcs, vcs, tcs = (kc0, kc1, kc2), (vc0, vc1, vc2), (tc0, tc1, tc2)
    masks = _head_masks(dec_seq)
    scale = HEAD_DIM ** -0.5

    @pl.when(pl.program_id(0) == 0)
    def _():
        for gi, (_, dil) in enumerate(ATT_GROUPS):
            past = pasts[gi]
            shift = dil.bit_length() - 1

            def steps(dist, dil=dil, shift=shift):
                return jnp.where((dist & (dil - 1)) == 0, dist >> shift, -1)

            qi_c = lax.broadcasted_iota(jnp.int32, (dec_seq, past), 0)
            row_c = lax.broadcasted_iota(jnp.int32, (dec_seq, past), 1)
            qi_n = lax.broadcasted_iota(jnp.int32, (dec_seq, dec_seq), 0)
            row_n = lax.broadcasted_iota(jnp.int32, (dec_seq, dec_seq), 1)
            for h in range(HPG):
                base = (gi * HPG + h) * N_KEYS
                tcs[gi][h * dec_seq:(h + 1) * dec_seq, :] = _bias_table(bias_ref, base, steps(past + qi_c - row_c))
                tn_scr[gi, h * dec_seq:(h + 1) * dec_seq, :] = _bias_table(bias_ref, base, steps(qi_n - row_n))

    for s_i in range(SEQ_BLK):
        rows = slice(s_i * dec_seq, (s_i + 1) * dec_seq)
        outs, lses = [], []
        for gi in range(N_ATT):
            cols = slice(gi * LANES, (gi + 1) * LANES)
            qs = _stack_heads(q_ref[rows, cols] * scale, masks)
            k_new, v_new = k_ref[rows, cols], v_ref[rows, cols]
            s_c = _dot(qs, kcs[gi][s_i]) + tcs[gi][...]
            s_n = _dot_nt(qs, k_new) + tn_scr[gi]
            m = jnp.maximum(jnp.max(s_c, axis=1, keepdims=True), jnp.max(s_n, axis=1, keepdims=True))
            p_c = jnp.exp(s_c - m)
            p_n = jnp.exp(s_n - m)
            den = jnp.sum(p_c, axis=1, keepdims=True) + jnp.sum(p_n, axis=1, keepdims=True)
            pv = _dot_nt(p_c, vcs[gi][s_i]) + _dot(p_n, v_new)
            outs.append(_unstack_heads(pv / den, masks, dec_seq))
            lses.append(_unstack_heads(m + jnp.log(den), masks, dec_seq))
        o_ref[rows, :] = _mix_groups(outs, lses)


def _attn_sample_call(l, q, k, v, caches_t, bias_flat, dec_batch, dec_seq):
    rows = SEQ_BLK * dec_seq
    pasts = tuple(caches_t[2 * gi].shape[3] for gi in range(N_ATT))
    new_spec = pl.BlockSpec((rows, ATT_DIM), lambda i: (i, 0))
    cache_specs = [pl.BlockSpec((None, SEQ_BLK, LANES, c.shape[3]), lambda i: (l, i, 0, 0)) for c in caches_t]
    return pl.pallas_call(
        functools.partial(_attn_sample_kernel, dec_seq, pasts),
        grid=(dec_batch // SEQ_BLK,),
        in_specs=[pl.BlockSpec(memory_space=pltpu.SMEM), new_spec, new_spec, new_spec] + cache_specs,
        out_specs=pl.BlockSpec((rows, ATT_OUT), lambda i: (i, 0)),
        out_shape=jax.ShapeDtypeStruct((dec_batch * dec_seq, ATT_OUT), F32),
        scratch_shapes=[pltpu.VMEM((HPG * dec_seq, p), F32) for p in pasts]
                       + [pltpu.VMEM((N_ATT, HPG * dec_seq, dec_seq), F32)],
        compiler_params=_params(1),
        name="attn_sample",
    )(bias_flat, q, k, v, *caches_t)


def _merge_kernel(alpha, n_ptiles, xp_ref, xs_ref, mab_ref, gc_ref, ycp_ref, ycs_ref, wuc_ref, wo_ref, g_ref, b_ref,
                  wrt_ref, br_ref, x1g_ref, w_ref, slot_ref, tcnt_ref, tbase_ref, cnt_scr):
    @pl.when(pl.program_id(0) == 0)
    def _():
        cnt_scr[...] = jnp.zeros_like(cnt_scr)

    is_prompt = pl.program_id(0) < n_ptiles
    x = jnp.where(is_prompt, xp_ref[...], xs_ref[...])
    yc = jnp.where(is_prompt, ycp_ref[...], ycs_ref[...])
    merged = mab_ref[...].astype(F32) + gc_ref[...].astype(F32) * _dot(yc, wuc_ref[...])
    x1 = _layer_norm(alpha * x + _dot(merged, wo_ref[...]), g_ref[...], b_ref[...])
    for c in range(LANE_CHUNKS):
        x1g_ref[pl.ds(c, TM_R, stride=LANE_CHUNKS), :] = x1[:, c * LANES:(c + 1) * LANES]

    logits = lax.dot_general(wrt_ref[...], x1, (((1,), (1,)), ((), ())),
                             precision=lax.Precision.HIGHEST, preferred_element_type=F32) + br_ref[...]
    eio = lax.broadcasted_iota(jnp.int32, (N_EXPERTS, TM_R), 0)
    vals, idxs = [], []
    for _ in range(TOP_K):
        mv = jnp.max(logits, axis=0, keepdims=True)
        ix = jnp.min(jnp.where(logits == mv, eio, N_EXPERTS), axis=0, keepdims=True)
        vals.append(mv)
        idxs.append(ix)
        logits = jnp.where(eio == ix, -jnp.inf, logits)
    tv = jnp.concatenate(vals, axis=0)
    e = jnp.exp(tv - tv[0:1, :])
    w_ref[...] = e / jnp.sum(e, axis=0, keepdims=True)

    onehots = [eio == ix for ix in idxs]
    chosen = functools.reduce(lambda a, c: a + c, [jnp.where(oh, 1.0, 0.0) for oh in onehots])
    earlier = (lax.broadcasted_iota(jnp.int32, (TM_R, TM_R), 0) < lax.broadcasted_iota(jnp.int32, (TM_R, TM_R), 1))
    within = _dot(chosen, jnp.where(earlier, 1.0, 0.0))
    tile_cnt = jnp.broadcast_to(jnp.sum(chosen, axis=1, keepdims=True), (N_EXPERTS, LANES))
    lower = (lax.broadcasted_iota(jnp.int32, (N_EXPERTS, N_EXPERTS), 1)
             < lax.broadcasted_iota(jnp.int32, (N_EXPERTS, N_EXPERTS), 0))
    place = within + jnp.dot(jnp.where(lower, 1.0, 0.0), tile_cnt, precision=lax.Precision.HIGHEST,
                             preferred_element_type=F32)[:, 0:1]
    slot_ref[...] = jnp.concatenate(
        [jnp.sum(jnp.where(oh, place, 0.0), axis=0, keepdims=True) for oh in onehots],
        axis=0).astype(jnp.int32) * LANE_CHUNKS
    tcnt_ref[...] = tile_cnt.astype(jnp.int32)
    tbase_ref[...] = cnt_scr[...].astype(jnp.int32)
    cnt_scr[...] = cnt_scr[...] + tile_cnt


def _merge_call(l, alpha, x_p, x_s, mab, gc, yc_p, yc_s, w_up_c, w_o, ln_g, ln_b, w_router_t, b_router):
    n_ptiles = x_p.shape[0] // TM_R
    n = x_p.shape[0] + x_s.shape[0]
    row_spec = lambda w: pl.BlockSpec((TM_R, w), lambda i: (i, 0))
    prow_spec = lambda w: pl.BlockSpec((TM_R, w), lambda i: (jnp.minimum(i, n_ptiles - 1), 0))
    srow_spec = lambda w: pl.BlockSpec((TM_R, w), lambda i: (jnp.maximum(i - n_ptiles, 0), 0))
    vec_spec = pl.BlockSpec((None, 1, D_MODEL), lambda i: (l, 0, 0))
    return pl.pallas_call(
        functools.partial(_merge_kernel, alpha, n_ptiles),
        grid=(n // TM_R,),
        in_specs=[
            prow_spec(D_MODEL), srow_spec(D_MODEL), row_spec(D_MODEL), row_spec(D_MODEL),
            prow_spec(ATT_OUT), srow_spec(ATT_OUT),
            pl.BlockSpec((None, ATT_OUT, D_MODEL), lambda i: (l, 0, 0)),
            pl.BlockSpec((None, D_MODEL, D_MODEL), lambda i: (l, 0, 0)),
            vec_spec, vec_spec,
            pl.BlockSpec((None, N_EXPERTS, D_MODEL), lambda i: (l, 0, 0)),
            pl.BlockSpec((None, N_EXPERTS, 1), lambda i: (l, 0, 0)),
        ],
        out_specs=[
            pl.BlockSpec((TM_R * LANE_CHUNKS, LANES), lambda i: (i, 0)),
            pl.BlockSpec((TOP_K, TM_R), lambda i: (0, i)),
            pl.BlockSpec((TOP_K, TM_R), lambda i: (0, i)),
            pl.BlockSpec((None, N_EXPERTS, LANES), lambda i: (i, 0, 0)),
            pl.BlockSpec((None, N_EXPERTS, LANES), lambda i: (i, 0, 0)),
        ],
        out_shape=[
            jax.ShapeDtypeStruct((n * LANE_CHUNKS, LANES), F32),
            jax.ShapeDtypeStruct((TOP_K, n), F32),
            jax.ShapeDtypeStruct((TOP_K, n), jnp.int32),
            jax.ShapeDtypeStruct((n // TM_R, N_EXPERTS, LANES), jnp.int32),
            jax.ShapeDtypeStruct((n // TM_R, N_EXPERTS, LANES), jnp.int32),
        ],
        scratch_shapes=[pltpu.VMEM((N_EXPERTS, LANES), F32)],
        compiler_params=_params(1),
        name="merge_ln1_router",
    )(x_p, x_s, mab, gc, yc_p, yc_s, w_up_c, w_o, ln_g, ln_b, w_router_t, b_router)


def _destride(ref, rows):
    return jnp.concatenate([ref[pl.ds(c, rows, stride=LANE_CHUNKS), :] for c in range(LANE_CHUNKS)], axis=1)


def _slab_rows(ref, row, count=1):
    return ref.at[pl.ds(pl.multiple_of(row, LANE_CHUNKS), count * LANE_CHUNKS), :]


def _start_expert_blocks(tile, tcnt_ref, hstart_ref, make_copy):
    big_bits = BIG_RUN.bit_length() - 1

    def per_expert(e, staged):
        cnt = tcnt_ref[tile * N_EXPERTS + e]
        first = hstart_ref[tile * N_EXPERTS + e]
        n_big = cnt >> big_bits

        def big(j, carry):
            make_copy(staged + j * BIG_RUN, first + j * BIG_RUN, BIG_RUN).start()
            return carry

        lax.fori_loop(0, n_big, big, 0)
        for bit in reversed(range(big_bits)):
            size = 1 << bit
            done = (cnt >> (bit + 1)) << (bit + 1)

            @pl.when((cnt & size) != 0)
            def _(size=size, done=done):
                make_copy(staged + done, first + done, size).start()
        return staged + cnt

    lax.fori_loop(0, N_EXPERTS, per_expert, 0)


def _dispatch_kernel(n_tiles, lt_ref, has_ref, nv_ref, tcnt_ref, hstart_ref, slot_ref, x1g_ref, xs_hbm,
                     zbuf, stg, zsem, sem):
    tile_rows = TM_E * LANE_CHUNKS
    i = pl.program_id(0)
    n_steps = pl.num_programs(0)
    buf = i % 2
    staged_all = lambda b: pltpu.make_async_copy(stg.at[b], xs_hbm.at[pl.ds(0, TM_R * TOP_K * LANE_CHUNKS), :],
                                                 sem.at[b])

    @pl.when(i == 0)
    def _():
        zbuf[...] = jnp.zeros_like(zbuf)
        zero_copy = lambda t: pltpu.make_async_copy(zbuf, xs_hbm.at[pl.ds(t * tile_rows, tile_rows), :], zsem)
        for e in range(N_EXPERTS):
            @pl.when(has_ref[e] > 0)
            def _(e=e):
                zero_copy(lt_ref[e]).start()
        lax.fori_loop(nv_ref[0], n_tiles, lambda t, c: (zero_copy(t).start(), c)[1], 0)
        for e in range(N_EXPERTS):
            @pl.when(has_ref[e] > 0)
            def _(e=e):
                zero_copy(lt_ref[e]).wait()
        lax.fori_loop(nv_ref[0], n_tiles, lambda t, c: (zero_copy(t).wait(), c)[1], 0)

    def step(b):
        @pl.when(i >= 2)
        def _():
            staged_all(b).wait()

        def place(t, carry):
            slab = _slab_rows(x1g_ref, t * LANE_CHUNKS)[...]
            for kk in range(TOP_K):
                _slab_rows(stg.at[b], slot_ref[0, 0, t * TOP_K + kk])[...] = slab
            return carry

        lax.fori_loop(0, TM_R, place, 0, unroll=DMA_UNROLL)
        _start_expert_blocks(i, tcnt_ref, hstart_ref, lambda staged, row, size: pltpu.make_async_copy(
            _slab_rows(stg.at[b], staged * LANE_CHUNKS, size), _slab_rows(xs_hbm, row * LANE_CHUNKS, size),
            sem.at[b]))

        @pl.when(i == n_steps - 1)
        def _():
            staged_all(b).wait()

        @pl.when(jnp.logical_and(i == n_steps - 1, i >= 1))
        def _():
            staged_all(1 - b).wait()

    for b in range(2):
        @pl.when(buf == b)
        def _(b=b):
            step(b)


def _dispatch_call(last_tile, has_rows, n_valid, tcnt, hstart, slot, x1g, n_tiles):
    n = slot.shape[0] * TM_R
    grid_spec = pltpu.PrefetchScalarGridSpec(
        num_scalar_prefetch=5,
        grid=(n // TM_R,),
        in_specs=[
            pl.BlockSpec((1, 1, TM_R * TOP_K), lambda i, *_: (i, 0, 0), memory_space=pltpu.SMEM),
            pl.BlockSpec((TM_R * LANE_CHUNKS, LANES), lambda i, *_: (i, 0)),
        ],
        out_specs=pl.BlockSpec(memory_space=pl.ANY),
        scratch_shapes=[pltpu.VMEM((TM_E * LANE_CHUNKS, LANES), F32),
                        pltpu.VMEM((2, TM_R * TOP_K * LANE_CHUNKS, LANES), F32),
                        pltpu.SemaphoreType.DMA(()), pltpu.SemaphoreType.DMA((2,))],
    )
    return pl.pallas_call(
        functools.partial(_dispatch_kernel, n_tiles),
        grid_spec=grid_spec,
        out_shape=jax.ShapeDtypeStruct((n_tiles * TM_E * LANE_CHUNKS, LANES), F32),
        compiler_params=_params(1),
        name="moe_dispatch",
    )(last_tile, has_rows, n_valid, tcnt, hstart, slot, x1g)


def _moe_ffn_kernel(te_ref, nv_ref, xs_ref, wg_ref, bg_ref, wu_ref, bu_ref, wd_ref, bd_ref, ys_ref, w_b16):
    t = pl.program_id(0)

    @pl.when(t < nv_ref[0])
    def _():
        @pl.when(jnp.logical_or(t == 0, te_ref[t] != te_ref[jnp.maximum(t - 1, 0)]))
        def _():
            for j, w_ref in enumerate((wg_ref, wu_ref, wd_ref)):
                w_b16[j] = w_ref[...].astype(BF16)

        x = _destride(xs_ref, TM_E).astype(BF16)
        gl = jnp.minimum(jnp.dot(x, w_b16[0], preferred_element_type=F32) + bg_ref[...], SWIGLU_LIMIT)
        ul = jnp.clip(jnp.dot(x, w_b16[1], preferred_element_type=F32) + bu_ref[...], -SWIGLU_LIMIT, SWIGLU_LIMIT)
        hid = gl * _sigmoid(SWIGLU_ALPHA * gl) * (ul + 1.0)
        y = jnp.dot(hid.astype(BF16), w_b16[2], preferred_element_type=F32) + bd_ref[...]
        for c in range(LANE_CHUNKS):
            ys_ref[pl.ds(c, TM_E, stride=LANE_CHUNKS), :] = y[:, c * LANES:(c + 1) * LANES]

    @pl.when(t >= nv_ref[0])
    def _():
        ys_ref[...] = jnp.zeros_like(ys_ref)


def _moe_ffn_call(l, tile_expert, n_valid, xs, w_gate, b_gate, w_up, b_up, w_down, b_down):
    n_tiles = tile_expert.shape[0]
    mat_spec = pl.BlockSpec((None, None, D_MODEL, D_MODEL), lambda t, te, nv: (l, te[t], 0, 0))
    vec_spec = pl.BlockSpec((None, None, 1, D_MODEL), lambda t, te, nv: (l, te[t], 0, 0))
    in_slab = pl.BlockSpec((TM_E * LANE_CHUNKS, LANES), lambda t, te, nv: (jnp.minimum(t, nv[0] - 1), 0))
    grid_spec = pltpu.PrefetchScalarGridSpec(
        num_scalar_prefetch=2,
        grid=(n_tiles,),
        in_specs=[in_slab, mat_spec, vec_spec, mat_spec, vec_spec, mat_spec, vec_spec],
        out_specs=pl.BlockSpec((TM_E * LANE_CHUNKS, LANES), lambda t, te, nv: (t, 0)),
        scratch_shapes=[pltpu.VMEM((3, D_MODEL, D_MODEL), BF16)],
    )
    return pl.pallas_call(
        _moe_ffn_kernel,
        grid_spec=grid_spec,
        out_shape=jax.ShapeDtypeStruct((n_tiles * TM_E * LANE_CHUNKS, LANES), F32),
        compiler_params=_params(1),
        name="moe_ffn",
    )(tile_expert, n_valid, xs, w_gate, b_gate, w_up, b_up, w_down, b_down)


def _combine_kernel(alpha, n_ptiles, tcnt_ref, hstart_ref, slot_ref, w_ref, x1g_ref, ys_hbm, g_ref, b_ref,
                    x2p_ref, x2s_ref, stg, mixed, sem):
    i = pl.program_id(0)
    n_steps = pl.num_programs(0)
    buf = i % 2

    def fetch(tile, b):
        _start_expert_blocks(tile, tcnt_ref, hstart_ref, lambda staged, row, size: pltpu.make_async_copy(
            _slab_rows(ys_hbm, row * LANE_CHUNKS, size), _slab_rows(stg.at[b], staged * LANE_CHUNKS, size),
            sem.at[b]))

    def step(b):
        @pl.when(i == 0)
        def _():
            fetch(0, b)

        @pl.when(i + 1 < n_steps)
        def _():
            fetch(i + 1, 1 - b)

        pltpu.make_async_copy(ys_hbm.at[pl.ds(0, TM_R * TOP_K * LANE_CHUNKS), :], stg.at[b], sem.at[b]).wait()

        def mix(t, carry):
            c0 = t * TOP_K
            acc = w_ref[0, 0, c0] * _slab_rows(stg.at[b], slot_ref[0, 0, c0])[...]
            for kk in range(1, TOP_K):
                acc = acc + w_ref[0, 0, c0 + kk] * _slab_rows(stg.at[b], slot_ref[0, 0, c0 + kk])[...]
            _slab_rows(mixed, t * LANE_CHUNKS)[...] = acc
            return carry

        lax.fori_loop(0, TM_R, mix, 0, unroll=DMA_UNROLL)

    for b in range(2):
        @pl.when(buf == b)
        def _(b=b):
            step(b)

    x2 = _layer_norm(alpha * _destride(x1g_ref, TM_R) + _destride(mixed, TM_R), g_ref[...], b_ref[...])

    @pl.when(i < n_ptiles)
    def _():
        x2p_ref[...] = x2

    @pl.when(i >= n_ptiles)
    def _():
        x2s_ref[...] = x2


def _combine_call(l, alpha, tcnt, hstart, slot, top_w, x1g, ys, ln_g, ln_b, n_prompt):
    n = x1g.shape[0] // LANE_CHUNKS
    n_ptiles = n_prompt // TM_R
    vec_spec = pl.BlockSpec((None, 1, D_MODEL), lambda i, *_: (l, 0, 0))
    choice_spec = pl.BlockSpec((1, 1, TM_R * TOP_K), lambda i, *_: (i, 0, 0), memory_space=pltpu.SMEM)
    grid_spec = pltpu.PrefetchScalarGridSpec(
        num_scalar_prefetch=2,
        grid=(n // TM_R,),
        in_specs=[
            choice_spec, choice_spec,
            pl.BlockSpec((TM_R * LANE_CHUNKS, LANES), lambda i, *_: (i, 0)),
            pl.BlockSpec(memory_space=pl.ANY),
            vec_spec, vec_spec,
        ],
        out_specs=[pl.BlockSpec((TM_R, D_MODEL), lambda i, *_: (jnp.minimum(i, n_ptiles - 1), 0)),
                   pl.BlockSpec((TM_R, D_MODEL), lambda i, *_: (jnp.maximum(i - n_ptiles, 0), 0))],
        scratch_shapes=[pltpu.VMEM((2, TM_R * TOP_K * LANE_CHUNKS, LANES), F32),
                        pltpu.VMEM((TM_R * LANE_CHUNKS, LANES), F32),
                        pltpu.SemaphoreType.DMA((2,))],
    )
    return pl.pallas_call(
        functools.partial(_combine_kernel, alpha, n_ptiles),
        grid_spec=grid_spec,
        out_shape=[jax.ShapeDtypeStruct((n_prompt, D_MODEL), F32),
                   jax.ShapeDtypeStruct((n - n_prompt, D_MODEL), F32)],
        compiler_params=_params(1),
        name="moe_combine_ln2",
    )(tcnt, hstart, slot, top_w, x1g, ys, ln_g, ln_b)


def _route(tcnt, tbase, n_tiles):
    experts = jnp.arange(N_EXPERTS, dtype=jnp.int32)
    counts = tbase[-1] + tcnt[-1]
    padded = ((counts + TM_E - 1) // TM_E) * TM_E
    ends = jnp.cumsum(padded)
    hstart = (ends - padded)[None, :] + tbase
    n_valid = ends[-1] // TM_E
    last_expert = jnp.max(jnp.where(counts > 0, experts, 0))
    tile_start = jnp.arange(n_tiles, dtype=jnp.int32) * TM_E
    tile_expert = jnp.minimum(jnp.sum(ends[None, :] <= tile_start[:, None], axis=1), last_expert)
    return (tile_expert.astype(jnp.int32), n_valid.astype(jnp.int32).reshape(1),
            tcnt.reshape(-1), hstart.astype(jnp.int32).reshape(-1),
            (ends // TM_E - 1).astype(jnp.int32), (counts > 0).astype(jnp.int32))


def _t5_bucket(dist):
    max_exact = NUM_BUCKETS // 2
    distf = jnp.maximum(dist, 1).astype(F32)
    large = max_exact + (jnp.log(distf / max_exact) / math.log(MAX_DISTANCE / max_exact)
                         * (NUM_BUCKETS - max_exact)).astype(jnp.int32)
    large = jnp.minimum(large, NUM_BUCKETS - 1)
    return jnp.where(dist < max_exact, dist, large)


def _group_bias(rel_bias, gi, dil):
    j = jnp.arange(N_KEYS, dtype=jnp.int32)
    return rel_bias[_t5_bucket(dil * j)][:, gi * HPG:(gi + 1) * HPG].T.astype(F32)


def _step_bias(rel_bias):
    return jnp.stack([_group_bias(rel_bias, gi, dil) for gi, (_, dil) in enumerate(ATT_GROUPS)], axis=0).reshape(-1)


def _spatial_tables(w_sp, b_sp, dec_seq):
    depth = w_sp.shape[0]
    tril = jnp.tril(jnp.ones((CHUNK, CHUNK), bool))
    wp = jnp.where(tril, w_sp, 0.0)
    ws_small = jnp.where(tril[:dec_seq, :dec_seq], w_sp[:, :, :dec_seq, :dec_seq], 0.0)
    eye = jnp.eye(CHUNK // dec_seq, dtype=w_sp.dtype)
    ws = jnp.einsum("ab,lgij->lgaibj", eye, ws_small).reshape(depth, SG_GROUPS, CHUNK, CHUNK)
    bp = jnp.repeat(jnp.swapaxes(b_sp, 1, 2), SG_HEAD, axis=2)
    bs = jnp.tile(bp[:, :dec_seq], (1, CHUNK // dec_seq, 1))
    return jnp.stack([wp, ws], axis=1).astype(BF16), jnp.stack([bp, bs], axis=1)


def kernel(x_prompt, x_sample, state_conv, cache_k_w128, cache_v_w128, cache_k_w512, cache_v_w512,
           cache_k_w2048, cache_v_w2048, w_in, conv_w, sg_ln_g, sg_ln_b, w_sp, b_sp, rel_bias,
           w_up_a, w_up_b, w_up_c, w_o, ln1_g, ln1_b, w_router, b_router, w_gate, b_gate, w_up, b_up,
           w_down, b_down, ln2_g, ln2_b):
    batch, seq, _ = x_prompt.shape
    dec_batch, dec_seq, _ = x_sample.shape
    depth = w_in.shape[0]
    n_prompt, n_sample = batch * seq, dec_batch * dec_seq
    n = n_prompt + n_sample
    alpha = (2 * depth) ** 0.25
    assert seq % TM == 0 and n_sample % TM == 0 and dec_seq & (dec_seq - 1) == 0 and CHUNK % dec_seq == 0
    assert n_prompt % TM_R == 0 and n_sample % TM_R == 0
    assert all(seq % (dil * Q_BLK) == 0 and dil & (dil - 1) == 0 for _, dil in ATT_GROUPS)
    assert dec_batch % SEQ_BLK == 0
    n_tiles = (n * TOP_K) // TM_E + N_EXPERTS

    caches_t = [jnp.transpose(c, (0, 1, 3, 4, 2)).reshape(c.shape[0], c.shape[1], LANES, c.shape[2]) for c in
                (cache_k_w128, cache_v_w128, cache_k_w512, cache_v_w512, cache_k_w2048, cache_v_w2048)]
    bias_flat = _step_bias(rel_bias)
    wsp2, bsp2 = _spatial_tables(w_sp, b_sp, dec_seq)
    w_in_b, w_up_a_b, w_up_b_b, w_up_c_b, w_o_b = (w.astype(BF16) for w in (w_in, w_up_a, w_up_b, w_up_c, w_o))
    vec = lambda a: a.reshape(depth, 1, a.shape[-1])
    w_router_t = jnp.swapaxes(w_router, 1, 2)
    b_router_c = b_router.reshape(depth, N_EXPERTS, 1)
    expert_vec = lambda a: a.reshape(depth, N_EXPERTS, 1, a.shape[-1])

    x_p, x_s = x_prompt.reshape(n_prompt, D_MODEL), x_sample.reshape(n_sample, D_MODEL)
    outs ={name: [] for name in ("conv_p", "conv_s", "sgv", "kp", "vp", "ks", "vs")}
    for l in range(depth):
        s1 = jnp.zeros((dec_batch, dec_seq, CONV_DIM), F32).at[:, 0].set(state_conv[l, :, 1])
        s2 = jnp.zeros((dec_batch, dec_seq, CONV_DIM), F32).at[:, 0].set(state_conv[l, :, 0])
        s2 = s2.at[:, 1].set(state_conv[l, :, 1])
        q_p, k_p, v_p, q_s, k_s, v_s, mab, gc, tail, z_s, v_rows = _inproj_call(
            l, x_p, x_s, w_in_b, conv_w, vec(sg_ln_g), vec(sg_ln_b), wsp2, bsp2, w_up_a_b, w_up_b_b,
            s1.reshape(n_sample, CONV_DIM), s2.reshape(n_sample, CONV_DIM), seq, dec_seq)
        yc_p = _attn_prompt_call(q_p, k_p, v_p, bias_flat, batch, seq)
        yc_s = _attn_sample_call(l, q_s, k_s, v_s, caches_t, bias_flat, dec_batch, dec_seq)
        x1g, top_w, slot, tcnt, tbase = _merge_call(
            l, alpha, x_p, x_s, mab, gc, yc_p, yc_s, w_up_c_b, w_o_b, vec(ln1_g), vec(ln1_b), w_router_t, b_router_c)
        tile_expert, n_valid, tcnt, hstart, last_tile, has_rows = _route(tcnt[:, :, 0], tbase[:, :, 0], n_tiles)
        by_tile = lambda a: a.reshape(TOP_K, n // TM_R, TM_R).transpose(1, 2, 0).reshape(n // TM_R, 1, TM_R * TOP_K)
        slot, top_w = by_tile(slot), by_tile(top_w)
        xs = _dispatch_call(last_tile, has_rows, n_valid, tcnt, hstart, slot, x1g, n_tiles)
        ys = _moe_ffn_call(l, tile_expert, n_valid, xs, w_gate, expert_vec(b_gate),
                           w_up, expert_vec(b_up), w_down, expert_vec(b_down))
        x_p, x_s = _combine_call(l, alpha, tcnt, hstart, slot, top_w, x1g, ys, vec(ln2_g), vec(ln2_b), n_prompt)

        outs["conv_p"].append(tail[:, SUBLANES - (CONV_W - 1):])
        outs["conv_s"].append(z_s.reshape(dec_batch, dec_seq, CONV_DIM)[:, dec_seq - (CONV_W - 1):])
        outs["sgv"].append(v_rows.reshape(dec_batch, dec_seq, SG_DIM))
        kp4 = k_p.reshape(batch, seq, N_ATT, HPG, HEAD_DIM)
        vp4 = v_p.reshape(batch, seq, N_ATT, HPG, HEAD_DIM)
        ks4 = k_s.reshape(dec_batch, dec_seq, N_ATT, HPG, HEAD_DIM)
        vs4 = v_s.reshape(dec_batch, dec_seq, N_ATT, HPG, HEAD_DIM)
        outs["kp"].append([kp4[:, seq - min(win, seq):, gi] for gi, (win, _) in enumerate(ATT_GROUPS)])
        outs["vp"].append([vp4[:, seq - min(win, seq):, gi] for gi, (win, _) in enumerate(ATT_GROUPS)])
        outs["ks"].append([ks4[:, dec_seq - min(win, dec_seq):, gi] for gi, (win, _) in enumerate(ATT_GROUPS)])
        outs["vs"].append([vs4[:, dec_seq - min(win, dec_seq):, gi] for gi, (win, _) in enumerate(ATT_GROUPS)])

    stack = lambda name: jnp.stack(outs[name], axis=0)
    per_group = lambda name, gi: jnp.stack([layer[gi] for layer in outs[name]], axis=0)
    result = [x_p.reshape(batch, seq, D_MODEL), x_s.reshape(dec_batch, dec_seq, D_MODEL),
              stack("conv_p"), stack("conv_s")]
    for gi in range(N_ATT):
        result += [per_group("kp", gi), per_group("vp", gi)]
    for gi in range(N_ATT):
        result += [per_group("ks", gi), per_group("vs", gi)]
    result.append(stack("sgv"))
    return tuple(result)
```

```python
import functools
import math

import jax
import jax.numpy as jnp
from jax import lax
from jax.experimental import pallas as pl
from jax.experimental.pallas import tpu as pltpu

D_MODEL = 1024
CONV_DIM = 384
CONV_W = 3
SG_DIM = 256
SG_GROUPS = 4
SG_HEAD = SG_DIM // SG_GROUPS
CHUNK = 128
ATT_GROUPS = ((128, 1), (512, 4), (2048, 16))
N_ATT = len(ATT_GROUPS)
HPG = 4
HEAD_DIM = 32
ATT_DIM = HPG * N_ATT * HEAD_DIM
ATT_OUT = HPG * HEAD_DIM
N_KEYS = 129
NUM_BUCKETS = 32
MAX_DISTANCE = 2048
N_EXPERTS = 32
TOP_K = 4
SWIGLU_LIMIT = 7.0
SWIGLU_ALPHA = 1.702
LN_EPS = 1e-5
OFF_A = 0
OFF_B = OFF_A + 3 * CONV_DIM
OFF_C = OFF_B + 2 * SG_DIM
OFF_G = OFF_C + 3 * ATT_DIM
IN_DIM = OFF_G + 3 * D_MODEL
NEG = -1e30

LANES = 128
SUBLANES = 8
LANE_CHUNKS = D_MODEL // LANES
VMEM_LIMIT = 56 * 1024 * 1024

TM = 256
TM_R = 512
TM_E = 384
W_AHEAD = (2, 1, 0)
W_WARMUP = max(W_AHEAD)
W_SLOTS = W_WARMUP + 1
Q_BLK = 128
SEQ_BLK = 4
DMA_UNROLL = 8
BIG_RUN = 64

F32 = jnp.float32
BF16 = jnp.bfloat16


def _dot(a, b):
    return jnp.dot(a.astype(BF16), b.astype(BF16), preferred_element_type=F32)


def _dot_nt(a, b):
    return lax.dot_general(a.astype(BF16), b.astype(BF16), (((1,), (1,)), ((), ())),
                           preferred_element_type=F32)


def _layer_norm(x, g, b):
    mu = jnp.mean(x, axis=-1, keepdims=True)
    xc = x - mu
    var = jnp.mean(xc * xc, axis=-1, keepdims=True)
    return xc * lax.rsqrt(var + LN_EPS) * g + b


def _gelu(x):
    return 0.5 * x * (1.0 + lax.erf(x * (2.0 ** -0.5)))


def _sigmoid(x):
    return 1.0 / (1.0 + jnp.exp(-x))


def _params(n_axes):
    return pltpu.CompilerParams(dimension_semantics=("arbitrary",) * n_axes,
                                vmem_limit_bytes=VMEM_LIMIT)


def _inproj_kernel(n_ptiles, tiles_per_seq, dec_seq,
                   xp_ref, xs_ref, w_in_ref, convw_ref, lng_ref, lnb_ref, wsp_ref, bsp_ref, wua_ref, wub_ref,
                   s1_ref, s2_ref,
                   qp_ref, kp_ref, vp_ref, qs_ref, ks_ref, vs_ref, mab_ref, gc_ref, tail_ref, zs_ref, vrows_ref,
                   carry_ref):
    i = pl.program_id(0)
    is_prompt = i < n_ptiles

    @pl.when(jnp.logical_and(is_prompt, i % tiles_per_seq == 0))
    def _():
        carry_ref[...] = jnp.zeros_like(carry_ref)

    x = jnp.where(is_prompt, xp_ref[...], xs_ref[...]).astype(BF16)

    pa = jnp.dot(x, w_in_ref[:, OFF_A:OFF_B], preferred_element_type=F32)
    bg, cg, h = pa[:, :CONV_DIM], pa[:, CONV_DIM:2 * CONV_DIM], pa[:, 2 * CONV_DIM:]
    z = cg * h
    row = lax.broadcasted_iota(jnp.int32, (TM, CONV_DIM), 0)
    pos = jnp.where(is_prompt, row, row & (dec_seq - 1))
    prev1 = jnp.where(is_prompt, jnp.broadcast_to(carry_ref[7:8, :], (TM, CONV_DIM)), s1_ref[...])
    prev2 = jnp.where(is_prompt,
                      jnp.where(row == 0, jnp.broadcast_to(carry_ref[6:7, :], (TM, CONV_DIM)),
                                jnp.broadcast_to(carry_ref[7:8, :], (TM, CONV_DIM))),
                      s2_ref[...])
    z1 = jnp.where(pos >= 1, pltpu.roll(z, 1, axis=0), prev1)
    z2 = jnp.where(pos >= 2, pltpu.roll(z, 2, axis=0), prev2)
    cw = convw_ref[...]
    ya = bg * (cw[0:1, :] * z2 + cw[1:2, :] * z1 + cw[2:3, :] * z)

    @pl.when(is_prompt)
    def _():
        carry_ref[...] = z[TM - SUBLANES:, :]
        tail_ref[0] = z[TM - SUBLANES:, :]

    @pl.when(jnp.logical_not(is_prompt))
    def _():
        zs_ref[...] = z

    pb = jnp.dot(x, w_in_ref[:, OFF_B:OFF_C], preferred_element_type=F32)
    uv = _gelu(pb)
    u = uv[:, :SG_DIM]
    vn = _layer_norm(uv[:, SG_DIM:], lng_ref[...], lnb_ref[...])

    @pl.when(jnp.logical_not(is_prompt))
    def _():
        vrows_ref[...] = vn

    vb = vn.astype(BF16)
    low_half = lax.broadcasted_iota(jnp.int32, (CHUNK, LANES), 1) < SG_HEAD
    s_rows = []
    for c in range(TM // CHUNK):
        s_cols = []
        for p in range(SG_DIM // LANES):
            vcp = vb[c * CHUNK:(c + 1) * CHUNK, p * LANES:(p + 1) * LANES]
            sa = jnp.dot(wsp_ref[2 * p], vcp, preferred_element_type=F32)
            sb = jnp.dot(wsp_ref[2 * p + 1], vcp, preferred_element_type=F32)
            s_cols.append(jnp.where(low_half, sa, sb))
        s_rows.append(jnp.concatenate(s_cols, axis=1) + bsp_ref[...])
    yb = u * jnp.concatenate(s_rows, axis=0)

    pc = jnp.dot(x, w_in_ref[:, OFF_C:OFF_G], preferred_element_type=F32)
    for group_refs, active in (((qp_ref, kp_ref, vp_ref), is_prompt),
                               ((qs_ref, ks_ref, vs_ref), jnp.logical_not(is_prompt))):
        @pl.when(active)
        def _(group_refs=group_refs):
            for j, ref in enumerate(group_refs):
                ref[...] = pc[:, j * ATT_DIM:(j + 1) * ATT_DIM]

    ga = _sigmoid(jnp.dot(x, w_in_ref[:, OFF_G:OFF_G + D_MODEL], preferred_element_type=F32))
    mab = ga * _dot(ya, wua_ref[...])
    gb = _sigmoid(jnp.dot(x, w_in_ref[:, OFF_G + D_MODEL:OFF_G + 2 * D_MODEL], preferred_element_type=F32))
    mab_ref[...] = (mab + gb * _dot(yb, wub_ref[...])).astype(BF16)
    gc_ref[...] = _sigmoid(jnp.dot(x, w_in_ref[:, OFF_G + 2 * D_MODEL:], preferred_element_type=F32)).astype(BF16)


def _inproj_call(l, x_p, x_s, w_in, conv_w, sg_ln_g, sg_ln_b, wsp2, bsp2, w_up_a, w_up_b, s1, s2, seq, dec_seq):
    n_prompt, n_sample = x_p.shape[0], x_s.shape[0]
    n = n_prompt + n_sample
    n_ptiles = n_prompt // TM
    n_stiles = n_sample // TM
    tiles_per_seq = seq // TM
    batch = n_prompt // seq

    def stile(i):
        return jnp.maximum(i - n_ptiles, 0)

    def mode(i):
        return jnp.where(i < n_ptiles, 0, 1)

    row_spec = lambda w: pl.BlockSpec((TM, w), lambda i: (i, 0))
    srow_spec = lambda w: pl.BlockSpec((TM, w), lambda i: (stile(i), 0))
    prow_spec = lambda w: pl.BlockSpec((TM, w), lambda i: (jnp.minimum(i, n_ptiles - 1), 0))
    kern = functools.partial(_inproj_kernel, n_ptiles, tiles_per_seq, dec_seq)
    return pl.pallas_call(
        kern,
        grid=(n_ptiles + n_stiles,),
        in_specs=[
            prow_spec(D_MODEL), srow_spec(D_MODEL),
            pl.BlockSpec((None, D_MODEL, IN_DIM), lambda i: (l, 0, 0), pipeline_mode=pl.Buffered(1)),
            pl.BlockSpec((None, CONV_W, CONV_DIM), lambda i: (l, 0, 0)),
            pl.BlockSpec((None, 1, SG_DIM), lambda i: (l, 0, 0)),
            pl.BlockSpec((None, 1, SG_DIM), lambda i: (l, 0, 0)),
            pl.BlockSpec((None, None, SG_GROUPS, CHUNK, CHUNK), lambda i: (l, mode(i), 0, 0, 0)),
            pl.BlockSpec((None, None, CHUNK, SG_DIM), lambda i: (l, mode(i), 0, 0)),
            pl.BlockSpec((None, CONV_DIM, D_MODEL), lambda i: (l, 0, 0)),
            pl.BlockSpec((None, SG_DIM, D_MODEL), lambda i: (l, 0, 0)),
            srow_spec(CONV_DIM),
            srow_spec(CONV_DIM),
        ],
        out_specs=[
            prow_spec(ATT_DIM), prow_spec(ATT_DIM), prow_spec(ATT_DIM),
            srow_spec(ATT_DIM), srow_spec(ATT_DIM), srow_spec(ATT_DIM),
            row_spec(D_MODEL), row_spec(D_MODEL),
            pl.BlockSpec((1, SUBLANES, CONV_DIM),
                         lambda i: (jnp.minimum(i // tiles_per_seq, batch - 1), 0, 0)),
            srow_spec(CONV_DIM),
            srow_spec(SG_DIM),
        ],
        out_shape=[jax.ShapeDtypeStruct((n_prompt, ATT_DIM), F32)] * 3 + [jax.ShapeDtypeStruct((n_sample, ATT_DIM), F32)] * 3 + [
            jax.ShapeDtypeStruct((n, D_MODEL), BF16), jax.ShapeDtypeStruct((n, D_MODEL), BF16),
            jax.ShapeDtypeStruct((batch, SUBLANES, CONV_DIM), F32),
            jax.ShapeDtypeStruct((n_sample, CONV_DIM), F32),
            jax.ShapeDtypeStruct((n_sample, SG_DIM), F32),
        ],
        scratch_shapes=[pltpu.VMEM((SUBLANES, CONV_DIM), F32)],
        compiler_params=_params(1),
        name="inproj_mix",
    )(x_p, x_s, w_in, conv_w, sg_ln_g, sg_ln_b, wsp2, bsp2, w_up_a, w_up_b, s1, s2)


def _head_masks(rows):
    lane_head = lax.broadcasted_iota(jnp.int32, (rows, LANES), 1) // HEAD_DIM
    return [lane_head == h for h in range(HPG)]


def _stack_heads(qb, masks):
    return jnp.concatenate([jnp.where(m, qb, 0.0) for m in masks], axis=0)


def _unstack_heads(stacked, masks, r):
    out = None
    for h, m in enumerate(masks):
        part = jnp.where(m, stacked[h * r:(h + 1) * r], 0.0)
        out = part if out is None else out + part
    return out


def _mix_groups(outs, lses):
    mx = functools.reduce(jnp.maximum, lses)
    ws = [jnp.exp(ls - mx) for ls in lses]
    num = functools.reduce(lambda a, b: a + b, [w * o for w, o in zip(ws, outs)])
    return num / functools.reduce(lambda a, b: a + b, ws)


def _bias_table(bias_ref, base, steps):
    def body(j, tab):
        return jnp.where(steps == j, bias_ref[base + j], tab)

    return lax.fori_loop(0, N_KEYS, body, jnp.full(steps.shape, NEG, F32))


def _attn_prompt_kernel(seq, bias_ref, q_ref, k_ref, v_ref, o_ref, tab_scr, *scr):
    b = pl.program_id(0)
    g = pl.program_id(1)
    o_scr, l_scr = scr[:N_ATT], scr[N_ATT:]
    masks = _head_masks(Q_BLK)
    scale = HEAD_DIM ** -0.5
    col = lax.broadcasted_iota(jnp.int32, (HPG * Q_BLK, 2 * Q_BLK), 1)

    def build_table(gi):
        qi = lax.broadcasted_iota(jnp.int32, (Q_BLK, 2 * Q_BLK), 0)
        kc = lax.broadcasted_iota(jnp.int32, (Q_BLK, 2 * Q_BLK), 1)
        for h in range(HPG):
            tab_scr[gi, h * Q_BLK:(h + 1) * Q_BLK, :] = _bias_table(bias_ref, (gi * HPG + h) * N_KEYS,
                                                                    qi + Q_BLK - kc)

    def run_group(gi, dil):
        rows_per_class = seq // dil
        n_blk = rows_per_class // Q_BLK

        def ld(ref, start):
            if dil > 1:
                return ref[pl.ds(start, Q_BLK, stride=dil), :]
            return ref[pl.ds(start, Q_BLK), :]

        def st(ref, start, val):
            if dil > 1:
                ref[pl.ds(start, Q_BLK, stride=dil), :] = val
            else:
                ref[pl.ds(start, Q_BLK), :] = val

        def block(it, carry):
            r = it // n_blk
            mb = it % n_blk
            cur = r + dil * Q_BLK * mb
            prev = r + dil * Q_BLK * jnp.maximum(mb - 1, 0)
            qb = ld(q_ref, cur) * scale
            kw = jnp.concatenate([ld(k_ref, prev), ld(k_ref, cur)], axis=0)
            vw = jnp.concatenate([ld(v_ref, prev), ld(v_ref, cur)], axis=0)
            s = _dot_nt(_stack_heads(qb, masks), kw) + tab_scr[gi]
            s = jnp.where(jnp.logical_or(col >= Q_BLK, mb > 0), s, NEG)
            m = jnp.max(s, axis=1, keepdims=True)
            p = jnp.exp(s - m)
            den = jnp.sum(p, axis=1, keepdims=True)
            pv = _dot(p, vw)
            st(o_scr[gi], cur, _unstack_heads(pv / den, masks, Q_BLK))
            st(l_scr[gi], cur, _unstack_heads(m + jnp.log(den), masks, Q_BLK))
            return carry

        lax.fori_loop(0, dil * n_blk, block, 0, unroll=4)

    for gi, (_, dil) in enumerate(ATT_GROUPS):
        @pl.when(jnp.logical_and(g == gi, b == 0))
        def _(gi=gi):
            build_table(gi)

        @pl.when(g == gi)
        def _(gi=gi, dil=dil):
            run_group(gi, dil)

    @pl.when(g == N_ATT - 1)
    def _():
        o_ref[...] = _mix_groups([s[...] for s in o_scr], [s[...] for s in l_scr])


def _attn_prompt_call(q, k, v, bias_flat, batch, seq):
    blk = lambda: pl.BlockSpec((seq, LANES), lambda b, g: (b, g))
    return pl.pallas_call(
        functools.partial(_attn_prompt_kernel, seq),
        grid=(batch, N_ATT),
        in_specs=[pl.BlockSpec(memory_space=pltpu.SMEM), blk(), blk(), blk()],
        out_specs=pl.BlockSpec((seq, ATT_OUT), lambda b, g: (b, 0)),
        out_shape=jax.ShapeDtypeStruct((batch * seq, ATT_OUT), F32),
        scratch_shapes=[pltpu.VMEM((N_ATT, HPG * Q_BLK, 2 * Q_BLK), F32)]
                       + [pltpu.VMEM((seq, LANES), F32)] * (2 * N_ATT),
        compiler_params=_params(2),
        name="attn_prompt",
    )(bias_flat, q, k, v)


def _attn_sample_kernel(dec_seq, pasts, bias_ref, q_ref, k_ref, v_ref, kc0, vc0, kc1, vc1, kc2, vc2,
                        o_ref, tc0, tc1, tc2, tn_scr):
    kcs, vcs, tcs = (kc0, kc1, kc2), (vc0, vc1, vc2), (tc0, tc1, tc2)
    masks = _head_masks(dec_seq)
    scale = HEAD_DIM ** -0.5

    @pl.when(pl.program_id(0) == 0)
    def _():
        for gi, (_, dil) in enumerate(ATT_GROUPS):
            past = pasts[gi]
            shift = dil.bit_length() - 1

            def steps(dist, dil=dil, shift=shift):
                return jnp.where((dist & (dil - 1)) == 0, dist >> shift, -1)

            qi_c = lax.broadcasted_iota(jnp.int32, (dec_seq, past), 0)
            row_c = lax.broadcasted_iota(jnp.int32, (dec_seq, past), 1)
            qi_n = lax.broadcasted_iota(jnp.int32, (dec_seq, dec_seq), 0)
            row_n = lax.broadcasted_iota(jnp.int32, (dec_seq, dec_seq), 1)
            for h in range(HPG):
                base = (gi * HPG + h) * N_KEYS
                tcs[gi][h * dec_seq:(h + 1) * dec_seq, :] = _bias_table(bias_ref, base, steps(past + qi_c - row_c))
                tn_scr[gi, h * dec_seq:(h + 1) * dec_seq, :] = _bias_table(bias_ref, base, steps(qi_n - row_n))

    for s_i in range(SEQ_BLK):
        rows = slice(s_i * dec_seq, (s_i + 1) * dec_seq)
        outs, lses = [], []
        for gi in range(N_ATT):
            cols = slice(gi * LANES, (gi + 1) * LANES)
            qs = _stack_heads(q_ref[rows, cols] * scale, masks)
            k_new, v_new = k_ref[rows, cols], v_ref[rows, cols]
            s_c = _dot(qs, kcs[gi][s_i]) + tcs[gi][...]
            s_n = _dot_nt(qs, k_new) + tn_scr[gi]
            m = jnp.maximum(jnp.max(s_c, axis=1, keepdims=True), jnp.max(s_n, axis=1, keepdims=True))
            p_c = jnp.exp(s_c - m)
            p_n = jnp.exp(s_n - m)
            den = jnp.sum(p_c, axis=1, keepdims=True) + jnp.sum(p_n, axis=1, keepdims=True)
            pv = _dot_nt(p_c, vcs[gi][s_i]) + _dot(p_n, v_new)
            outs.append(_unstack_heads(pv / den, masks, dec_seq))
            lses.append(_unstack_heads(m + jnp.log(den), masks, dec_seq))
        o_ref[rows, :] = _mix_groups(outs, lses)


def _attn_sample_call(l, q, k, v, caches_t, bias_flat, dec_batch, dec_seq):
    rows = SEQ_BLK * dec_seq
    pasts = tuple(caches_t[2 * gi].shape[3] for gi in range(N_ATT))
    new_spec = pl.BlockSpec((rows, ATT_DIM), lambda i: (i, 0))
    cache_specs = [pl.BlockSpec((None, SEQ_BLK, LANES, c.shape[3]), lambda i: (l, i, 0, 0)) for c in caches_t]
    return pl.pallas_call(
        functools.partial(_attn_sample_kernel, dec_seq, pasts),
        grid=(dec_batch // SEQ_BLK,),
        in_specs=[pl.BlockSpec(memory_space=pltpu.SMEM), new_spec, new_spec, new_spec] + cache_specs,
        out_specs=pl.BlockSpec((rows, ATT_OUT), lambda i: (i, 0)),
        out_shape=jax.ShapeDtypeStruct((dec_batch * dec_seq, ATT_OUT), F32),
        scratch_shapes=[pltpu.VMEM((HPG * dec_seq, p), F32) for p in pasts]
                       + [pltpu.VMEM((N_ATT, HPG * dec_seq, dec_seq), F32)],
        compiler_params=_params(1),
        name="attn_sample",
    )(bias_flat, q, k, v, *caches_t)


def _merge_kernel(alpha, n_ptiles, xp_ref, xs_ref, mab_ref, gc_ref, ycp_ref, ycs_ref, wuc_ref, wo_ref, g_ref, b_ref,
                  wrt_ref, br_ref, x1g_ref, w_ref, slot_ref, tcnt_ref, tbase_ref, cnt_scr):
    @pl.when(pl.program_id(0) == 0)
    def _():
        cnt_scr[...] = jnp.zeros_like(cnt_scr)

    is_prompt = pl.program_id(0) < n_ptiles
    x = jnp.where(is_prompt, xp_ref[...], xs_ref[...])
    yc = jnp.where(is_prompt, ycp_ref[...], ycs_ref[...])
    merged = mab_ref[...].astype(F32) + gc_ref[...].astype(F32) * _dot(yc, wuc_ref[...])
    x1 = _layer_norm(alpha * x + _dot(merged, wo_ref[...]), g_ref[...], b_ref[...])
    for c in range(LANE_CHUNKS):
        x1g_ref[pl.ds(c, TM_R, stride=LANE_CHUNKS), :] = x1[:, c * LANES:(c + 1) * LANES]

    logits = lax.dot_general(wrt_ref[...], x1, (((1,), (1,)), ((), ())),
                             precision=lax.Precision.HIGHEST, preferred_element_type=F32) + br_ref[...]
    eio = lax.broadcasted_iota(jnp.int32, (N_EXPERTS, TM_R), 0)
    vals, idxs = [], []
    for _ in range(TOP_K):
        mv = jnp.max(logits, axis=0, keepdims=True)
        ix = jnp.min(jnp.where(logits == mv, eio, N_EXPERTS), axis=0, keepdims=True)
        vals.append(mv)
        idxs.append(ix)
        logits = jnp.where(eio == ix, -jnp.inf, logits)
    tv = jnp.concatenate(vals, axis=0)
    e = jnp.exp(tv - tv[0:1, :])
    w_ref[...] = e / jnp.sum(e, axis=0, keepdims=True)

    onehots = [eio == ix for ix in idxs]
    chosen = functools.reduce(lambda a, c: a + c, [jnp.where(oh, 1.0, 0.0) for oh in onehots])
    earlier = (lax.broadcasted_iota(jnp.int32, (TM_R, TM_R), 0) < lax.broadcasted_iota(jnp.int32, (TM_R, TM_R), 1))
    within = _dot(chosen, jnp.where(earlier, 1.0, 0.0))
    tile_cnt = jnp.broadcast_to(jnp.sum(chosen, axis=1, keepdims=True), (N_EXPERTS, LANES))
    lower = (lax.broadcasted_iota(jnp.int32, (N_EXPERTS, N_EXPERTS), 1)
             < lax.broadcasted_iota(jnp.int32, (N_EXPERTS, N_EXPERTS), 0))
    place = within + jnp.dot(jnp.where(lower, 1.0, 0.0), tile_cnt, precision=lax.Precision.HIGHEST,
                             preferred_element_type=F32)[:, 0:1]
    slot_ref[...] = jnp.concatenate(
        [jnp.sum(jnp.where(oh, place, 0.0), axis=0, keepdims=True) for oh in onehots],
        axis=0).astype(jnp.int32) * LANE_CHUNKS
    tcnt_ref[...] = tile_cnt.astype(jnp.int32)
    tbase_ref[...] = cnt_scr[...].astype(jnp.int32)
    cnt_scr[...] = cnt_scr[...] + tile_cnt


def _merge_call(l, alpha, x_p, x_s, mab, gc, yc_p, yc_s, w_up_c, w_o, ln_g, ln_b, w_router_t, b_router):
    n_ptiles = x_p.shape[0] // TM_R
    n = x_p.shape[0] + x_s.shape[0]
    row_spec = lambda w: pl.BlockSpec((TM_R, w), lambda i: (i, 0))
    prow_spec = lambda w: pl.BlockSpec((TM_R, w), lambda i: (jnp.minimum(i, n_ptiles - 1), 0))
    srow_spec = lambda w: pl.BlockSpec((TM_R, w), lambda i: (jnp.maximum(i - n_ptiles, 0), 0))
    vec_spec = pl.BlockSpec((None, 1, D_MODEL), lambda i: (l, 0, 0))
    return pl.pallas_call(
        functools.partial(_merge_kernel, alpha, n_ptiles),
        grid=(n // TM_R,),
        in_specs=[
            prow_spec(D_MODEL), srow_spec(D_MODEL), row_spec(D_MODEL), row_spec(D_MODEL),
            prow_spec(ATT_OUT), srow_spec(ATT_OUT),
            pl.BlockSpec((None, ATT_OUT, D_MODEL), lambda i: (l, 0, 0)),
            pl.BlockSpec((None, D_MODEL, D_MODEL), lambda i: (l, 0, 0)),
            vec_spec, vec_spec,
            pl.BlockSpec((None, N_EXPERTS, D_MODEL), lambda i: (l, 0, 0)),
            pl.BlockSpec((None, N_EXPERTS, 1), lambda i: (l, 0, 0)),
        ],
        out_specs=[
            pl.BlockSpec((TM_R * LANE_CHUNKS, LANES), lambda i: (i, 0)),
            pl.BlockSpec((TOP_K, TM_R), lambda i: (0, i)),
            pl.BlockSpec((TOP_K, TM_R), lambda i: (0, i)),
            pl.BlockSpec((None, N_EXPERTS, LANES), lambda i: (i, 0, 0)),
            pl.BlockSpec((None, N_EXPERTS, LANES), lambda i: (i, 0, 0)),
        ],
        out_shape=[
            jax.ShapeDtypeStruct((n * LANE_CHUNKS, LANES), F32),
            jax.ShapeDtypeStruct((TOP_K, n), F32),
            jax.ShapeDtypeStruct((TOP_K, n), jnp.int32),
            jax.ShapeDtypeStruct((n // TM_R, N_EXPERTS, LANES), jnp.int32),
            jax.ShapeDtypeStruct((n // TM_R, N_EXPERTS, LANES), jnp.int32),
        ],
        scratch_shapes=[pltpu.VMEM((N_EXPERTS, LANES), F32)],
        compiler_params=_params(1),
        name="merge_ln1_router",
    )(x_p, x_s, mab, gc, yc_p, yc_s, w_up_c, w_o, ln_g, ln_b, w_router_t, b_router)


def _destride(ref, rows):
    return jnp.concatenate([ref[pl.ds(c, rows, stride=LANE_CHUNKS), :] for c in range(LANE_CHUNKS)], axis=1)


def _slab_rows(ref, row, count=1):
    return ref.at[pl.ds(pl.multiple_of(row, LANE_CHUNKS), count * LANE_CHUNKS), :]


def _start_expert_blocks(tile, tcnt_ref, hstart_ref, make_copy):
    big_bits = BIG_RUN.bit_length() - 1

    def per_expert(e, staged):
        cnt = tcnt_ref[tile * N_EXPERTS + e]
        first = hstart_ref[tile * N_EXPERTS + e]
        n_big = cnt >> big_bits

        def big(j, carry):
            make_copy(staged + j * BIG_RUN, first + j * BIG_RUN, BIG_RUN).start()
            return carry

        lax.fori_loop(0, n_big, big, 0)
        for bit in reversed(range(big_bits)):
            size = 1 << bit
            done = (cnt >> (bit + 1)) << (bit + 1)

            @pl.when((cnt & size) != 0)
            def _(size=size, done=done):
                make_copy(staged + done, first + done, size).start()
        return staged + cnt

    lax.fori_loop(0, N_EXPERTS, per_expert, 0)


def _dispatch_kernel(n_tiles, lt_ref, has_ref, nv_ref, tcnt_ref, hstart_ref, slot_ref, x1g_ref, xs_hbm,
                     zbuf, stg, zsem, sem):
    tile_rows = TM_E * LANE_CHUNKS
    i = pl.program_id(0)
    n_steps = pl.num_programs(0)
    buf = i % 2
    staged_all = lambda b: pltpu.make_async_copy(stg.at[b], xs_hbm.at[pl.ds(0, TM_R * TOP_K * LANE_CHUNKS), :],
                                                 sem.at[b])

    @pl.when(i == 0)
    def _():
        zbuf[...] = jnp.zeros_like(zbuf)
        zero_copy = lambda t: pltpu.make_async_copy(zbuf, xs_hbm.at[pl.ds(t * tile_rows, tile_rows), :], zsem)
        for e in range(N_EXPERTS):
            @pl.when(has_ref[e] > 0)
            def _(e=e):
                zero_copy(lt_ref[e]).start()
        lax.fori_loop(nv_ref[0], n_tiles, lambda t, c: (zero_copy(t).start(), c)[1], 0)
        for e in range(N_EXPERTS):
            @pl.when(has_ref[e] > 0)
            def _(e=e):
                zero_copy(lt_ref[e]).wait()
        lax.fori_loop(nv_ref[0], n_tiles, lambda t, c: (zero_copy(t).wait(), c)[1], 0)

    def step(b):
        @pl.when(i >= 2)
        def _():
            staged_all(b).wait()

        def place(t, carry):
            slab = _slab_rows(x1g_ref, t * LANE_CHUNKS)[...]
            for kk in range(TOP_K):
                _slab_rows(stg.at[b], slot_ref[0, 0, t * TOP_K + kk])[...] = slab
            return carry

        lax.fori_loop(0, TM_R, place, 0, unroll=DMA_UNROLL)
        _start_expert_blocks(i, tcnt_ref, hstart_ref, lambda staged, row, size: pltpu.make_async_copy(
            _slab_rows(stg.at[b], staged * LANE_CHUNKS, size), _slab_rows(xs_hbm, row * LANE_CHUNKS, size),
            sem.at[b]))

        @pl.when(i == n_steps - 1)
        def _():
            staged_all(b).wait()

        @pl.when(jnp.logical_and(i == n_steps - 1, i >= 1))
        def _():
            staged_all(1 - b).wait()

    for b in range(2):
        @pl.when(buf == b)
        def _(b=b):
            step(b)


def _dispatch_call(last_tile, has_rows, n_valid, tcnt, hstart, slot, x1g, n_tiles):
    n = slot.shape[0] * TM_R
    grid_spec = pltpu.PrefetchScalarGridSpec(
        num_scalar_prefetch=5,
        grid=(n // TM_R,),
        in_specs=[
            pl.BlockSpec((1, 1, TM_R * TOP_K), lambda i, *_: (i, 0, 0), memory_space=pltpu.SMEM),
            pl.BlockSpec((TM_R * LANE_CHUNKS, LANES), lambda i, *_: (i, 0)),
        ],
        out_specs=pl.BlockSpec(memory_space=pl.ANY),
        scratch_shapes=[pltpu.VMEM((TM_E * LANE_CHUNKS, LANES), F32),
                        pltpu.VMEM((2, TM_R * TOP_K * LANE_CHUNKS, LANES), F32),
                        pltpu.SemaphoreType.DMA(()), pltpu.SemaphoreType.DMA((2,))],
    )
    return pl.pallas_call(
        functools.partial(_dispatch_kernel, n_tiles),
        grid_spec=grid_spec,
        out_shape=jax.ShapeDtypeStruct((n_tiles * TM_E * LANE_CHUNKS, LANES), F32),
        compiler_params=_params(1),
        name="moe_dispatch",
    )(last_tile, has_rows, n_valid, tcnt, hstart, slot, x1g)


def _moe_ffn_kernel(te_ref, nv_ref, wslot_ref, xs_ref, wg_ref, bg_ref, wu_ref, bu_ref, wd_ref, bd_ref, ys_ref, w_b16):
    s = pl.program_id(0)
    t = s - W_WARMUP
    last = nv_ref[0] - 1
    expert_at = lambda tile: te_ref[jnp.clip(tile, 0, last)]

    for j, w_ref in enumerate((wg_ref, wu_ref, wd_ref)):
        arriving = expert_at(t + W_AHEAD[j])

        @pl.when(jnp.logical_or(s == 0, arriving != expert_at(t + W_AHEAD[j] - 1)))
        def _(j=j, w_ref=w_ref, arriving=arriving):
            w_b16[wslot_ref[arriving], j] = w_ref[...].astype(BF16)

    @pl.when(jnp.logical_and(t >= 0, t <= last))
    def _():
        slot = wslot_ref[te_ref[jnp.maximum(t, 0)]]
        x = _destride(xs_ref, TM_E).astype(BF16)
        gl = jnp.minimum(jnp.dot(x, w_b16[slot, 0], preferred_element_type=F32) + bg_ref[...], SWIGLU_LIMIT)
        ul = jnp.clip(jnp.dot(x, w_b16[slot, 1], preferred_element_type=F32) + bu_ref[...],
                      -SWIGLU_LIMIT, SWIGLU_LIMIT)
        hid = gl * _sigmoid(SWIGLU_ALPHA * gl) * (ul + 1.0)
        y = jnp.dot(hid.astype(BF16), w_b16[slot, 2], preferred_element_type=F32) + bd_ref[...]
        for c in range(LANE_CHUNKS):
            ys_ref[pl.ds(c, TM_E, stride=LANE_CHUNKS), :] = y[:, c * LANES:(c + 1) * LANES]

    @pl.when(t > last)
    def _():
        ys_ref[...] = jnp.zeros_like(ys_ref)


def _moe_ffn_call(l, tile_expert, n_valid, weight_slot, xs, w_gate, b_gate, w_up, b_up, w_down, b_down):
    n_tiles = tile_expert.shape[0]
    tile = lambda s, nv, ahead=0: jnp.clip(s - W_WARMUP + ahead, 0, nv[0] - 1)
    mat_spec = lambda ahead: pl.BlockSpec((None, None, D_MODEL, D_MODEL),
                                          lambda s, te, nv, ws: (l, te[tile(s, nv, ahead)], 0, 0))
    vec_spec = pl.BlockSpec((None, None, 1, D_MODEL), lambda s, te, nv, ws: (l, te[tile(s, nv)], 0, 0))
    grid_spec = pltpu.PrefetchScalarGridSpec(
        num_scalar_prefetch=3,
        grid=(n_tiles + W_WARMUP,),
        in_specs=[pl.BlockSpec((TM_E * LANE_CHUNKS, LANES), lambda s, te, nv, ws: (tile(s, nv), 0)),
                  mat_spec(W_AHEAD[0]), vec_spec, mat_spec(W_AHEAD[1]), vec_spec, mat_spec(W_AHEAD[2]), vec_spec],
        out_specs=pl.BlockSpec((TM_E * LANE_CHUNKS, LANES), lambda s, te, nv, ws: (jnp.maximum(s - W_WARMUP, 0), 0)),
        scratch_shapes=[pltpu.VMEM((W_SLOTS, 3, D_MODEL, D_MODEL), BF16)],
    )
    return pl.pallas_call(
        _moe_ffn_kernel,
        grid_spec=grid_spec,
        out_shape=jax.ShapeDtypeStruct((n_tiles * TM_E * LANE_CHUNKS, LANES), F32),
        compiler_params=_params(1),
        name="moe_ffn",
    )(tile_expert, n_valid, weight_slot, xs, w_gate, b_gate, w_up, b_up, w_down, b_down)


def _combine_kernel(alpha, n_ptiles, tcnt_ref, hstart_ref, slot_ref, w_ref, x1g_ref, ys_hbm, g_ref, b_ref,
                    x2p_ref, x2s_ref, stg, mixed, sem):
    i = pl.program_id(0)
    n_steps = pl.num_programs(0)
    buf = i % 2

    def fetch(tile, b):
        _start_expert_blocks(tile, tcnt_ref, hstart_ref, lambda staged, row, size: pltpu.make_async_copy(
            _slab_rows(ys_hbm, row * LANE_CHUNKS, size), _slab_rows(stg.at[b], staged * LANE_CHUNKS, size),
            sem.at[b]))

    def step(b):
        @pl.when(i == 0)
        def _():
            fetch(0, b)

        @pl.when(i + 1 < n_steps)
        def _():
            fetch(i + 1, 1 - b)

        pltpu.make_async_copy(ys_hbm.at[pl.ds(0, TM_R * TOP_K * LANE_CHUNKS), :], stg.at[b], sem.at[b]).wait()

        def mix(t, carry):
            c0 = t * TOP_K
            acc = w_ref[0, 0, c0] * _slab_rows(stg.at[b], slot_ref[0, 0, c0])[...]
            for kk in range(1, TOP_K):
                acc = acc + w_ref[0, 0, c0 + kk] * _slab_rows(stg.at[b], slot_ref[0, 0, c0 + kk])[...]
            _slab_rows(mixed, t * LANE_CHUNKS)[...] = acc
            return carry

        lax.fori_loop(0, TM_R, mix, 0, unroll=DMA_UNROLL)

    for b in range(2):
        @pl.when(buf == b)
        def _(b=b):
            step(b)

    x2 = _layer_norm(alpha * _destride(x1g_ref, TM_R) + _destride(mixed, TM_R), g_ref[...], b_ref[...])

    @pl.when(i < n_ptiles)
    def _():
        x2p_ref[...] = x2

    @pl.when(i >= n_ptiles)
    def _():
        x2s_ref[...] = x2


def _combine_call(l, alpha, tcnt, hstart, slot, top_w, x1g, ys, ln_g, ln_b, n_prompt):
    n = x1g.shape[0] // LANE_CHUNKS
    n_ptiles = n_prompt // TM_R
    vec_spec = pl.BlockSpec((None, 1, D_MODEL), lambda i, *_: (l, 0, 0))
    choice_spec = pl.BlockSpec((1, 1, TM_R * TOP_K), lambda i, *_: (i, 0, 0), memory_space=pltpu.SMEM)
    grid_spec = pltpu.PrefetchScalarGridSpec(
        num_scalar_prefetch=2,
        grid=(n // TM_R,),
        in_specs=[
            choice_spec, choice_spec,
            pl.BlockSpec((TM_R * LANE_CHUNKS, LANES), lambda i, *_: (i, 0)),
            pl.BlockSpec(memory_space=pl.ANY),
            vec_spec, vec_spec,
        ],
        out_specs=[pl.BlockSpec((TM_R, D_MODEL), lambda i, *_: (jnp.minimum(i, n_ptiles - 1), 0)),
                   pl.BlockSpec((TM_R, D_MODEL), lambda i, *_: (jnp.maximum(i - n_ptiles, 0), 0))],
        scratch_shapes=[pltpu.VMEM((2, TM_R * TOP_K * LANE_CHUNKS, LANES), F32),
                        pltpu.VMEM((TM_R * LANE_CHUNKS, LANES), F32),
                        pltpu.SemaphoreType.DMA((2,))],
    )
    return pl.pallas_call(
        functools.partial(_combine_kernel, alpha, n_ptiles),
        grid_spec=grid_spec,
        out_shape=[jax.ShapeDtypeStruct((n_prompt, D_MODEL), F32),
                   jax.ShapeDtypeStruct((n - n_prompt, D_MODEL), F32)],
        compiler_params=_params(1),
        name="moe_combine_ln2",
    )(tcnt, hstart, slot, top_w, x1g, ys, ln_g, ln_b)


def _route(tcnt, tbase, n_tiles):
    experts = jnp.arange(N_EXPERTS, dtype=jnp.int32)
    counts = tbase[-1] + tcnt[-1]
    padded = ((counts + TM_E - 1) // TM_E) * TM_E
    ends = jnp.cumsum(padded)
    hstart = (ends - padded)[None, :] + tbase
    n_valid = ends[-1] // TM_E
    last_expert = jnp.max(jnp.where(counts > 0, experts, 0))
    tile_start = jnp.arange(n_tiles, dtype=jnp.int32) * TM_E
    tile_expert = jnp.minimum(jnp.sum(ends[None, :] <= tile_start[:, None], axis=1), last_expert)
    has = (counts > 0).astype(jnp.int32)
    weight_slot = (jnp.cumsum(has) - has) % W_SLOTS
    return (tile_expert.astype(jnp.int32), n_valid.astype(jnp.int32).reshape(1), weight_slot.astype(jnp.int32),
            tcnt.reshape(-1), hstart.astype(jnp.int32).reshape(-1), (ends // TM_E - 1).astype(jnp.int32), has)


def _t5_bucket(dist):
    max_exact = NUM_BUCKETS // 2
    distf = jnp.maximum(dist, 1).astype(F32)
    large = max_exact + (jnp.log(distf / max_exact) / math.log(MAX_DISTANCE / max_exact)
                         * (NUM_BUCKETS - max_exact)).astype(jnp.int32)
    large = jnp.minimum(large, NUM_BUCKETS - 1)
    return jnp.where(dist < max_exact, dist, large)


def _group_bias(rel_bias, gi, dil):
    j = jnp.arange(N_KEYS, dtype=jnp.int32)
    return rel_bias[_t5_bucket(dil * j)][:, gi * HPG:(gi + 1) * HPG].T.astype(F32)


def _step_bias(rel_bias):
    return jnp.stack([_group_bias(rel_bias, gi, dil) for gi, (_, dil) in enumerate(ATT_GROUPS)], axis=0).reshape(-1)


def _spatial_tables(w_sp, b_sp, dec_seq):
    depth = w_sp.shape[0]
    tril = jnp.tril(jnp.ones((CHUNK, CHUNK), bool))
    wp = jnp.where(tril, w_sp, 0.0)
    ws_small = jnp.where(tril[:dec_seq, :dec_seq], w_sp[:, :, :dec_seq, :dec_seq], 0.0)
    eye = jnp.eye(CHUNK // dec_seq, dtype=w_sp.dtype)
    ws = jnp.einsum("ab,lgij->lgaibj", eye, ws_small).reshape(depth, SG_GROUPS, CHUNK, CHUNK)
    bp = jnp.repeat(jnp.swapaxes(b_sp, 1, 2), SG_HEAD, axis=2)
    bs = jnp.tile(bp[:, :dec_seq], (1, CHUNK // dec_seq, 1))
    return jnp.stack([wp, ws], axis=1).astype(BF16), jnp.stack([bp, bs], axis=1)


def kernel(x_prompt, x_sample, state_conv, cache_k_w128, cache_v_w128, cache_k_w512, cache_v_w512,
           cache_k_w2048, cache_v_w2048, w_in, conv_w, sg_ln_g, sg_ln_b, w_sp, b_sp, rel_bias,
           w_up_a, w_up_b, w_up_c, w_o, ln1_g, ln1_b, w_router, b_router, w_gate, b_gate, w_up, b_up,
           w_down, b_down, ln2_g, ln2_b):
    batch, seq, _ = x_prompt.shape
    dec_batch, dec_seq, _ = x_sample.shape
    depth = w_in.shape[0]
    n_prompt, n_sample = batch * seq, dec_batch * dec_seq
    n = n_prompt + n_sample
    alpha = (2 * depth) ** 0.25
    assert seq % TM == 0 and n_sample % TM == 0 and dec_seq & (dec_seq - 1) == 0 and CHUNK % dec_seq == 0
    assert n_prompt % TM_R == 0 and n_sample % TM_R == 0
    assert all(seq % (dil * Q_BLK) == 0 and dil & (dil - 1) == 0 for _, dil in ATT_GROUPS)
    assert dec_batch % SEQ_BLK == 0
    n_tiles = (n * TOP_K) // TM_E + N_EXPERTS

    caches_t = [jnp.transpose(c, (0, 1, 3, 4, 2)).reshape(c.shape[0], c.shape[1], LANES, c.shape[2]) for c in
                (cache_k_w128, cache_v_w128, cache_k_w512, cache_v_w512, cache_k_w2048, cache_v_w2048)]
    bias_flat = _step_bias(rel_bias)
    wsp2, bsp2 = _spatial_tables(w_sp, b_sp, dec_seq)
    w_in_b, w_up_a_b, w_up_b_b, w_up_c_b, w_o_b = (w.astype(BF16) for w in (w_in, w_up_a, w_up_b, w_up_c, w_o))
    vec = lambda a: a.reshape(depth, 1, a.shape[-1])
    w_router_t = jnp.swapaxes(w_router, 1, 2)
    b_router_c = b_router.reshape(depth, N_EXPERTS, 1)
    expert_vec = lambda a: a.reshape(depth, N_EXPERTS, 1, a.shape[-1])

    x_p, x_s = x_prompt.reshape(n_prompt, D_MODEL), x_sample.reshape(n_sample, D_MODEL)
    outs ={name: [] for name in ("conv_p", "conv_s", "sgv", "kp", "vp", "ks", "vs")}
    for l in range(depth):
        s1 = jnp.zeros((dec_batch, dec_seq, CONV_DIM), F32).at[:, 0].set(state_conv[l, :, 1])
        s2 = jnp.zeros((dec_batch, dec_seq, CONV_DIM), F32).at[:, 0].set(state_conv[l, :, 0])
        s2 = s2.at[:, 1].set(state_conv[l, :, 1])
        q_p, k_p, v_p, q_s, k_s, v_s, mab, gc, tail, z_s, v_rows = _inproj_call(
            l, x_p, x_s, w_in_b, conv_w, vec(sg_ln_g), vec(sg_ln_b), wsp2, bsp2, w_up_a_b, w_up_b_b,
            s1.reshape(n_sample, CONV_DIM), s2.reshape(n_sample, CONV_DIM), seq, dec_seq)
        yc_p = _attn_prompt_call(q_p, k_p, v_p, bias_flat, batch, seq)
        yc_s = _attn_sample_call(l, q_s, k_s, v_s, caches_t, bias_flat, dec_batch, dec_seq)
        x1g, top_w, slot, tcnt, tbase = _merge_call(
            l, alpha, x_p, x_s, mab, gc, yc_p, yc_s, w_up_c_b, w_o_b, vec(ln1_g), vec(ln1_b), w_router_t, b_router_c)
        tile_expert, n_valid, weight_slot, tcnt, hstart, last_tile, has_rows = _route(
            tcnt[:, :, 0], tbase[:, :, 0], n_tiles)
        by_tile = lambda a: a.reshape(TOP_K, n // TM_R, TM_R).transpose(1, 2, 0).reshape(n // TM_R, 1, TM_R * TOP_K)
        slot, top_w = by_tile(slot), by_tile(top_w)
        xs = _dispatch_call(last_tile, has_rows, n_valid, tcnt, hstart, slot, x1g, n_tiles)
        ys = _moe_ffn_call(l, tile_expert, n_valid, weight_slot, xs, w_gate, expert_vec(b_gate),
                           w_up, expert_vec(b_up), w_down, expert_vec(b_down))
        x_p, x_s = _combine_call(l, alpha, tcnt, hstart, slot, top_w, x1g, ys, vec(ln2_g), vec(ln2_b), n_prompt)

        outs["conv_p"].append(tail[:, SUBLANES - (CONV_W - 1):])
        outs["conv_s"].append(z_s.reshape(dec_batch, dec_seq, CONV_DIM)[:, dec_seq - (CONV_W - 1):])
        outs["sgv"].append(v_rows.reshape(dec_batch, dec_seq, SG_DIM))
        kp4 = k_p.reshape(batch, seq, N_ATT, HPG, HEAD_DIM)
        vp4 = v_p.reshape(batch, seq, N_ATT, HPG, HEAD_DIM)
        ks4 = k_s.reshape(dec_batch, dec_seq, N_ATT, HPG, HEAD_DIM)
        vs4 = v_s.reshape(dec_batch, dec_seq, N_ATT, HPG, HEAD_DIM)
        outs["kp"].append([kp4[:, seq - min(win, seq):, gi] for gi, (win, _) in enumerate(ATT_GROUPS)])
        outs["vp"].append([vp4[:, seq - min(win, seq):, gi] for gi, (win, _) in enumerate(ATT_GROUPS)])
        outs["ks"].append([ks4[:, dec_seq - min(win, dec_seq):, gi] for gi, (win, _) in enumerate(ATT_GROUPS)])
        outs["vs"].append([vs4[:, dec_seq - min(win, dec_seq):, gi] for gi, (win, _) in enumerate(ATT_GROUPS)])

    stack = lambda name: jnp.stack(outs[name], axis=0)
    per_group = lambda name, gi: jnp.stack([layer[gi] for layer in outs[name]], axis=0)
    result = [x_p.reshape(batch, seq, D_MODEL), x_s.reshape(dec_batch, dec_seq, D_MODEL),
              stack("conv_p"), stack("conv_s")]
    for gi in range(N_ATT):
        result += [per_group("kp", gi), per_group("vp", gi)]
    for gi in range(N_ATT):
        result += [per_group("ks", gi), per_group("vs", gi)]
    result.append(stack("sgv"))
    return tuple(result)
```

```python
import functools
import math

import jax
import jax.numpy as jnp
from jax import lax
from jax.experimental import pallas as pl
from jax.experimental.pallas import tpu as pltpu

D_MODEL = 1024
CONV_DIM = 384
CONV_W = 3
SG_DIM = 256
SG_GROUPS = 4
SG_HEAD = SG_DIM // SG_GROUPS
CHUNK = 128
ATT_GROUPS = ((128, 1), (512, 4), (2048, 16))
N_ATT = len(ATT_GROUPS)
HPG = 4
HEAD_DIM = 32
ATT_DIM = HPG * N_ATT * HEAD_DIM
ATT_OUT = HPG * HEAD_DIM
N_KEYS = 129
NUM_BUCKETS = 32
MAX_DISTANCE = 2048
N_EXPERTS = 32
TOP_K = 4
SWIGLU_LIMIT = 7.0
SWIGLU_ALPHA = 1.702
LN_EPS = 1e-5
OFF_A = 0
OFF_B = OFF_A + 3 * CONV_DIM
OFF_C = OFF_B + 2 * SG_DIM
OFF_G = OFF_C + 3 * ATT_DIM
IN_DIM = OFF_G + 3 * D_MODEL
NEG = -1e30

LANES = 128
SUBLANES = 8
LANE_CHUNKS = D_MODEL // LANES
VMEM_LIMIT = 56 * 1024 * 1024

TM = 256
TM_R = 512
TM_E = 384
W_AHEAD = (2, 1, 0)
W_WARMUP = max(W_AHEAD)
W_SLOTS = W_WARMUP + 1
Q_BLK = 128
SEQ_BLK = 4
DMA_UNROLL = 8
BIG_RUN = 64

F32 = jnp.float32
BF16 = jnp.bfloat16


def _dot(a, b):
    return jnp.dot(a.astype(BF16), b.astype(BF16), preferred_element_type=F32)


def _dot_nt(a, b):
    return lax.dot_general(a.astype(BF16), b.astype(BF16), (((1,), (1,)), ((), ())),
                           preferred_element_type=F32)


def _layer_norm(x, g, b):
    mu = jnp.mean(x, axis=-1, keepdims=True)
    xc = x - mu
    var = jnp.mean(xc * xc, axis=-1, keepdims=True)
    return xc * lax.rsqrt(var + LN_EPS) * g + b


def _gelu(x):
    return 0.5 * x * (1.0 + lax.erf(x * (2.0 ** -0.5)))


def _sigmoid(x):
    return 1.0 / (1.0 + jnp.exp(-x))


def _params(n_axes):
    return pltpu.CompilerParams(dimension_semantics=("arbitrary",) * n_axes,
                                vmem_limit_bytes=VMEM_LIMIT)


def _inproj_kernel(n_ptiles, tiles_per_seq, dec_seq,
                   xp_ref, xs_ref, w_in_ref, convw_ref, lng_ref, lnb_ref, wsp_ref, bsp_ref, wua_ref, wub_ref,
                   s1_ref, s2_ref,
                   qp_ref, kp_ref, vp_ref, qs_ref, ks_ref, vs_ref, mab_ref, gc_ref, tail_ref, zs_ref, vrows_ref,
                   carry_ref):
    i = pl.program_id(0)
    is_prompt = i < n_ptiles

    @pl.when(jnp.logical_and(is_prompt, i % tiles_per_seq == 0))
    def _():
        carry_ref[...] = jnp.zeros_like(carry_ref)

    x = jnp.where(is_prompt, xp_ref[...], xs_ref[...]).astype(BF16)

    pa = jnp.dot(x, w_in_ref[:, OFF_A:OFF_B], preferred_element_type=F32)
    bg, cg, h = pa[:, :CONV_DIM], pa[:, CONV_DIM:2 * CONV_DIM], pa[:, 2 * CONV_DIM:]
    z = cg * h
    row = lax.broadcasted_iota(jnp.int32, (TM, CONV_DIM), 0)
    pos = jnp.where(is_prompt, row, row & (dec_seq - 1))
    prev1 = jnp.where(is_prompt, jnp.broadcast_to(carry_ref[7:8, :], (TM, CONV_DIM)), s1_ref[...])
    prev2 = jnp.where(is_prompt,
                      jnp.where(row == 0, jnp.broadcast_to(carry_ref[6:7, :], (TM, CONV_DIM)),
                                jnp.broadcast_to(carry_ref[7:8, :], (TM, CONV_DIM))),
                      s2_ref[...])
    z1 = jnp.where(pos >= 1, pltpu.roll(z, 1, axis=0), prev1)
    z2 = jnp.where(pos >= 2, pltpu.roll(z, 2, axis=0), prev2)
    cw = convw_ref[...]
    ya = bg * (cw[0:1, :] * z2 + cw[1:2, :] * z1 + cw[2:3, :] * z)

    @pl.when(is_prompt)
    def _():
        carry_ref[...] = z[TM - SUBLANES:, :]
        tail_ref[0] = z[TM - SUBLANES:, :]

    @pl.when(jnp.logical_not(is_prompt))
    def _():
        zs_ref[...] = z

    pb = jnp.dot(x, w_in_ref[:, OFF_B:OFF_C], preferred_element_type=F32)
    uv = _gelu(pb)
    u = uv[:, :SG_DIM]
    vn = _layer_norm(uv[:, SG_DIM:], lng_ref[...], lnb_ref[...])

    @pl.when(jnp.logical_not(is_prompt))
    def _():
        vrows_ref[...] = vn

    vb = vn.astype(BF16)
    low_half = lax.broadcasted_iota(jnp.int32, (CHUNK, LANES), 1) < SG_HEAD
    s_rows = []
    for c in range(TM // CHUNK):
        s_cols = []
        for p in range(SG_DIM // LANES):
            vcp = vb[c * CHUNK:(c + 1) * CHUNK, p * LANES:(p + 1) * LANES]
            sa = jnp.dot(wsp_ref[2 * p], vcp, preferred_element_type=F32)
            sb = jnp.dot(wsp_ref[2 * p + 1], vcp, preferred_element_type=F32)
            s_cols.append(jnp.where(low_half, sa, sb))
        s_rows.append(jnp.concatenate(s_cols, axis=1) + bsp_ref[...])
    yb = u * jnp.concatenate(s_rows, axis=0)

    pc = jnp.dot(x, w_in_ref[:, OFF_C:OFF_G], preferred_element_type=F32)
    for group_refs, active in (((qp_ref, kp_ref, vp_ref), is_prompt),
                               ((qs_ref, ks_ref, vs_ref), jnp.logical_not(is_prompt))):
        @pl.when(active)
        def _(group_refs=group_refs):
            for j, ref in enumerate(group_refs):
                ref[...] = pc[:, j * ATT_DIM:(j + 1) * ATT_DIM]

    ga = _sigmoid(jnp.dot(x, w_in_ref[:, OFF_G:OFF_G + D_MODEL], preferred_element_type=F32))
    mab = ga * _dot(ya, wua_ref[...])
    gb = _sigmoid(jnp.dot(x, w_in_ref[:, OFF_G + D_MODEL:OFF_G + 2 * D_MODEL], preferred_element_type=F32))
    mab_ref[...] = (mab + gb * _dot(yb, wub_ref[...])).astype(BF16)
    gc_ref[...] = _sigmoid(jnp.dot(x, w_in_ref[:, OFF_G + 2 * D_MODEL:], preferred_element_type=F32)).astype(BF16)


def _inproj_call(l, x_p, x_s, w_in, conv_w, sg_ln_g, sg_ln_b, wsp2, bsp2, w_up_a, w_up_b, s1, s2, seq, dec_seq):
    n_prompt, n_sample = x_p.shape[0], x_s.shape[0]
    n = n_prompt + n_sample
    n_ptiles = n_prompt // TM
    n_stiles = n_sample // TM
    tiles_per_seq = seq // TM
    batch = n_prompt // seq

    def stile(i):
        return jnp.maximum(i - n_ptiles, 0)

    def mode(i):
        return jnp.where(i < n_ptiles, 0, 1)

    row_spec = lambda w: pl.BlockSpec((TM, w), lambda i: (i, 0))
    srow_spec = lambda w: pl.BlockSpec((TM, w), lambda i: (stile(i), 0))
    prow_spec = lambda w: pl.BlockSpec((TM, w), lambda i: (jnp.minimum(i, n_ptiles - 1), 0))
    kern = functools.partial(_inproj_kernel, n_ptiles, tiles_per_seq, dec_seq)
    return pl.pallas_call(
        kern,
        grid=(n_ptiles + n_stiles,),
        in_specs=[
            prow_spec(D_MODEL), srow_spec(D_MODEL),
            pl.BlockSpec((None, D_MODEL, IN_DIM), lambda i: (l, 0, 0), pipeline_mode=pl.Buffered(1)),
            pl.BlockSpec((None, CONV_W, CONV_DIM), lambda i: (l, 0, 0)),
            pl.BlockSpec((None, 1, SG_DIM), lambda i: (l, 0, 0)),
            pl.BlockSpec((None, 1, SG_DIM), lambda i: (l, 0, 0)),
            pl.BlockSpec((None, None, SG_GROUPS, CHUNK, CHUNK), lambda i: (l, mode(i), 0, 0, 0)),
            pl.BlockSpec((None, None, CHUNK, SG_DIM), lambda i: (l, mode(i), 0, 0)),
            pl.BlockSpec((None, CONV_DIM, D_MODEL), lambda i: (l, 0, 0)),
            pl.BlockSpec((None, SG_DIM, D_MODEL), lambda i: (l, 0, 0)),
            srow_spec(CONV_DIM),
            srow_spec(CONV_DIM),
        ],
        out_specs=[
            prow_spec(ATT_DIM), prow_spec(ATT_DIM), prow_spec(ATT_DIM),
            srow_spec(ATT_DIM), srow_spec(ATT_DIM), srow_spec(ATT_DIM),
            row_spec(D_MODEL), row_spec(D_MODEL),
            pl.BlockSpec((1, SUBLANES, CONV_DIM),
                         lambda i: (jnp.minimum(i // tiles_per_seq, batch - 1), 0, 0)),
            srow_spec(CONV_DIM),
            srow_spec(SG_DIM),
        ],
        out_shape=[jax.ShapeDtypeStruct((n_prompt, ATT_DIM), F32)] * 3 + [jax.ShapeDtypeStruct((n_sample, ATT_DIM), F32)] * 3 + [
            jax.ShapeDtypeStruct((n, D_MODEL), BF16), jax.ShapeDtypeStruct((n, D_MODEL), BF16),
            jax.ShapeDtypeStruct((batch, SUBLANES, CONV_DIM), F32),
            jax.ShapeDtypeStruct((n_sample, CONV_DIM), F32),
            jax.ShapeDtypeStruct((n_sample, SG_DIM), F32),
        ],
        scratch_shapes=[pltpu.VMEM((SUBLANES, CONV_DIM), F32)],
        compiler_params=_params(1),
        name="inproj_mix",
    )(x_p, x_s, w_in, conv_w, sg_ln_g, sg_ln_b, wsp2, bsp2, w_up_a, w_up_b, s1, s2)


def _head_masks(rows):
    lane_head = lax.broadcasted_iota(jnp.int32, (rows, LANES), 1) // HEAD_DIM
    return [lane_head == h for h in range(HPG)]


def _stack_heads(qb, masks):
    return jnp.concatenate([jnp.where(m, qb, 0.0) for m in masks], axis=0)


def _unstack_heads(stacked, masks, r):
    out = None
    for h, m in enumerate(masks):
        part = jnp.where(m, stacked[h * r:(h + 1) * r], 0.0)
        out = part if out is None else out + part
    return out


def _mix_groups(outs, lses):
    mx = functools.reduce(jnp.maximum, lses)
    ws = [jnp.exp(ls - mx) for ls in lses]
    num = functools.reduce(lambda a, b: a + b, [w * o for w, o in zip(ws, outs)])
    return num / functools.reduce(lambda a, b: a + b, ws)


def _bias_table(rel_ref, start_ref, gi, h, steps):
    def body(b, tab):
        return jnp.where(steps >= start_ref[gi * NUM_BUCKETS + b], rel_ref[b * (N_ATT * HPG) + gi * HPG + h], tab)

    tab = lax.fori_loop(0, NUM_BUCKETS, body, jnp.zeros(steps.shape, F32))
    return jnp.where(jnp.logical_and(steps >= 0, steps < N_KEYS), tab, NEG)


def _bias_tables_kernel(dec_seq, pasts, rel_ref, start_ref, tp_ref, tc0, tc1, tc2, tn_ref):
    tcs = (tc0, tc1, tc2)
    qi = lax.broadcasted_iota(jnp.int32, (Q_BLK, 2 * Q_BLK), 0)
    kc = lax.broadcasted_iota(jnp.int32, (Q_BLK, 2 * Q_BLK), 1)
    for gi, (_, dil) in enumerate(ATT_GROUPS):
        past = pasts[gi]
        shift = dil.bit_length() - 1

        def steps(dist, dil=dil, shift=shift):
            return jnp.where((dist & (dil - 1)) == 0, dist >> shift, -1)

        qi_c = lax.broadcasted_iota(jnp.int32, (dec_seq, past), 0)
        row_c = lax.broadcasted_iota(jnp.int32, (dec_seq, past), 1)
        qi_n = lax.broadcasted_iota(jnp.int32, (dec_seq, dec_seq), 0)
        row_n = lax.broadcasted_iota(jnp.int32, (dec_seq, dec_seq), 1)
        for h in range(HPG):
            table = functools.partial(_bias_table, rel_ref, start_ref, gi, h)
            tp_ref[gi, h * Q_BLK:(h + 1) * Q_BLK, :] = table(qi + Q_BLK - kc)
            tcs[gi][h * dec_seq:(h + 1) * dec_seq, :] = table(steps(past + qi_c - row_c))
            tn_ref[gi, h * dec_seq:(h + 1) * dec_seq, :] = table(steps(qi_n - row_n))


def _bias_tables_call(rel_bias, pasts, dec_seq):
    shapes = ([(N_ATT, HPG * Q_BLK, 2 * Q_BLK)] + [(HPG * dec_seq, p) for p in pasts]
              + [(N_ATT, HPG * dec_seq, dec_seq)])
    return pl.pallas_call(
        functools.partial(_bias_tables_kernel, dec_seq, pasts),
        in_specs=[pl.BlockSpec(memory_space=pltpu.SMEM), pl.BlockSpec(memory_space=pltpu.SMEM)],
        out_shape=[jax.ShapeDtypeStruct(s, F32) for s in shapes],
        compiler_params=pltpu.CompilerParams(vmem_limit_bytes=VMEM_LIMIT),
        name="bias_tables",
    )(rel_bias.astype(F32).reshape(-1), _bucket_starts())


def _attn_prompt_kernel(seq, tab_ref, q_ref, k_ref, v_ref, o_ref, *scr):
    g = pl.program_id(1)
    o_scr, l_scr = scr[:N_ATT], scr[N_ATT:]
    masks = _head_masks(Q_BLK)
    scale = HEAD_DIM ** -0.5
    col = lax.broadcasted_iota(jnp.int32, (HPG * Q_BLK, 2 * Q_BLK), 1)

    def run_group(gi, dil):
        rows_per_class = seq // dil
        n_blk = rows_per_class // Q_BLK

        def ld(ref, start):
            if dil > 1:
                return ref[pl.ds(start, Q_BLK, stride=dil), :]
            return ref[pl.ds(start, Q_BLK), :]

        def st(ref, start, val):
            if dil > 1:
                ref[pl.ds(start, Q_BLK, stride=dil), :] = val
            else:
                ref[pl.ds(start, Q_BLK), :] = val

        def block(it, carry):
            r = it // n_blk
            mb = it % n_blk
            cur = r + dil * Q_BLK * mb
            prev = r + dil * Q_BLK * jnp.maximum(mb - 1, 0)
            qb = ld(q_ref, cur) * scale
            kw = jnp.concatenate([ld(k_ref, prev), ld(k_ref, cur)], axis=0)
            vw = jnp.concatenate([ld(v_ref, prev), ld(v_ref, cur)], axis=0)
            s = _dot_nt(_stack_heads(qb, masks), kw) + tab_ref[...]
            s = jnp.where(jnp.logical_or(col >= Q_BLK, mb > 0), s, NEG)
            m = jnp.max(s, axis=1, keepdims=True)
            p = jnp.exp(s - m)
            den = jnp.sum(p, axis=1, keepdims=True)
            pv = _dot(p, vw)
            st(o_scr[gi], cur, _unstack_heads(pv / den, masks, Q_BLK))
            st(l_scr[gi], cur, _unstack_heads(m + jnp.log(den), masks, Q_BLK))
            return carry

        lax.fori_loop(0, dil * n_blk, block, 0, unroll=4)

    for gi, (_, dil) in enumerate(ATT_GROUPS):
        @pl.when(g == gi)
        def _(gi=gi, dil=dil):
            run_group(gi, dil)

    @pl.when(g == N_ATT - 1)
    def _():
        o_ref[...] = _mix_groups([s[...] for s in o_scr], [s[...] for s in l_scr])


def _attn_prompt_call(q, k, v, table, batch, seq):
    blk = lambda: pl.BlockSpec((seq, LANES), lambda b, g: (b, g))
    return pl.pallas_call(
        functools.partial(_attn_prompt_kernel, seq),
        grid=(batch, N_ATT),
        in_specs=[pl.BlockSpec((None, HPG * Q_BLK, 2 * Q_BLK), lambda b, g: (g, 0, 0)), blk(), blk(), blk()],
        out_specs=pl.BlockSpec((seq, ATT_OUT), lambda b, g: (b, 0)),
        out_shape=jax.ShapeDtypeStruct((batch * seq, ATT_OUT), F32),
        scratch_shapes=[pltpu.VMEM((seq, LANES), F32)] * (2 * N_ATT),
        compiler_params=_params(2),
        name="attn_prompt",
    )(table, q, k, v)


def _attn_sample_kernel(dec_seq, q_ref, k_ref, v_ref, kc0, vc0, kc1, vc1, kc2, vc2, tc0, tc1, tc2, tn_ref, o_ref):
    kcs, vcs, tcs = (kc0, kc1, kc2), (vc0, vc1, vc2), (tc0, tc1, tc2)
    masks = _head_masks(dec_seq)
    scale = HEAD_DIM ** -0.5

    for s_i in range(SEQ_BLK):
        rows = slice(s_i * dec_seq, (s_i + 1) * dec_seq)
        outs, lses = [], []
        for gi in range(N_ATT):
            cols = slice(gi * LANES, (gi + 1) * LANES)
            qs = _stack_heads(q_ref[rows, cols] * scale, masks)
            k_new, v_new = k_ref[rows, cols], v_ref[rows, cols]
            s_c = _dot(qs, kcs[gi][s_i]) + tcs[gi][...]
            s_n = _dot_nt(qs, k_new) + tn_ref[gi]
            m = jnp.maximum(jnp.max(s_c, axis=1, keepdims=True), jnp.max(s_n, axis=1, keepdims=True))
            p_c = jnp.exp(s_c - m)
            p_n = jnp.exp(s_n - m)
            den = jnp.sum(p_c, axis=1, keepdims=True) + jnp.sum(p_n, axis=1, keepdims=True)
            pv = _dot_nt(p_c, vcs[gi][s_i]) + _dot(p_n, v_new)
            outs.append(_unstack_heads(pv / den, masks, dec_seq))
            lses.append(_unstack_heads(m + jnp.log(den), masks, dec_seq))
        o_ref[rows, :] = _mix_groups(outs, lses)


def _attn_sample_call(l, q, k, v, caches_t, tables, dec_batch, dec_seq):
    rows = SEQ_BLK * dec_seq
    new_spec = pl.BlockSpec((rows, ATT_DIM), lambda i: (i, 0))
    cache_specs = [pl.BlockSpec((None, SEQ_BLK, LANES, c.shape[3]), lambda i: (l, i, 0, 0)) for c in caches_t]
    table_specs = [pl.BlockSpec(t.shape, lambda i, nd=t.ndim: (0,) * nd) for t in tables]
    return pl.pallas_call(
        functools.partial(_attn_sample_kernel, dec_seq),
        grid=(dec_batch // SEQ_BLK,),
        in_specs=[new_spec, new_spec, new_spec] + cache_specs + table_specs,
        out_specs=pl.BlockSpec((rows, ATT_OUT), lambda i: (i, 0)),
        out_shape=jax.ShapeDtypeStruct((dec_batch * dec_seq, ATT_OUT), F32),
        compiler_params=_params(1),
        name="attn_sample",
    )(q, k, v, *caches_t, *tables)


def _merge_kernel(alpha, n_ptiles, xp_ref, xs_ref, mab_ref, gc_ref, ycp_ref, ycs_ref, wuc_ref, wo_ref, g_ref, b_ref,
                  wrt_ref, br_ref, x1g_ref, w_ref, slot_ref, tcnt_ref, tbase_ref, cnt_scr):
    @pl.when(pl.program_id(0) == 0)
    def _():
        cnt_scr[...] = jnp.zeros_like(cnt_scr)

    is_prompt = pl.program_id(0) < n_ptiles
    x = jnp.where(is_prompt, xp_ref[...], xs_ref[...])
    yc = jnp.where(is_prompt, ycp_ref[...], ycs_ref[...])
    merged = mab_ref[...].astype(F32) + gc_ref[...].astype(F32) * _dot(yc, wuc_ref[...])
    x1 = _layer_norm(alpha * x + _dot(merged, wo_ref[...]), g_ref[...], b_ref[...])
    for c in range(LANE_CHUNKS):
        x1g_ref[pl.ds(c, TM_R, stride=LANE_CHUNKS), :] = x1[:, c * LANES:(c + 1) * LANES]

    wr = wrt_ref[...]
    wr_hi, x1_hi = wr.astype(BF16), x1.astype(BF16)
    wr_lo, x1_lo = (wr - wr_hi.astype(F32)).astype(BF16), (x1 - x1_hi.astype(F32)).astype(BF16)
    logits = _dot_nt(wr_hi, x1_hi) + (_dot_nt(wr_lo, x1_hi) + _dot_nt(wr_hi, x1_lo)) + br_ref[...]
    eio = lax.broadcasted_iota(jnp.int32, (N_EXPERTS, TM_R), 0)
    vals, idxs = [], []
    for _ in range(TOP_K):
        mv = jnp.max(logits, axis=0, keepdims=True)
        ix = jnp.min(jnp.where(logits == mv, eio, N_EXPERTS), axis=0, keepdims=True)
        vals.append(mv)
        idxs.append(ix)
        logits = jnp.where(eio == ix, -jnp.inf, logits)
    tv = jnp.concatenate(vals, axis=0)
    e = jnp.exp(tv - tv[0:1, :])
    w_ref[...] = e / jnp.sum(e, axis=0, keepdims=True)

    onehots = [eio == ix for ix in idxs]
    chosen = functools.reduce(lambda a, c: a + c, [jnp.where(oh, 1.0, 0.0) for oh in onehots])
    earlier = (lax.broadcasted_iota(jnp.int32, (TM_R, TM_R), 0) < lax.broadcasted_iota(jnp.int32, (TM_R, TM_R), 1))
    within = _dot(chosen, jnp.where(earlier, 1.0, 0.0))
    tile_cnt = jnp.broadcast_to(jnp.sum(chosen, axis=1, keepdims=True), (N_EXPERTS, LANES))
    lower = (lax.broadcasted_iota(jnp.int32, (N_EXPERTS, N_EXPERTS), 1)
             < lax.broadcasted_iota(jnp.int32, (N_EXPERTS, N_EXPERTS), 0))
    place = within + jnp.dot(jnp.where(lower, 1.0, 0.0), tile_cnt, precision=lax.Precision.HIGHEST,
                             preferred_element_type=F32)[:, 0:1]
    slot_ref[...] = jnp.concatenate(
        [jnp.sum(jnp.where(oh, place, 0.0), axis=0, keepdims=True) for oh in onehots],
        axis=0).astype(jnp.int32) * LANE_CHUNKS
    tcnt_ref[...] = tile_cnt.astype(jnp.int32)
    tbase_ref[...] = cnt_scr[...].astype(jnp.int32)
    cnt_scr[...] = cnt_scr[...] + tile_cnt


def _merge_call(l, alpha, x_p, x_s, mab, gc, yc_p, yc_s, w_up_c, w_o, ln_g, ln_b, w_router_t, b_router):
    n_ptiles = x_p.shape[0] // TM_R
    n = x_p.shape[0] + x_s.shape[0]
    row_spec = lambda w: pl.BlockSpec((TM_R, w), lambda i: (i, 0))
    prow_spec = lambda w: pl.BlockSpec((TM_R, w), lambda i: (jnp.minimum(i, n_ptiles - 1), 0))
    srow_spec = lambda w: pl.BlockSpec((TM_R, w), lambda i: (jnp.maximum(i - n_ptiles, 0), 0))
    vec_spec = pl.BlockSpec((None, 1, D_MODEL), lambda i: (l, 0, 0))
    return pl.pallas_call(
        functools.partial(_merge_kernel, alpha, n_ptiles),
        grid=(n // TM_R,),
        in_specs=[
            prow_spec(D_MODEL), srow_spec(D_MODEL), row_spec(D_MODEL), row_spec(D_MODEL),
            prow_spec(ATT_OUT), srow_spec(ATT_OUT),
            pl.BlockSpec((None, ATT_OUT, D_MODEL), lambda i: (l, 0, 0)),
            pl.BlockSpec((None, D_MODEL, D_MODEL), lambda i: (l, 0, 0)),
            vec_spec, vec_spec,
            pl.BlockSpec((None, N_EXPERTS, D_MODEL), lambda i: (l, 0, 0)),
            pl.BlockSpec((None, N_EXPERTS, 1), lambda i: (l, 0, 0)),
        ],
        out_specs=[
            pl.BlockSpec((TM_R * LANE_CHUNKS, LANES), lambda i: (i, 0)),
            pl.BlockSpec((TOP_K, TM_R), lambda i: (0, i)),
            pl.BlockSpec((TOP_K, TM_R), lambda i: (0, i)),
            pl.BlockSpec((None, N_EXPERTS, LANES), lambda i: (i, 0, 0)),
            pl.BlockSpec((None, N_EXPERTS, LANES), lambda i: (i, 0, 0)),
        ],
        out_shape=[
            jax.ShapeDtypeStruct((n * LANE_CHUNKS, LANES), F32),
            jax.ShapeDtypeStruct((TOP_K, n), F32),
            jax.ShapeDtypeStruct((TOP_K, n), jnp.int32),
            jax.ShapeDtypeStruct((n // TM_R, N_EXPERTS, LANES), jnp.int32),
            jax.ShapeDtypeStruct((n // TM_R, N_EXPERTS, LANES), jnp.int32),
        ],
        scratch_shapes=[pltpu.VMEM((N_EXPERTS, LANES), F32)],
        compiler_params=_params(1),
        name="merge_ln1_router",
    )(x_p, x_s, mab, gc, yc_p, yc_s, w_up_c, w_o, ln_g, ln_b, w_router_t, b_router)


def _destride(ref, rows):
    return jnp.concatenate([ref[pl.ds(c, rows, stride=LANE_CHUNKS), :] for c in range(LANE_CHUNKS)], axis=1)


def _slab_rows(ref, row, count=1):
    return ref.at[pl.ds(pl.multiple_of(row, LANE_CHUNKS), count * LANE_CHUNKS), :]


def _start_expert_blocks(tile, tcnt_ref, hstart_ref, make_copy):
    big_bits = BIG_RUN.bit_length() - 1

    def per_expert(e, staged):
        cnt = tcnt_ref[tile * N_EXPERTS + e]
        first = hstart_ref[tile * N_EXPERTS + e]
        n_big = cnt >> big_bits

        def big(j, carry):
            make_copy(staged + j * BIG_RUN, first + j * BIG_RUN, BIG_RUN).start()
            return carry

        lax.fori_loop(0, n_big, big, 0)
        for bit in reversed(range(big_bits)):
            size = 1 << bit
            done = (cnt >> (bit + 1)) << (bit + 1)

            @pl.when((cnt & size) != 0)
            def _(size=size, done=done):
                make_copy(staged + done, first + done, size).start()
        return staged + cnt

    lax.fori_loop(0, N_EXPERTS, per_expert, 0)


def _dispatch_kernel(n_tiles, lt_ref, has_ref, nv_ref, tcnt_ref, hstart_ref, slot_ref, x1g_ref, xs_hbm,
                     zbuf, stg, zsem, sem):
    tile_rows = TM_E * LANE_CHUNKS
    i = pl.program_id(0)
    n_steps = pl.num_programs(0)
    buf = i % 2
    staged_all = lambda b: pltpu.make_async_copy(stg.at[b], xs_hbm.at[pl.ds(0, TM_R * TOP_K * LANE_CHUNKS), :],
                                                 sem.at[b])

    @pl.when(i == 0)
    def _():
        zbuf[...] = jnp.zeros_like(zbuf)
        zero_copy = lambda t: pltpu.make_async_copy(zbuf, xs_hbm.at[pl.ds(t * tile_rows, tile_rows), :], zsem)
        for e in range(N_EXPERTS):
            @pl.when(has_ref[e] > 0)
            def _(e=e):
                zero_copy(lt_ref[e]).start()
        lax.fori_loop(nv_ref[0], n_tiles, lambda t, c: (zero_copy(t).start(), c)[1], 0)
        for e in range(N_EXPERTS):
            @pl.when(has_ref[e] > 0)
            def _(e=e):
                zero_copy(lt_ref[e]).wait()
        lax.fori_loop(nv_ref[0], n_tiles, lambda t, c: (zero_copy(t).wait(), c)[1], 0)

    def step(b):
        @pl.when(i >= 2)
        def _():
            staged_all(b).wait()

        def place(t, carry):
            slab = _slab_rows(x1g_ref, t * LANE_CHUNKS)[...]
            for kk in range(TOP_K):
                _slab_rows(stg.at[b], slot_ref[0, 0, t * TOP_K + kk])[...] = slab
            return carry

        lax.fori_loop(0, TM_R, place, 0, unroll=DMA_UNROLL)
        _start_expert_blocks(i, tcnt_ref, hstart_ref, lambda staged, row, size: pltpu.make_async_copy(
            _slab_rows(stg.at[b], staged * LANE_CHUNKS, size), _slab_rows(xs_hbm, row * LANE_CHUNKS, size),
            sem.at[b]))

        @pl.when(i == n_steps - 1)
        def _():
            staged_all(b).wait()

        @pl.when(jnp.logical_and(i == n_steps - 1, i >= 1))
        def _():
            staged_all(1 - b).wait()

    for b in range(2):
        @pl.when(buf == b)
        def _(b=b):
            step(b)


def _dispatch_call(last_tile, has_rows, n_valid, tcnt, hstart, slot, x1g, n_tiles):
    n = slot.shape[0] * TM_R
    grid_spec = pltpu.PrefetchScalarGridSpec(
        num_scalar_prefetch=5,
        grid=(n // TM_R,),
        in_specs=[
            pl.BlockSpec((1, 1, TM_R * TOP_K), lambda i, *_: (i, 0, 0), memory_space=pltpu.SMEM),
            pl.BlockSpec((TM_R * LANE_CHUNKS, LANES), lambda i, *_: (i, 0)),
        ],
        out_specs=pl.BlockSpec(memory_space=pl.ANY),
        scratch_shapes=[pltpu.VMEM((TM_E * LANE_CHUNKS, LANES), F32),
                        pltpu.VMEM((2, TM_R * TOP_K * LANE_CHUNKS, LANES), F32),
                        pltpu.SemaphoreType.DMA(()), pltpu.SemaphoreType.DMA((2,))],
    )
    return pl.pallas_call(
        functools.partial(_dispatch_kernel, n_tiles),
        grid_spec=grid_spec,
        out_shape=jax.ShapeDtypeStruct((n_tiles * TM_E * LANE_CHUNKS, LANES), F32),
        compiler_params=_params(1),
        name="moe_dispatch",
    )(last_tile, has_rows, n_valid, tcnt, hstart, slot, x1g)


def _moe_ffn_kernel(te_ref, nv_ref, wslot_ref, xs_ref, wg_ref, bg_ref, wu_ref, bu_ref, wd_ref, bd_ref, ys_ref, w_b16):
    s = pl.program_id(0)
    t = s - W_WARMUP
    last = nv_ref[0] - 1
    expert_at = lambda tile: te_ref[jnp.clip(tile, 0, last)]

    for j, w_ref in enumerate((wg_ref, wu_ref, wd_ref)):
        arriving = expert_at(t + W_AHEAD[j])

        @pl.when(jnp.logical_or(s == 0, arriving != expert_at(t + W_AHEAD[j] - 1)))
        def _(j=j, w_ref=w_ref, arriving=arriving):
            w_b16[wslot_ref[arriving], j] = w_ref[...].astype(BF16)

    @pl.when(jnp.logical_and(t >= 0, t <= last))
    def _():
        slot = wslot_ref[te_ref[jnp.maximum(t, 0)]]
        x = _destride(xs_ref, TM_E).astype(BF16)
        gl = jnp.minimum(jnp.dot(x, w_b16[slot, 0], preferred_element_type=F32) + bg_ref[...], SWIGLU_LIMIT)
        ul = jnp.clip(jnp.dot(x, w_b16[slot, 1], preferred_element_type=F32) + bu_ref[...],
                      -SWIGLU_LIMIT, SWIGLU_LIMIT)
        hid = gl * _sigmoid(SWIGLU_ALPHA * gl) * (ul + 1.0)
        y = jnp.dot(hid.astype(BF16), w_b16[slot, 2], preferred_element_type=F32) + bd_ref[...]
        for c in range(LANE_CHUNKS):
            ys_ref[pl.ds(c, TM_E, stride=LANE_CHUNKS), :] = y[:, c * LANES:(c + 1) * LANES]

    @pl.when(t > last)
    def _():
        ys_ref[...] = jnp.zeros_like(ys_ref)


def _moe_ffn_call(l, tile_expert, n_valid, weight_slot, xs, w_gate, b_gate, w_up, b_up, w_down, b_down):
    n_tiles = tile_expert.shape[0]
    tile = lambda s, nv, ahead=0: jnp.clip(s - W_WARMUP + ahead, 0, nv[0] - 1)
    mat_spec = lambda ahead: pl.BlockSpec((None, None, D_MODEL, D_MODEL),
                                          lambda s, te, nv, ws: (l, te[tile(s, nv, ahead)], 0, 0))
    vec_spec = pl.BlockSpec((None, None, 1, D_MODEL), lambda s, te, nv, ws: (l, te[tile(s, nv)], 0, 0))
    grid_spec = pltpu.PrefetchScalarGridSpec(
        num_scalar_prefetch=3,
        grid=(n_tiles + W_WARMUP,),
        in_specs=[pl.BlockSpec((TM_E * LANE_CHUNKS, LANES), lambda s, te, nv, ws: (tile(s, nv), 0)),
                  mat_spec(W_AHEAD[0]), vec_spec, mat_spec(W_AHEAD[1]), vec_spec, mat_spec(W_AHEAD[2]), vec_spec],
        out_specs=pl.BlockSpec((TM_E * LANE_CHUNKS, LANES), lambda s, te, nv, ws: (jnp.maximum(s - W_WARMUP, 0), 0)),
        scratch_shapes=[pltpu.VMEM((W_SLOTS, 3, D_MODEL, D_MODEL), BF16)],
    )
    return pl.pallas_call(
        _moe_ffn_kernel,
        grid_spec=grid_spec,
        out_shape=jax.ShapeDtypeStruct((n_tiles * TM_E * LANE_CHUNKS, LANES), F32),
        compiler_params=_params(1),
        name="moe_ffn",
    )(tile_expert, n_valid, weight_slot, xs, w_gate, b_gate, w_up, b_up, w_down, b_down)


def _combine_kernel(alpha, n_ptiles, tcnt_ref, hstart_ref, slot_ref, w_ref, x1g_ref, ys_hbm, g_ref, b_ref,
                    x2p_ref, x2s_ref, stg, mixed, sem):
    i = pl.program_id(0)
    n_steps = pl.num_programs(0)
    buf = i % 2

    def fetch(tile, b):
        _start_expert_blocks(tile, tcnt_ref, hstart_ref, lambda staged, row, size: pltpu.make_async_copy(
            _slab_rows(ys_hbm, row * LANE_CHUNKS, size), _slab_rows(stg.at[b], staged * LANE_CHUNKS, size),
            sem.at[b]))

    def step(b):
        @pl.when(i == 0)
        def _():
            fetch(0, b)

        @pl.when(i + 1 < n_steps)
        def _():
            fetch(i + 1, 1 - b)

        pltpu.make_async_copy(ys_hbm.at[pl.ds(0, TM_R * TOP_K * LANE_CHUNKS), :], stg.at[b], sem.at[b]).wait()

        def mix(t, carry):
            c0 = t * TOP_K
            acc = w_ref[0, 0, c0] * _slab_rows(stg.at[b], slot_ref[0, 0, c0])[...]
            for kk in range(1, TOP_K):
                acc = acc + w_ref[0, 0, c0 + kk] * _slab_rows(stg.at[b], slot_ref[0, 0, c0 + kk])[...]
            _slab_rows(mixed, t * LANE_CHUNKS)[...] = acc
            return carry

        lax.fori_loop(0, TM_R, mix, 0, unroll=DMA_UNROLL)

    for b in range(2):
        @pl.when(buf == b)
        def _(b=b):
            step(b)

    x2 = _layer_norm(alpha * _destride(x1g_ref, TM_R) + _destride(mixed, TM_R), g_ref[...], b_ref[...])

    @pl.when(i < n_ptiles)
    def _():
        x2p_ref[...] = x2

    @pl.when(i >= n_ptiles)
    def _():
        x2s_ref[...] = x2


def _combine_call(l, alpha, tcnt, hstart, slot, top_w, x1g, ys, ln_g, ln_b, n_prompt):
    n = x1g.shape[0] // LANE_CHUNKS
    n_ptiles = n_prompt // TM_R
    vec_spec = pl.BlockSpec((None, 1, D_MODEL), lambda i, *_: (l, 0, 0))
    choice_spec = pl.BlockSpec((1, 1, TM_R * TOP_K), lambda i, *_: (i, 0, 0), memory_space=pltpu.SMEM)
    grid_spec = pltpu.PrefetchScalarGridSpec(
        num_scalar_prefetch=2,
        grid=(n // TM_R,),
        in_specs=[
            choice_spec, choice_spec,
            pl.BlockSpec((TM_R * LANE_CHUNKS, LANES), lambda i, *_: (i, 0)),
            pl.BlockSpec(memory_space=pl.ANY),
            vec_spec, vec_spec,
        ],
        out_specs=[pl.BlockSpec((TM_R, D_MODEL), lambda i, *_: (jnp.minimum(i, n_ptiles - 1), 0)),
                   pl.BlockSpec((TM_R, D_MODEL), lambda i, *_: (jnp.maximum(i - n_ptiles, 0), 0))],
        scratch_shapes=[pltpu.VMEM((2, TM_R * TOP_K * LANE_CHUNKS, LANES), F32),
                        pltpu.VMEM((TM_R * LANE_CHUNKS, LANES), F32),
                        pltpu.SemaphoreType.DMA((2,))],
    )
    return pl.pallas_call(
        functools.partial(_combine_kernel, alpha, n_ptiles),
        grid_spec=grid_spec,
        out_shape=[jax.ShapeDtypeStruct((n_prompt, D_MODEL), F32),
                   jax.ShapeDtypeStruct((n - n_prompt, D_MODEL), F32)],
        compiler_params=_params(1),
        name="moe_combine_ln2",
    )(tcnt, hstart, slot, top_w, x1g, ys, ln_g, ln_b)


def _route(tcnt, tbase, n_tiles):
    experts = jnp.arange(N_EXPERTS, dtype=jnp.int32)
    counts = tbase[-1] + tcnt[-1]
    padded = ((counts + TM_E - 1) // TM_E) * TM_E
    ends = jnp.cumsum(padded)
    hstart = (ends - padded)[None, :] + tbase
    n_valid = ends[-1] // TM_E
    last_expert = jnp.max(jnp.where(counts > 0, experts, 0))
    tile_start = jnp.arange(n_tiles, dtype=jnp.int32) * TM_E
    tile_expert = jnp.minimum(jnp.sum(ends[None, :] <= tile_start[:, None], axis=1), last_expert)
    has = (counts > 0).astype(jnp.int32)
    weight_slot = (jnp.cumsum(has) - has) % W_SLOTS
    return (tile_expert.astype(jnp.int32), n_valid.astype(jnp.int32).reshape(1), weight_slot.astype(jnp.int32),
            tcnt.reshape(-1), hstart.astype(jnp.int32).reshape(-1), (ends // TM_E - 1).astype(jnp.int32), has)


def _t5_bucket(dist):
    max_exact = NUM_BUCKETS // 2
    distf = jnp.maximum(dist, 1).astype(F32)
    large = max_exact + (jnp.log(distf / max_exact) / math.log(MAX_DISTANCE / max_exact)
                         * (NUM_BUCKETS - max_exact)).astype(jnp.int32)
    large = jnp.minimum(large, NUM_BUCKETS - 1)
    return jnp.where(dist < max_exact, dist, large)


def _bucket_starts():
    j = jnp.arange(N_KEYS, dtype=jnp.int32)
    b = jnp.arange(NUM_BUCKETS, dtype=jnp.int32)
    return jnp.stack([jnp.sum(_t5_bucket(dil * j)[None, :] < b[:, None], axis=1) for _, dil in ATT_GROUPS],
                     axis=0).astype(jnp.int32).reshape(-1)


def _spatial_tables(w_sp, b_sp, dec_seq):
    depth = w_sp.shape[0]
    tril = jnp.tril(jnp.ones((CHUNK, CHUNK), bool))
    wp = jnp.where(tril, w_sp, 0.0)
    ws_small = jnp.where(tril[:dec_seq, :dec_seq], w_sp[:, :, :dec_seq, :dec_seq], 0.0)
    eye = jnp.eye(CHUNK // dec_seq, dtype=w_sp.dtype)
    ws = jnp.einsum("ab,lgij->lgaibj", eye, ws_small).reshape(depth, SG_GROUPS, CHUNK, CHUNK)
    bp = jnp.repeat(jnp.swapaxes(b_sp, 1, 2), SG_HEAD, axis=2)
    bs = jnp.tile(bp[:, :dec_seq], (1, CHUNK // dec_seq, 1))
    return jnp.stack([wp, ws], axis=1).astype(BF16), jnp.stack([bp, bs], axis=1)


def kernel(x_prompt, x_sample, state_conv, cache_k_w128, cache_v_w128, cache_k_w512, cache_v_w512,
           cache_k_w2048, cache_v_w2048, w_in, conv_w, sg_ln_g, sg_ln_b, w_sp, b_sp, rel_bias,
           w_up_a, w_up_b, w_up_c, w_o, ln1_g, ln1_b, w_router, b_router, w_gate, b_gate, w_up, b_up,
           w_down, b_down, ln2_g, ln2_b):
    batch, seq, _ = x_prompt.shape
    dec_batch, dec_seq, _ = x_sample.shape
    depth = w_in.shape[0]
    n_prompt, n_sample = batch * seq, dec_batch * dec_seq
    n = n_prompt + n_sample
    alpha = (2 * depth) ** 0.25
    assert seq % TM == 0 and n_sample % TM == 0 and dec_seq & (dec_seq - 1) == 0 and CHUNK % dec_seq == 0
    assert n_prompt % TM_R == 0 and n_sample % TM_R == 0
    assert all(seq % (dil * Q_BLK) == 0 and dil & (dil - 1) == 0 for _, dil in ATT_GROUPS)
    assert dec_batch % SEQ_BLK == 0
    n_tiles = (n * TOP_K) // TM_E + N_EXPERTS

    caches_t = [jnp.transpose(c, (0, 1, 3, 4, 2)).reshape(c.shape[0], c.shape[1], LANES, c.shape[2]) for c in
                (cache_k_w128, cache_v_w128, cache_k_w512, cache_v_w512, cache_k_w2048, cache_v_w2048)]
    pasts = tuple(caches_t[2 * gi].shape[3] for gi in range(N_ATT))
    prompt_table, *sample_tables = _bias_tables_call(rel_bias, pasts, dec_seq)
    wsp2, bsp2 = _spatial_tables(w_sp, b_sp, dec_seq)
    w_in_b, w_up_a_b, w_up_b_b, w_up_c_b, w_o_b = (w.astype(BF16) for w in (w_in, w_up_a, w_up_b, w_up_c, w_o))
    vec = lambda a: a.reshape(depth, 1, a.shape[-1])
    w_router_t = jnp.swapaxes(w_router, 1, 2)
    b_router_c = b_router.reshape(depth, N_EXPERTS, 1)
    expert_vec = lambda a: a.reshape(depth, N_EXPERTS, 1, a.shape[-1])

    x_p, x_s = x_prompt.reshape(n_prompt, D_MODEL), x_sample.reshape(n_sample, D_MODEL)
    outs ={name: [] for name in ("conv_p", "conv_s", "sgv", "kp", "vp", "ks", "vs")}
    for l in range(depth):
        s1 = jnp.zeros((dec_batch, dec_seq, CONV_DIM), F32).at[:, 0].set(state_conv[l, :, 1])
        s2 = jnp.zeros((dec_batch, dec_seq, CONV_DIM), F32).at[:, 0].set(state_conv[l, :, 0])
        s2 = s2.at[:, 1].set(state_conv[l, :, 1])
        q_p, k_p, v_p, q_s, k_s, v_s, mab, gc, tail, z_s, v_rows = _inproj_call(
            l, x_p, x_s, w_in_b, conv_w, vec(sg_ln_g), vec(sg_ln_b), wsp2, bsp2, w_up_a_b, w_up_b_b,
            s1.reshape(n_sample, CONV_DIM), s2.reshape(n_sample, CONV_DIM), seq, dec_seq)
        yc_p = _attn_prompt_call(q_p, k_p, v_p, prompt_table, batch, seq)
        yc_s = _attn_sample_call(l, q_s, k_s, v_s, caches_t, sample_tables, dec_batch, dec_seq)
        x1g, top_w, slot, tcnt, tbase = _merge_call(
            l, alpha, x_p, x_s, mab, gc, yc_p, yc_s, w_up_c_b, w_o_b, vec(ln1_g), vec(ln1_b), w_router_t, b_router_c)
        tile_expert, n_valid, weight_slot, tcnt, hstart, last_tile, has_rows = _route(
            tcnt[:, :, 0], tbase[:, :, 0], n_tiles)
        by_tile = lambda a: a.reshape(TOP_K, n // TM_R, TM_R).transpose(1, 2, 0).reshape(n // TM_R, 1, TM_R * TOP_K)
        slot, top_w = by_tile(slot), by_tile(top_w)
        xs = _dispatch_call(last_tile, has_rows, n_valid, tcnt, hstart, slot, x1g, n_tiles)
        ys = _moe_ffn_call(l, tile_expert, n_valid, weight_slot, xs, w_gate, expert_vec(b_gate),
                           w_up, expert_vec(b_up), w_down, expert_vec(b_down))
        x_p, x_s = _combine_call(l, alpha, tcnt, hstart, slot, top_w, x1g, ys, vec(ln2_g), vec(ln2_b), n_prompt)

        outs["conv_p"].append(tail[:, SUBLANES - (CONV_W - 1):])
        outs["conv_s"].append(z_s.reshape(dec_batch, dec_seq, CONV_DIM)[:, dec_seq - (CONV_W - 1):])
        outs["sgv"].append(v_rows.reshape(dec_batch, dec_seq, SG_DIM))
        kp4 = k_p.reshape(batch, seq, N_ATT, HPG, HEAD_DIM)
        vp4 = v_p.reshape(batch, seq, N_ATT, HPG, HEAD_DIM)
        ks4 = k_s.reshape(dec_batch, dec_seq, N_ATT, HPG, HEAD_DIM)
        vs4 = v_s.reshape(dec_batch, dec_seq, N_ATT, HPG, HEAD_DIM)
        outs["kp"].append([kp4[:, seq - min(win, seq):, gi] for gi, (win, _) in enumerate(ATT_GROUPS)])
        outs["vp"].append([vp4[:, seq - min(win, seq):, gi] for gi, (win, _) in enumerate(ATT_GROUPS)])
        outs["ks"].append([ks4[:, dec_seq - min(win, dec_seq):, gi] for gi, (win, _) in enumerate(ATT_GROUPS)])
        outs["vs"].append([vs4[:, dec_seq - min(win, dec_seq):, gi] for gi, (win, _) in enumerate(ATT_GROUPS)])

    stack = lambda name: jnp.stack(outs[name], axis=0)
    per_group = lambda name, gi: jnp.stack([layer[gi] for layer in outs[name]], axis=0)
    result = [x_p.reshape(batch, seq, D_MODEL), x_s.reshape(dec_batch, dec_seq, D_MODEL),
              stack("conv_p"), stack("conv_s")]
    for gi in range(N_ATT):
        result += [per_group("kp", gi), per_group("vp", gi)]
    for gi in range(N_ATT):
        result += [per_group("ks", gi), per_group("vs", gi)]
    result.append(stack("sgv"))
    return tuple(result)
```

```python
import functools
import math

import jax
import jax.numpy as jnp
from jax import lax
from jax.experimental import pallas as pl
from jax.experimental.pallas import tpu as pltpu

D_MODEL = 1024
CONV_DIM = 384
CONV_W = 3
SG_DIM = 256
SG_GROUPS = 4
SG_HEAD = SG_DIM // SG_GROUPS
CHUNK = 128
ATT_GROUPS = ((128, 1), (512, 4), (2048, 16))
N_ATT = len(ATT_GROUPS)
HPG = 4
HEAD_DIM = 32
ATT_DIM = HPG * N_ATT * HEAD_DIM
ATT_OUT = HPG * HEAD_DIM
N_KEYS = 129
NUM_BUCKETS = 32
MAX_DISTANCE = 2048
N_EXPERTS = 32
TOP_K = 4
SWIGLU_LIMIT = 7.0
SWIGLU_ALPHA = 1.702
LN_EPS = 1e-5
OFF_A = 0
OFF_B = OFF_A + 3 * CONV_DIM
OFF_C = OFF_B + 2 * SG_DIM
OFF_G = OFF_C + 3 * ATT_DIM
IN_DIM = OFF_G + 3 * D_MODEL
NEG = -1e30

LANES = 128
SUBLANES = 8
LANE_CHUNKS = D_MODEL // LANES
VMEM_LIMIT = 56 * 1024 * 1024

TM = 256
TM_R = 512
TM_E = 448
W_AHEAD = (2, 1, 0)
W_WARMUP = max(W_AHEAD)
W_SLOTS = W_WARMUP + 1
Q_BLK = 128
SEQ_BLK = 4
DMA_UNROLL = 8
BIG_RUN = 64

F32 = jnp.float32
BF16 = jnp.bfloat16


def _dot(a, b):
    return jnp.dot(a.astype(BF16), b.astype(BF16), preferred_element_type=F32)


def _dot_nt(a, b):
    return lax.dot_general(a.astype(BF16), b.astype(BF16), (((1,), (1,)), ((), ())),
                           preferred_element_type=F32)


def _layer_norm(x, g, b):
    mu = jnp.mean(x, axis=-1, keepdims=True)
    xc = x - mu
    var = jnp.mean(xc * xc, axis=-1, keepdims=True)
    return xc * lax.rsqrt(var + LN_EPS) * g + b


def _gelu(x):
    return 0.5 * x * (1.0 + lax.erf(x * (2.0 ** -0.5)))


def _sigmoid(x):
    return 1.0 / (1.0 + jnp.exp(-x))


def _params(n_axes):
    return pltpu.CompilerParams(dimension_semantics=("arbitrary",) * n_axes,
                                vmem_limit_bytes=VMEM_LIMIT)


def _inproj_kernel(n_ptiles, tiles_per_seq, dec_seq,
                   xp_ref, xs_ref, w_in_ref, convw_ref, lng_ref, lnb_ref, wsp_ref, bsp_ref, wua_ref, wub_ref,
                   s1_ref, s2_ref,
                   qp_ref, kp_ref, vp_ref, qs_ref, ks_ref, vs_ref, mab_ref, gc_ref, tail_ref, zs_ref, vrows_ref,
                   carry_ref):
    i = pl.program_id(0)
    is_prompt = i < n_ptiles

    @pl.when(jnp.logical_and(is_prompt, i % tiles_per_seq == 0))
    def _():
        carry_ref[...] = jnp.zeros_like(carry_ref)

    x = jnp.where(is_prompt, xp_ref[...], xs_ref[...]).astype(BF16)

    pa = jnp.dot(x, w_in_ref[:, OFF_A:OFF_B], preferred_element_type=F32)
    bg, cg, h = pa[:, :CONV_DIM], pa[:, CONV_DIM:2 * CONV_DIM], pa[:, 2 * CONV_DIM:]
    z = cg * h
    row = lax.broadcasted_iota(jnp.int32, (TM, CONV_DIM), 0)
    pos = jnp.where(is_prompt, row, row & (dec_seq - 1))
    prev1 = jnp.where(is_prompt, jnp.broadcast_to(carry_ref[7:8, :], (TM, CONV_DIM)), s1_ref[...])
    prev2 = jnp.where(is_prompt,
                      jnp.where(row == 0, jnp.broadcast_to(carry_ref[6:7, :], (TM, CONV_DIM)),
                                jnp.broadcast_to(carry_ref[7:8, :], (TM, CONV_DIM))),
                      s2_ref[...])
    z1 = jnp.where(pos >= 1, pltpu.roll(z, 1, axis=0), prev1)
    z2 = jnp.where(pos >= 2, pltpu.roll(z, 2, axis=0), prev2)
    cw = convw_ref[...]
    ya = bg * (cw[0:1, :] * z2 + cw[1:2, :] * z1 + cw[2:3, :] * z)

    @pl.when(is_prompt)
    def _():
        carry_ref[...] = z[TM - SUBLANES:, :]
        tail_ref[0] = z[TM - SUBLANES:, :]

    @pl.when(jnp.logical_not(is_prompt))
    def _():
        zs_ref[...] = z

    pb = jnp.dot(x, w_in_ref[:, OFF_B:OFF_C], preferred_element_type=F32)
    uv = _gelu(pb)
    u = uv[:, :SG_DIM]
    vn = _layer_norm(uv[:, SG_DIM:], lng_ref[...], lnb_ref[...])

    @pl.when(jnp.logical_not(is_prompt))
    def _():
        vrows_ref[...] = vn

    vb = vn.astype(BF16)
    low_half = lax.broadcasted_iota(jnp.int32, (CHUNK, LANES), 1) < SG_HEAD
    s_rows = []
    for c in range(TM // CHUNK):
        s_cols = []
        for p in range(SG_DIM // LANES):
            vcp = vb[c * CHUNK:(c + 1) * CHUNK, p * LANES:(p + 1) * LANES]
            sa = jnp.dot(wsp_ref[2 * p], vcp, preferred_element_type=F32)
            sb = jnp.dot(wsp_ref[2 * p + 1], vcp, preferred_element_type=F32)
            s_cols.append(jnp.where(low_half, sa, sb))
        s_rows.append(jnp.concatenate(s_cols, axis=1) + bsp_ref[...])
    yb = u * jnp.concatenate(s_rows, axis=0)

    pc = jnp.dot(x, w_in_ref[:, OFF_C:OFF_G], preferred_element_type=F32)
    for group_refs, active in (((qp_ref, kp_ref, vp_ref), is_prompt),
                               ((qs_ref, ks_ref, vs_ref), jnp.logical_not(is_prompt))):
        @pl.when(active)
        def _(group_refs=group_refs):
            for j, ref in enumerate(group_refs):
                ref[...] = pc[:, j * ATT_DIM:(j + 1) * ATT_DIM]

    ga = _sigmoid(jnp.dot(x, w_in_ref[:, OFF_G:OFF_G + D_MODEL], preferred_element_type=F32))
    mab = ga * _dot(ya, wua_ref[...])
    gb = _sigmoid(jnp.dot(x, w_in_ref[:, OFF_G + D_MODEL:OFF_G + 2 * D_MODEL], preferred_element_type=F32))
    mab_ref[...] = (mab + gb * _dot(yb, wub_ref[...])).astype(BF16)
    gc_ref[...] = _sigmoid(jnp.dot(x, w_in_ref[:, OFF_G + 2 * D_MODEL:], preferred_element_type=F32)).astype(BF16)


def _inproj_call(l, x_p, x_s, w_in, conv_w, sg_ln_g, sg_ln_b, wsp2, bsp2, w_up_a, w_up_b, s1, s2, seq, dec_seq):
    n_prompt, n_sample = x_p.shape[0], x_s.shape[0]
    n = n_prompt + n_sample
    n_ptiles = n_prompt // TM
    n_stiles = n_sample // TM
    tiles_per_seq = seq // TM
    batch = n_prompt // seq

    def stile(i):
        return jnp.maximum(i - n_ptiles, 0)

    def mode(i):
        return jnp.where(i < n_ptiles, 0, 1)

    row_spec = lambda w: pl.BlockSpec((TM, w), lambda i: (i, 0))
    srow_spec = lambda w: pl.BlockSpec((TM, w), lambda i: (stile(i), 0))
    prow_spec = lambda w: pl.BlockSpec((TM, w), lambda i: (jnp.minimum(i, n_ptiles - 1), 0))
    kern = functools.partial(_inproj_kernel, n_ptiles, tiles_per_seq, dec_seq)
    return pl.pallas_call(
        kern,
        grid=(n_ptiles + n_stiles,),
        in_specs=[
            prow_spec(D_MODEL), srow_spec(D_MODEL),
            pl.BlockSpec((None, D_MODEL, IN_DIM), lambda i: (l, 0, 0), pipeline_mode=pl.Buffered(1)),
            pl.BlockSpec((None, CONV_W, CONV_DIM), lambda i: (l, 0, 0)),
            pl.BlockSpec((None, 1, SG_DIM), lambda i: (l, 0, 0)),
            pl.BlockSpec((None, 1, SG_DIM), lambda i: (l, 0, 0)),
            pl.BlockSpec((None, None, SG_GROUPS, CHUNK, CHUNK), lambda i: (l, mode(i), 0, 0, 0)),
            pl.BlockSpec((None, None, CHUNK, SG_DIM), lambda i: (l, mode(i), 0, 0)),
            pl.BlockSpec((None, CONV_DIM, D_MODEL), lambda i: (l, 0, 0)),
            pl.BlockSpec((None, SG_DIM, D_MODEL), lambda i: (l, 0, 0)),
            srow_spec(CONV_DIM),
            srow_spec(CONV_DIM),
        ],
        out_specs=[
            prow_spec(ATT_DIM), prow_spec(ATT_DIM), prow_spec(ATT_DIM),
            srow_spec(ATT_DIM), srow_spec(ATT_DIM), srow_spec(ATT_DIM),
            row_spec(D_MODEL), row_spec(D_MODEL),
            pl.BlockSpec((1, SUBLANES, CONV_DIM),
                         lambda i: (jnp.minimum(i // tiles_per_seq, batch - 1), 0, 0)),
            srow_spec(CONV_DIM),
            srow_spec(SG_DIM),
        ],
        out_shape=[jax.ShapeDtypeStruct((n_prompt, ATT_DIM), F32)] * 3 + [jax.ShapeDtypeStruct((n_sample, ATT_DIM), F32)] * 3 + [
            jax.ShapeDtypeStruct((n, D_MODEL), BF16), jax.ShapeDtypeStruct((n, D_MODEL), BF16),
            jax.ShapeDtypeStruct((batch, SUBLANES, CONV_DIM), F32),
            jax.ShapeDtypeStruct((n_sample, CONV_DIM), F32),
            jax.ShapeDtypeStruct((n_sample, SG_DIM), F32),
        ],
        scratch_shapes=[pltpu.VMEM((SUBLANES, CONV_DIM), F32)],
        compiler_params=_params(1),
        name="inproj_mix",
    )(x_p, x_s, w_in, conv_w, sg_ln_g, sg_ln_b, wsp2, bsp2, w_up_a, w_up_b, s1, s2)


def _head_masks(rows):
    lane_head = lax.broadcasted_iota(jnp.int32, (rows, LANES), 1) // HEAD_DIM
    return [lane_head == h for h in range(HPG)]


def _stack_heads(qb, masks):
    return jnp.concatenate([jnp.where(m, qb, 0.0) for m in masks], axis=0)


def _unstack_heads(stacked, masks, r):
    out = None
    for h, m in enumerate(masks):
        part = jnp.where(m, stacked[h * r:(h + 1) * r], 0.0)
        out = part if out is None else out + part
    return out


def _mix_groups(outs, lses):
    mx = functools.reduce(jnp.maximum, lses)
    ws = [jnp.exp(ls - mx) for ls in lses]
    num = functools.reduce(lambda a, b: a + b, [w * o for w, o in zip(ws, outs)])
    return num / functools.reduce(lambda a, b: a + b, ws)


def _bias_table(rel_ref, start_ref, gi, h, steps):
    def body(b, tab):
        return jnp.where(steps >= start_ref[gi * NUM_BUCKETS + b], rel_ref[b * (N_ATT * HPG) + gi * HPG + h], tab)

    tab = lax.fori_loop(0, NUM_BUCKETS, body, jnp.zeros(steps.shape, F32))
    return jnp.where(jnp.logical_and(steps >= 0, steps < N_KEYS), tab, NEG)


def _bias_tables_kernel(dec_seq, pasts, rel_ref, start_ref, tp_ref, tc0, tc1, tc2, tn_ref):
    tcs = (tc0, tc1, tc2)
    qi = lax.broadcasted_iota(jnp.int32, (Q_BLK, 2 * Q_BLK), 0)
    kc = lax.broadcasted_iota(jnp.int32, (Q_BLK, 2 * Q_BLK), 1)
    for gi, (_, dil) in enumerate(ATT_GROUPS):
        past = pasts[gi]
        shift = dil.bit_length() - 1

        def steps(dist, dil=dil, shift=shift):
            return jnp.where((dist & (dil - 1)) == 0, dist >> shift, -1)

        qi_c = lax.broadcasted_iota(jnp.int32, (dec_seq, past), 0)
        row_c = lax.broadcasted_iota(jnp.int32, (dec_seq, past), 1)
        qi_n = lax.broadcasted_iota(jnp.int32, (dec_seq, dec_seq), 0)
        row_n = lax.broadcasted_iota(jnp.int32, (dec_seq, dec_seq), 1)
        for h in range(HPG):
            table = functools.partial(_bias_table, rel_ref, start_ref, gi, h)
            tp_ref[gi, h * Q_BLK:(h + 1) * Q_BLK, :] = table(qi + Q_BLK - kc)
            tcs[gi][h * dec_seq:(h + 1) * dec_seq, :] = table(steps(past + qi_c - row_c))
            tn_ref[gi, h * dec_seq:(h + 1) * dec_seq, :] = table(steps(qi_n - row_n))


def _bias_tables_call(rel_bias, pasts, dec_seq):
    shapes = ([(N_ATT, HPG * Q_BLK, 2 * Q_BLK)] + [(HPG * dec_seq, p) for p in pasts]
              + [(N_ATT, HPG * dec_seq, dec_seq)])
    return pl.pallas_call(
        functools.partial(_bias_tables_kernel, dec_seq, pasts),
        in_specs=[pl.BlockSpec(memory_space=pltpu.SMEM), pl.BlockSpec(memory_space=pltpu.SMEM)],
        out_shape=[jax.ShapeDtypeStruct(s, F32) for s in shapes],
        compiler_params=pltpu.CompilerParams(vmem_limit_bytes=VMEM_LIMIT),
        name="bias_tables",
    )(rel_bias.astype(F32).reshape(-1), _bucket_starts())


def _attn_prompt_kernel(seq, tab_ref, q_ref, k_ref, v_ref, o_ref, *scr):
    g = pl.program_id(1)
    o_scr, l_scr = scr[:N_ATT], scr[N_ATT:]
    masks = _head_masks(Q_BLK)
    scale = HEAD_DIM ** -0.5
    col = lax.broadcasted_iota(jnp.int32, (HPG * Q_BLK, 2 * Q_BLK), 1)

    def run_group(gi, dil):
        rows_per_class = seq // dil
        n_blk = rows_per_class // Q_BLK

        def ld(ref, start):
            if dil > 1:
                return ref[pl.ds(start, Q_BLK, stride=dil), :]
            return ref[pl.ds(start, Q_BLK), :]

        def st(ref, start, val):
            if dil > 1:
                ref[pl.ds(start, Q_BLK, stride=dil), :] = val
            else:
                ref[pl.ds(start, Q_BLK), :] = val

        def block(it, carry):
            r = it // n_blk
            mb = it % n_blk
            cur = r + dil * Q_BLK * mb
            prev = r + dil * Q_BLK * jnp.maximum(mb - 1, 0)
            qb = ld(q_ref, cur) * scale
            kw = jnp.concatenate([ld(k_ref, prev), ld(k_ref, cur)], axis=0)
            vw = jnp.concatenate([ld(v_ref, prev), ld(v_ref, cur)], axis=0)
            s = _dot_nt(_stack_heads(qb, masks), kw) + tab_ref[...]
            s = jnp.where(jnp.logical_or(col >= Q_BLK, mb > 0), s, NEG)
            m = jnp.max(s, axis=1, keepdims=True)
            p = jnp.exp(s - m)
            den = jnp.sum(p, axis=1, keepdims=True)
            pv = _dot(p, vw)
            st(o_scr[gi], cur, _unstack_heads(pv / den, masks, Q_BLK))
            st(l_scr[gi], cur, _unstack_heads(m + jnp.log(den), masks, Q_BLK))
            return carry

        lax.fori_loop(0, dil * n_blk, block, 0, unroll=4)

    for gi, (_, dil) in enumerate(ATT_GROUPS):
        @pl.when(g == gi)
        def _(gi=gi, dil=dil):
            run_group(gi, dil)

    @pl.when(g == N_ATT - 1)
    def _():
        o_ref[...] = _mix_groups([s[...] for s in o_scr], [s[...] for s in l_scr])


def _attn_prompt_call(q, k, v, table, batch, seq):
    blk = lambda: pl.BlockSpec((seq, LANES), lambda b, g: (b, g))
    return pl.pallas_call(
        functools.partial(_attn_prompt_kernel, seq),
        grid=(batch, N_ATT),
        in_specs=[pl.BlockSpec((None, HPG * Q_BLK, 2 * Q_BLK), lambda b, g: (g, 0, 0)), blk(), blk(), blk()],
        out_specs=pl.BlockSpec((seq, ATT_OUT), lambda b, g: (b, 0)),
        out_shape=jax.ShapeDtypeStruct((batch * seq, ATT_OUT), F32),
        scratch_shapes=[pltpu.VMEM((seq, LANES), F32)] * (2 * N_ATT),
        compiler_params=_params(2),
        name="attn_prompt",
    )(table, q, k, v)


def _attn_sample_kernel(dec_seq, q_ref, k_ref, v_ref, kc0, vc0, kc1, vc1, kc2, vc2, tc0, tc1, tc2, tn_ref, o_ref):
    kcs, vcs, tcs = (kc0, kc1, kc2), (vc0, vc1, vc2), (tc0, tc1, tc2)
    masks = _head_masks(dec_seq)
    scale = HEAD_DIM ** -0.5

    for s_i in range(SEQ_BLK):
        rows = slice(s_i * dec_seq, (s_i + 1) * dec_seq)
        outs, lses = [], []
        for gi in range(N_ATT):
            cols = slice(gi * LANES, (gi + 1) * LANES)
            qs = _stack_heads(q_ref[rows, cols] * scale, masks)
            k_new, v_new = k_ref[rows, cols], v_ref[rows, cols]
            s_c = _dot(qs, kcs[gi][s_i]) + tcs[gi][...]
            s_n = _dot_nt(qs, k_new) + tn_ref[gi]
            m = jnp.maximum(jnp.max(s_c, axis=1, keepdims=True), jnp.max(s_n, axis=1, keepdims=True))
            p_c = jnp.exp(s_c - m)
            p_n = jnp.exp(s_n - m)
            den = jnp.sum(p_c, axis=1, keepdims=True) + jnp.sum(p_n, axis=1, keepdims=True)
            pv = _dot_nt(p_c, vcs[gi][s_i]) + _dot(p_n, v_new)
            outs.append(_unstack_heads(pv / den, masks, dec_seq))
            lses.append(_unstack_heads(m + jnp.log(den), masks, dec_seq))
        o_ref[rows, :] = _mix_groups(outs, lses)


def _attn_sample_call(l, q, k, v, caches_t, tables, dec_batch, dec_seq):
    rows = SEQ_BLK * dec_seq
    new_spec = pl.BlockSpec((rows, ATT_DIM), lambda i: (i, 0))
    cache_specs = [pl.BlockSpec((None, SEQ_BLK, LANES, c.shape[3]), lambda i: (l, i, 0, 0)) for c in caches_t]
    table_specs = [pl.BlockSpec(t.shape, lambda i, nd=t.ndim: (0,) * nd) for t in tables]
    return pl.pallas_call(
        functools.partial(_attn_sample_kernel, dec_seq),
        grid=(dec_batch // SEQ_BLK,),
        in_specs=[new_spec, new_spec, new_spec] + cache_specs + table_specs,
        out_specs=pl.BlockSpec((rows, ATT_OUT), lambda i: (i, 0)),
        out_shape=jax.ShapeDtypeStruct((dec_batch * dec_seq, ATT_OUT), F32),
        compiler_params=_params(1),
        name="attn_sample",
    )(q, k, v, *caches_t, *tables)


def _merge_kernel(alpha, n_ptiles, xp_ref, xs_ref, mab_ref, gc_ref, ycp_ref, ycs_ref, wuc_ref, wo_ref, g_ref, b_ref,
                  wrt_ref, br_ref, x1g_ref, w_ref, slot_ref, tcnt_ref, tbase_ref, cnt_scr):
    @pl.when(pl.program_id(0) == 0)
    def _():
        cnt_scr[...] = jnp.zeros_like(cnt_scr)

    is_prompt = pl.program_id(0) < n_ptiles
    x = jnp.where(is_prompt, xp_ref[...], xs_ref[...])
    yc = jnp.where(is_prompt, ycp_ref[...], ycs_ref[...])
    merged = mab_ref[...].astype(F32) + gc_ref[...].astype(F32) * _dot(yc, wuc_ref[...])
    x1 = _layer_norm(alpha * x + _dot(merged, wo_ref[...]), g_ref[...], b_ref[...])
    for c in range(LANE_CHUNKS):
        x1g_ref[pl.ds(c, TM_R, stride=LANE_CHUNKS), :] = x1[:, c * LANES:(c + 1) * LANES]

    wr = wrt_ref[...]
    wr_hi, x1_hi = wr.astype(BF16), x1.astype(BF16)
    wr_lo, x1_lo = (wr - wr_hi.astype(F32)).astype(BF16), (x1 - x1_hi.astype(F32)).astype(BF16)
    logits = _dot_nt(wr_hi, x1_hi) + (_dot_nt(wr_lo, x1_hi) + _dot_nt(wr_hi, x1_lo)) + br_ref[...]
    eio = lax.broadcasted_iota(jnp.int32, (N_EXPERTS, TM_R), 0)
    vals, idxs = [], []
    for _ in range(TOP_K):
        mv = jnp.max(logits, axis=0, keepdims=True)
        ix = jnp.min(jnp.where(logits == mv, eio, N_EXPERTS), axis=0, keepdims=True)
        vals.append(mv)
        idxs.append(ix)
        logits = jnp.where(eio == ix, -jnp.inf, logits)
    tv = jnp.concatenate(vals, axis=0)
    e = jnp.exp(tv - tv[0:1, :])
    w_ref[...] = e / jnp.sum(e, axis=0, keepdims=True)

    onehots = [eio == ix for ix in idxs]
    chosen = functools.reduce(lambda a, c: a + c, [jnp.where(oh, 1.0, 0.0) for oh in onehots])
    earlier = (lax.broadcasted_iota(jnp.int32, (TM_R, TM_R), 0) < lax.broadcasted_iota(jnp.int32, (TM_R, TM_R), 1))
    within = _dot(chosen, jnp.where(earlier, 1.0, 0.0))
    tile_cnt = jnp.broadcast_to(jnp.sum(chosen, axis=1, keepdims=True), (N_EXPERTS, LANES))
    lower = (lax.broadcasted_iota(jnp.int32, (N_EXPERTS, N_EXPERTS), 1)
             < lax.broadcasted_iota(jnp.int32, (N_EXPERTS, N_EXPERTS), 0))
    place = within + jnp.dot(jnp.where(lower, 1.0, 0.0), tile_cnt, precision=lax.Precision.HIGHEST,
                             preferred_element_type=F32)[:, 0:1]
    slot_ref[...] = jnp.concatenate(
        [jnp.sum(jnp.where(oh, place, 0.0), axis=0, keepdims=True) for oh in onehots],
        axis=0).astype(jnp.int32) * LANE_CHUNKS
    tcnt_ref[...] = tile_cnt.astype(jnp.int32)
    tbase_ref[...] = cnt_scr[...].astype(jnp.int32)
    cnt_scr[...] = cnt_scr[...] + tile_cnt


def _merge_call(l, alpha, x_p, x_s, mab, gc, yc_p, yc_s, w_up_c, w_o, ln_g, ln_b, w_router_t, b_router):
    n_ptiles = x_p.shape[0] // TM_R
    n = x_p.shape[0] + x_s.shape[0]
    row_spec = lambda w: pl.BlockSpec((TM_R, w), lambda i: (i, 0))
    prow_spec = lambda w: pl.BlockSpec((TM_R, w), lambda i: (jnp.minimum(i, n_ptiles - 1), 0))
    srow_spec = lambda w: pl.BlockSpec((TM_R, w), lambda i: (jnp.maximum(i - n_ptiles, 0), 0))
    vec_spec = pl.BlockSpec((None, 1, D_MODEL), lambda i: (l, 0, 0))
    return pl.pallas_call(
        functools.partial(_merge_kernel, alpha, n_ptiles),
        grid=(n // TM_R,),
        in_specs=[
            prow_spec(D_MODEL), srow_spec(D_MODEL), row_spec(D_MODEL), row_spec(D_MODEL),
            prow_spec(ATT_OUT), srow_spec(ATT_OUT),
            pl.BlockSpec((None, ATT_OUT, D_MODEL), lambda i: (l, 0, 0)),
            pl.BlockSpec((None, D_MODEL, D_MODEL), lambda i: (l, 0, 0)),
            vec_spec, vec_spec,
            pl.BlockSpec((None, N_EXPERTS, D_MODEL), lambda i: (l, 0, 0)),
            pl.BlockSpec((None, N_EXPERTS, 1), lambda i: (l, 0, 0)),
        ],
        out_specs=[
            pl.BlockSpec((TM_R * LANE_CHUNKS, LANES), lambda i: (i, 0)),
            pl.BlockSpec((TOP_K, TM_R), lambda i: (0, i)),
            pl.BlockSpec((TOP_K, TM_R), lambda i: (0, i)),
            pl.BlockSpec((None, N_EXPERTS, LANES), lambda i: (i, 0, 0)),
            pl.BlockSpec((None, N_EXPERTS, LANES), lambda i: (i, 0, 0)),
        ],
        out_shape=[
            jax.ShapeDtypeStruct((n * LANE_CHUNKS, LANES), F32),
            jax.ShapeDtypeStruct((TOP_K, n), F32),
            jax.ShapeDtypeStruct((TOP_K, n), jnp.int32),
            jax.ShapeDtypeStruct((n // TM_R, N_EXPERTS, LANES), jnp.int32),
            jax.ShapeDtypeStruct((n // TM_R, N_EXPERTS, LANES), jnp.int32),
        ],
        scratch_shapes=[pltpu.VMEM((N_EXPERTS, LANES), F32)],
        compiler_params=_params(1),
        name="merge_ln1_router",
    )(x_p, x_s, mab, gc, yc_p, yc_s, w_up_c, w_o, ln_g, ln_b, w_router_t, b_router)


def _destride(ref, rows):
    return jnp.concatenate([ref[pl.ds(c, rows, stride=LANE_CHUNKS), :] for c in range(LANE_CHUNKS)], axis=1)


def _slab_rows(ref, row, count=1):
    return ref.at[pl.ds(pl.multiple_of(row, LANE_CHUNKS), count * LANE_CHUNKS), :]


def _start_expert_blocks(tile, tcnt_ref, hstart_ref, make_copy):
    big_bits = BIG_RUN.bit_length() - 1

    def per_expert(e, staged):
        cnt = tcnt_ref[tile * N_EXPERTS + e]
        first = hstart_ref[tile * N_EXPERTS + e]
        n_big = cnt >> big_bits

        def big(j, carry):
            make_copy(staged + j * BIG_RUN, first + j * BIG_RUN, BIG_RUN).start()
            return carry

        lax.fori_loop(0, n_big, big, 0)
        for bit in reversed(range(big_bits)):
            size = 1 << bit
            done = (cnt >> (bit + 1)) << (bit + 1)

            @pl.when((cnt & size) != 0)
            def _(size=size, done=done):
                make_copy(staged + done, first + done, size).start()
        return staged + cnt

    lax.fori_loop(0, N_EXPERTS, per_expert, 0)


def _dispatch_kernel(n_tiles, lt_ref, has_ref, nv_ref, tcnt_ref, hstart_ref, slot_ref, x1g_ref, xs_hbm,
                     zbuf, stg, zsem, sem):
    tile_rows = TM_E * LANE_CHUNKS
    i = pl.program_id(0)
    n_steps = pl.num_programs(0)
    buf = i % 2
    staged_all = lambda b: pltpu.make_async_copy(stg.at[b], xs_hbm.at[pl.ds(0, TM_R * TOP_K * LANE_CHUNKS), :],
                                                 sem.at[b])

    @pl.when(i == 0)
    def _():
        zbuf[...] = jnp.zeros_like(zbuf)
        zero_copy = lambda t: pltpu.make_async_copy(zbuf, xs_hbm.at[pl.ds(t * tile_rows, tile_rows), :], zsem)
        for e in range(N_EXPERTS):
            @pl.when(has_ref[e] > 0)
            def _(e=e):
                zero_copy(lt_ref[e]).start()
        lax.fori_loop(nv_ref[0], n_tiles, lambda t, c: (zero_copy(t).start(), c)[1], 0)
        for e in range(N_EXPERTS):
            @pl.when(has_ref[e] > 0)
            def _(e=e):
                zero_copy(lt_ref[e]).wait()
        lax.fori_loop(nv_ref[0], n_tiles, lambda t, c: (zero_copy(t).wait(), c)[1], 0)

    def step(b):
        @pl.when(i >= 2)
        def _():
            staged_all(b).wait()

        def place(t, carry):
            slab = _slab_rows(x1g_ref, t * LANE_CHUNKS)[...]
            for kk in range(TOP_K):
                _slab_rows(stg.at[b], slot_ref[0, 0, t * TOP_K + kk])[...] = slab
            return carry

        lax.fori_loop(0, TM_R, place, 0, unroll=DMA_UNROLL)
        _start_expert_blocks(i, tcnt_ref, hstart_ref, lambda staged, row, size: pltpu.make_async_copy(
            _slab_rows(stg.at[b], staged * LANE_CHUNKS, size), _slab_rows(xs_hbm, row * LANE_CHUNKS, size),
            sem.at[b]))

        @pl.when(i == n_steps - 1)
        def _():
            staged_all(b).wait()

        @pl.when(jnp.logical_and(i == n_steps - 1, i >= 1))
        def _():
            staged_all(1 - b).wait()

    for b in range(2):
        @pl.when(buf == b)
        def _(b=b):
            step(b)


def _dispatch_call(last_tile, has_rows, n_valid, tcnt, hstart, slot, x1g, n_tiles):
    n = slot.shape[0] * TM_R
    grid_spec = pltpu.PrefetchScalarGridSpec(
        num_scalar_prefetch=5,
        grid=(n // TM_R,),
        in_specs=[
            pl.BlockSpec((1, 1, TM_R * TOP_K), lambda i, *_: (i, 0, 0), memory_space=pltpu.SMEM),
            pl.BlockSpec((TM_R * LANE_CHUNKS, LANES), lambda i, *_: (i, 0)),
        ],
        out_specs=pl.BlockSpec(memory_space=pl.ANY),
        scratch_shapes=[pltpu.VMEM((TM_E * LANE_CHUNKS, LANES), F32),
                        pltpu.VMEM((2, TM_R * TOP_K * LANE_CHUNKS, LANES), F32),
                        pltpu.SemaphoreType.DMA(()), pltpu.SemaphoreType.DMA((2,))],
    )
    return pl.pallas_call(
        functools.partial(_dispatch_kernel, n_tiles),
        grid_spec=grid_spec,
        out_shape=jax.ShapeDtypeStruct((n_tiles * TM_E * LANE_CHUNKS, LANES), F32),
        compiler_params=_params(1),
        name="moe_dispatch",
    )(last_tile, has_rows, n_valid, tcnt, hstart, slot, x1g)


def _moe_ffn_kernel(te_ref, nv_ref, wslot_ref, xs_ref, wg_ref, bg_ref, wu_ref, bu_ref, wd_ref, bd_ref, ys_ref, w_b16):
    s = pl.program_id(0)
    t = s - W_WARMUP
    last = nv_ref[0] - 1
    expert_at = lambda tile: te_ref[jnp.clip(tile, 0, last)]

    for j, w_ref in enumerate((wg_ref, wu_ref, wd_ref)):
        arriving = expert_at(t + W_AHEAD[j])

        @pl.when(jnp.logical_or(s == 0, arriving != expert_at(t + W_AHEAD[j] - 1)))
        def _(j=j, w_ref=w_ref, arriving=arriving):
            w_b16[wslot_ref[arriving], j] = w_ref[...].astype(BF16)

    @pl.when(jnp.logical_and(t >= 0, t <= last))
    def _():
        slot = wslot_ref[te_ref[jnp.maximum(t, 0)]]
        x = _destride(xs_ref, TM_E).astype(BF16)
        gl = jnp.minimum(jnp.dot(x, w_b16[slot, 0], preferred_element_type=F32) + bg_ref[...], SWIGLU_LIMIT)
        ul = jnp.clip(jnp.dot(x, w_b16[slot, 1], preferred_element_type=F32) + bu_ref[...],
                      -SWIGLU_LIMIT, SWIGLU_LIMIT)
        hid = gl * _sigmoid(SWIGLU_ALPHA * gl) * (ul + 1.0)
        y = jnp.dot(hid.astype(BF16), w_b16[slot, 2], preferred_element_type=F32) + bd_ref[...]
        for c in range(LANE_CHUNKS):
            ys_ref[pl.ds(c, TM_E, stride=LANE_CHUNKS), :] = y[:, c * LANES:(c + 1) * LANES]

    @pl.when(t > last)
    def _():
        ys_ref[...] = jnp.zeros_like(ys_ref)


def _moe_ffn_call(l, tile_expert, n_valid, weight_slot, xs, w_gate, b_gate, w_up, b_up, w_down, b_down):
    n_tiles = tile_expert.shape[0]
    tile = lambda s, nv, ahead=0: jnp.clip(s - W_WARMUP + ahead, 0, nv[0] - 1)
    mat_spec = lambda ahead: pl.BlockSpec((None, None, D_MODEL, D_MODEL),
                                          lambda s, te, nv, ws: (l, te[tile(s, nv, ahead)], 0, 0))
    vec_spec = pl.BlockSpec((None, None, 1, D_MODEL), lambda s, te, nv, ws: (l, te[tile(s, nv)], 0, 0))
    grid_spec = pltpu.PrefetchScalarGridSpec(
        num_scalar_prefetch=3,
        grid=(n_tiles + W_WARMUP,),
        in_specs=[pl.BlockSpec((TM_E * LANE_CHUNKS, LANES), lambda s, te, nv, ws: (tile(s, nv), 0)),
                  mat_spec(W_AHEAD[0]), vec_spec, mat_spec(W_AHEAD[1]), vec_spec, mat_spec(W_AHEAD[2]), vec_spec],
        out_specs=pl.BlockSpec((TM_E * LANE_CHUNKS, LANES), lambda s, te, nv, ws: (jnp.maximum(s - W_WARMUP, 0), 0)),
        scratch_shapes=[pltpu.VMEM((W_SLOTS, 3, D_MODEL, D_MODEL), BF16)],
    )
    return pl.pallas_call(
        _moe_ffn_kernel,
        grid_spec=grid_spec,
        out_shape=jax.ShapeDtypeStruct((n_tiles * TM_E * LANE_CHUNKS, LANES), F32),
        compiler_params=_params(1),
        name="moe_ffn",
    )(tile_expert, n_valid, weight_slot, xs, w_gate, b_gate, w_up, b_up, w_down, b_down)


def _combine_kernel(alpha, n_ptiles, tcnt_ref, hstart_ref, slot_ref, w_ref, x1g_ref, ys_hbm, g_ref, b_ref,
                    x2p_ref, x2s_ref, stg, mixed, sem):
    i = pl.program_id(0)
    n_steps = pl.num_programs(0)
    buf = i % 2

    def fetch(tile, b):
        _start_expert_blocks(tile, tcnt_ref, hstart_ref, lambda staged, row, size: pltpu.make_async_copy(
            _slab_rows(ys_hbm, row * LANE_CHUNKS, size), _slab_rows(stg.at[b], staged * LANE_CHUNKS, size),
            sem.at[b]))

    def step(b):
        @pl.when(i == 0)
        def _():
            fetch(0, b)

        @pl.when(i + 1 < n_steps)
        def _():
            fetch(i + 1, 1 - b)

        pltpu.make_async_copy(ys_hbm.at[pl.ds(0, TM_R * TOP_K * LANE_CHUNKS), :], stg.at[b], sem.at[b]).wait()

        def mix(t, carry):
            c0 = t * TOP_K
            acc = w_ref[0, 0, c0] * _slab_rows(stg.at[b], slot_ref[0, 0, c0])[...]
            for kk in range(1, TOP_K):
                acc = acc + w_ref[0, 0, c0 + kk] * _slab_rows(stg.at[b], slot_ref[0, 0, c0 + kk])[...]
            _slab_rows(mixed, t * LANE_CHUNKS)[...] = acc
            return carry

        lax.fori_loop(0, TM_R, mix, 0, unroll=DMA_UNROLL)

    for b in range(2):
        @pl.when(buf == b)
        def _(b=b):
            step(b)

    x2 = _layer_norm(alpha * _destride(x1g_ref, TM_R) + _destride(mixed, TM_R), g_ref[...], b_ref[...])

    @pl.when(i < n_ptiles)
    def _():
        x2p_ref[...] = x2

    @pl.when(i >= n_ptiles)
    def _():
        x2s_ref[...] = x2


def _combine_call(l, alpha, tcnt, hstart, slot, top_w, x1g, ys, ln_g, ln_b, n_prompt):
    n = x1g.shape[0] // LANE_CHUNKS
    n_ptiles = n_prompt // TM_R
    vec_spec = pl.BlockSpec((None, 1, D_MODEL), lambda i, *_: (l, 0, 0))
    choice_spec = pl.BlockSpec((1, 1, TM_R * TOP_K), lambda i, *_: (i, 0, 0), memory_space=pltpu.SMEM)
    grid_spec = pltpu.PrefetchScalarGridSpec(
        num_scalar_prefetch=2,
        grid=(n // TM_R,),
        in_specs=[
            choice_spec, choice_spec,
            pl.BlockSpec((TM_R * LANE_CHUNKS, LANES), lambda i, *_: (i, 0)),
            pl.BlockSpec(memory_space=pl.ANY),
            vec_spec, vec_spec,
        ],
        out_specs=[pl.BlockSpec((TM_R, D_MODEL), lambda i, *_: (jnp.minimum(i, n_ptiles - 1), 0)),
                   pl.BlockSpec((TM_R, D_MODEL), lambda i, *_: (jnp.maximum(i - n_ptiles, 0), 0))],
        scratch_shapes=[pltpu.VMEM((2, TM_R * TOP_K * LANE_CHUNKS, LANES), F32),
                        pltpu.VMEM((TM_R * LANE_CHUNKS, LANES), F32),
                        pltpu.SemaphoreType.DMA((2,))],
    )
    return pl.pallas_call(
        functools.partial(_combine_kernel, alpha, n_ptiles),
        grid_spec=grid_spec,
        out_shape=[jax.ShapeDtypeStruct((n_prompt, D_MODEL), F32),
                   jax.ShapeDtypeStruct((n - n_prompt, D_MODEL), F32)],
        compiler_params=_params(1),
        name="moe_combine_ln2",
    )(tcnt, hstart, slot, top_w, x1g, ys, ln_g, ln_b)


def _route(tcnt, tbase, n_tiles):
    experts = jnp.arange(N_EXPERTS, dtype=jnp.int32)
    counts = tbase[-1] + tcnt[-1]
    padded = ((counts + TM_E - 1) // TM_E) * TM_E
    ends = jnp.cumsum(padded)
    hstart = (ends - padded)[None, :] + tbase
    n_valid = ends[-1] // TM_E
    last_expert = jnp.max(jnp.where(counts > 0, experts, 0))
    tile_start = jnp.arange(n_tiles, dtype=jnp.int32) * TM_E
    tile_expert = jnp.minimum(jnp.sum(ends[None, :] <= tile_start[:, None], axis=1), last_expert)
    has = (counts > 0).astype(jnp.int32)
    weight_slot = (jnp.cumsum(has) - has) % W_SLOTS
    return (tile_expert.astype(jnp.int32), n_valid.astype(jnp.int32).reshape(1), weight_slot.astype(jnp.int32),
            tcnt.reshape(-1), hstart.astype(jnp.int32).reshape(-1), (ends // TM_E - 1).astype(jnp.int32), has)


def _t5_bucket(dist):
    max_exact = NUM_BUCKETS // 2
    distf = jnp.maximum(dist, 1).astype(F32)
    large = max_exact + (jnp.log(distf / max_exact) / math.log(MAX_DISTANCE / max_exact)
                         * (NUM_BUCKETS - max_exact)).astype(jnp.int32)
    large = jnp.minimum(large, NUM_BUCKETS - 1)
    return jnp.where(dist < max_exact, dist, large)


def _bucket_starts():
    j = jnp.arange(N_KEYS, dtype=jnp.int32)
    b = jnp.arange(NUM_BUCKETS, dtype=jnp.int32)
    return jnp.stack([jnp.sum(_t5_bucket(dil * j)[None, :] < b[:, None], axis=1) for _, dil in ATT_GROUPS],
                     axis=0).astype(jnp.int32).reshape(-1)


def _spatial_tables(w_sp, b_sp, dec_seq):
    depth = w_sp.shape[0]
    tril = jnp.tril(jnp.ones((CHUNK, CHUNK), bool))
    wp = jnp.where(tril, w_sp, 0.0)
    ws_small = jnp.where(tril[:dec_seq, :dec_seq], w_sp[:, :, :dec_seq, :dec_seq], 0.0)
    eye = jnp.eye(CHUNK // dec_seq, dtype=w_sp.dtype)
    ws = jnp.einsum("ab,lgij->lgaibj", eye, ws_small).reshape(depth, SG_GROUPS, CHUNK, CHUNK)
    bp = jnp.repeat(jnp.swapaxes(b_sp, 1, 2), SG_HEAD, axis=2)
    bs = jnp.tile(bp[:, :dec_seq], (1, CHUNK // dec_seq, 1))
    return jnp.stack([wp, ws], axis=1).astype(BF16), jnp.stack([bp, bs], axis=1)


def kernel(x_prompt, x_sample, state_conv, cache_k_w128, cache_v_w128, cache_k_w512, cache_v_w512,
           cache_k_w2048, cache_v_w2048, w_in, conv_w, sg_ln_g, sg_ln_b, w_sp, b_sp, rel_bias,
           w_up_a, w_up_b, w_up_c, w_o, ln1_g, ln1_b, w_router, b_router, w_gate, b_gate, w_up, b_up,
           w_down, b_down, ln2_g, ln2_b):
    batch, seq, _ = x_prompt.shape
    dec_batch, dec_seq, _ = x_sample.shape
    depth = w_in.shape[0]
    n_prompt, n_sample = batch * seq, dec_batch * dec_seq
    n = n_prompt + n_sample
    alpha = (2 * depth) ** 0.25
    assert seq % TM == 0 and n_sample % TM == 0 and dec_seq & (dec_seq - 1) == 0 and CHUNK % dec_seq == 0
    assert n_prompt % TM_R == 0 and n_sample % TM_R == 0
    assert all(seq % (dil * Q_BLK) == 0 and dil & (dil - 1) == 0 for _, dil in ATT_GROUPS)
    assert dec_batch % SEQ_BLK == 0
    n_tiles = (n * TOP_K) // TM_E + N_EXPERTS

    caches_t = [jnp.transpose(c, (0, 1, 3, 4, 2)).reshape(c.shape[0], c.shape[1], LANES, c.shape[2]) for c in
                (cache_k_w128, cache_v_w128, cache_k_w512, cache_v_w512, cache_k_w2048, cache_v_w2048)]
    pasts = tuple(caches_t[2 * gi].shape[3] for gi in range(N_ATT))
    prompt_table, *sample_tables = _bias_tables_call(rel_bias, pasts, dec_seq)
    wsp2, bsp2 = _spatial_tables(w_sp, b_sp, dec_seq)
    w_in_b, w_up_a_b, w_up_b_b, w_up_c_b, w_o_b = (w.astype(BF16) for w in (w_in, w_up_a, w_up_b, w_up_c, w_o))
    vec = lambda a: a.reshape(depth, 1, a.shape[-1])
    w_router_t = jnp.swapaxes(w_router, 1, 2)
    b_router_c = b_router.reshape(depth, N_EXPERTS, 1)
    expert_vec = lambda a: a.reshape(depth, N_EXPERTS, 1, a.shape[-1])

    x_p, x_s = x_prompt.reshape(n_prompt, D_MODEL), x_sample.reshape(n_sample, D_MODEL)
    outs ={name: [] for name in ("conv_p", "conv_s", "sgv", "kp", "vp", "ks", "vs")}
    for l in range(depth):
        s1 = jnp.zeros((dec_batch, dec_seq, CONV_DIM), F32).at[:, 0].set(state_conv[l, :, 1])
        s2 = jnp.zeros((dec_batch, dec_seq, CONV_DIM), F32).at[:, 0].set(state_conv[l, :, 0])
        s2 = s2.at[:, 1].set(state_conv[l, :, 1])
        q_p, k_p, v_p, q_s, k_s, v_s, mab, gc, tail, z_s, v_rows = _inproj_call(
            l, x_p, x_s, w_in_b, conv_w, vec(sg_ln_g), vec(sg_ln_b), wsp2, bsp2, w_up_a_b, w_up_b_b,
            s1.reshape(n_sample, CONV_DIM), s2.reshape(n_sample, CONV_DIM), seq, dec_seq)
        yc_p = _attn_prompt_call(q_p, k_p, v_p, prompt_table, batch, seq)
        yc_s = _attn_sample_call(l, q_s, k_s, v_s, caches_t, sample_tables, dec_batch, dec_seq)
        x1g, top_w, slot, tcnt, tbase = _merge_call(
            l, alpha, x_p, x_s, mab, gc, yc_p, yc_s, w_up_c_b, w_o_b, vec(ln1_g), vec(ln1_b), w_router_t, b_router_c)
        tile_expert, n_valid, weight_slot, tcnt, hstart, last_tile, has_rows = _route(
            tcnt[:, :, 0], tbase[:, :, 0], n_tiles)
        by_tile = lambda a: a.reshape(TOP_K, n // TM_R, TM_R).transpose(1, 2, 0).reshape(n // TM_R, 1, TM_R * TOP_K)
        slot, top_w = by_tile(slot), by_tile(top_w)
        xs = _dispatch_call(last_tile, has_rows, n_valid, tcnt, hstart, slot, x1g, n_tiles)
        ys = _moe_ffn_call(l, tile_expert, n_valid, weight_slot, xs, w_gate, expert_vec(b_gate),
                           w_up, expert_vec(b_up), w_down, expert_vec(b_down))
        x_p, x_s = _combine_call(l, alpha, tcnt, hstart, slot, top_w, x1g, ys, vec(ln2_g), vec(ln2_b), n_prompt)

        outs["conv_p"].append(tail[:, SUBLANES - (CONV_W - 1):])
        outs["conv_s"].append(z_s.reshape(dec_batch, dec_seq, CONV_DIM)[:, dec_seq - (CONV_W - 1):])
        outs["sgv"].append(v_rows.reshape(dec_batch, dec_seq, SG_DIM))
        kp4 = k_p.reshape(batch, seq, N_ATT, HPG, HEAD_DIM)
        vp4 = v_p.reshape(batch, seq, N_ATT, HPG, HEAD_DIM)
        ks4 = k_s.reshape(dec_batch, dec_seq, N_ATT, HPG, HEAD_DIM)
        vs4 = v_s.reshape(dec_batch, dec_seq, N_ATT, HPG, HEAD_DIM)
        outs["kp"].append([kp4[:, seq - min(win, seq):, gi] for gi, (win, _) in enumerate(ATT_GROUPS)])
        outs["vp"].append([vp4[:, seq - min(win, seq):, gi] for gi, (win, _) in enumerate(ATT_GROUPS)])
        outs["ks"].append([ks4[:, dec_seq - min(win, dec_seq):, gi] for gi, (win, _) in enumerate(ATT_GROUPS)])
        outs["vs"].append([vs4[:, dec_seq - min(win, dec_seq):, gi] for gi, (win, _) in enumerate(ATT_GROUPS)])

    stack = lambda name: jnp.stack(outs[name], axis=0)
    per_group = lambda name, gi: jnp.stack([layer[gi] for layer in outs[name]], axis=0)
    result = [x_p.reshape(batch, seq, D_MODEL), x_s.reshape(dec_batch, dec_seq, D_MODEL),
              stack("conv_p"), stack("conv_s")]
    for gi in range(N_ATT):
        result += [per_group("kp", gi), per_group("vp", gi)]
    for gi in range(N_ATT):
        result += [per_group("ks", gi), per_group("vs", gi)]
    result.append(stack("sgv"))
    return tuple(result)
```

```python
import functools
import math

import jax
import jax.numpy as jnp
from jax import lax
from jax.experimental import pallas as pl
from jax.experimental.pallas import tpu as pltpu

D_MODEL = 1024
CONV_DIM = 384
CONV_W = 3
SG_DIM = 256
SG_GROUPS = 4
SG_HEAD = SG_DIM // SG_GROUPS
CHUNK = 128
ATT_GROUPS = ((128, 1), (512, 4), (2048, 16))
N_ATT = len(ATT_GROUPS)
HPG = 4
HEAD_DIM = 32
ATT_DIM = HPG * N_ATT * HEAD_DIM
ATT_OUT = HPG * HEAD_DIM
N_KEYS = 129
NUM_BUCKETS = 32
MAX_DISTANCE = 2048
N_EXPERTS = 32
TOP_K = 4
SWIGLU_LIMIT = 7.0
SWIGLU_ALPHA = 1.702
LN_EPS = 1e-5
OFF_A = 0
OFF_B = OFF_A + 3 * CONV_DIM
OFF_C = OFF_B + 2 * SG_DIM
OFF_G = OFF_C + 3 * ATT_DIM
IN_DIM = OFF_G + 3 * D_MODEL
NEG = -1e30

LANES = 128
SUBLANES = 8
LANE_CHUNKS = D_MODEL // LANES
VMEM_LIMIT = 56 * 1024 * 1024

TM = 256
TM_R = 512
TM_E = 384
W_AHEAD = (2, 1, 0)
W_WARMUP = max(W_AHEAD)
W_SLOTS = W_WARMUP + 1
Q_BLK = 128
SEQ_BLK = 4
DMA_UNROLL = 8
BIG_RUN = 64

F32 = jnp.float32
BF16 = jnp.bfloat16


def _dot(a, b):
    return jnp.dot(a.astype(BF16), b.astype(BF16), preferred_element_type=F32)


def _dot_nt(a, b):
    return lax.dot_general(a.astype(BF16), b.astype(BF16), (((1,), (1,)), ((), ())),
                           preferred_element_type=F32)


def _layer_norm(x, g, b):
    mu = jnp.mean(x, axis=-1, keepdims=True)
    xc = x - mu
    var = jnp.mean(xc * xc, axis=-1, keepdims=True)
    return xc * lax.rsqrt(var + LN_EPS) * g + b


def _gelu(x):
    return 0.5 * x * (1.0 + lax.erf(x * (2.0 ** -0.5)))


def _sigmoid(x):
    return 1.0 / (1.0 + jnp.exp(-x))


def _params(n_axes):
    return pltpu.CompilerParams(dimension_semantics=("arbitrary",) * n_axes,
                                vmem_limit_bytes=VMEM_LIMIT)


def _inproj_kernel(n_ptiles, tiles_per_seq, dec_seq,
                   xp_ref, xs_ref, w_in_ref, convw_ref, lng_ref, lnb_ref, wsp_ref, bsp_ref, wua_ref, wub_ref,
                   s1_ref, s2_ref,
                   qp_ref, kp_ref, vp_ref, qs_ref, ks_ref, vs_ref, mab_ref, gc_ref, tail_ref, zs_ref, vrows_ref,
                   carry_ref):
    i = pl.program_id(0)
    is_prompt = i < n_ptiles

    @pl.when(jnp.logical_and(is_prompt, i % tiles_per_seq == 0))
    def _():
        carry_ref[...] = jnp.zeros_like(carry_ref)

    x = jnp.where(is_prompt, xp_ref[...], xs_ref[...]).astype(BF16)

    pa = jnp.dot(x, w_in_ref[:, OFF_A:OFF_B], preferred_element_type=F32)
    bg, cg, h = pa[:, :CONV_DIM], pa[:, CONV_DIM:2 * CONV_DIM], pa[:, 2 * CONV_DIM:]
    z = cg * h
    row = lax.broadcasted_iota(jnp.int32, (TM, CONV_DIM), 0)
    pos = jnp.where(is_prompt, row, row & (dec_seq - 1))
    prev1 = jnp.where(is_prompt, jnp.broadcast_to(carry_ref[7:8, :], (TM, CONV_DIM)), s1_ref[...])
    prev2 = jnp.where(is_prompt,
                      jnp.where(row == 0, jnp.broadcast_to(carry_ref[6:7, :], (TM, CONV_DIM)),
                                jnp.broadcast_to(carry_ref[7:8, :], (TM, CONV_DIM))),
                      s2_ref[...])
    z1 = jnp.where(pos >= 1, pltpu.roll(z, 1, axis=0), prev1)
    z2 = jnp.where(pos >= 2, pltpu.roll(z, 2, axis=0), prev2)
    cw = convw_ref[...]
    ya = bg * (cw[0:1, :] * z2 + cw[1:2, :] * z1 + cw[2:3, :] * z)

    @pl.when(is_prompt)
    def _():
        carry_ref[...] = z[TM - SUBLANES:, :]
        tail_ref[0] = z[TM - SUBLANES:, :]

    @pl.when(jnp.logical_not(is_prompt))
    def _():
        zs_ref[...] = z

    pb = jnp.dot(x, w_in_ref[:, OFF_B:OFF_C], preferred_element_type=F32)
    uv = _gelu(pb)
    u = uv[:, :SG_DIM]
    vn = _layer_norm(uv[:, SG_DIM:], lng_ref[...], lnb_ref[...])

    @pl.when(jnp.logical_not(is_prompt))
    def _():
        vrows_ref[...] = vn

    vb = vn.astype(BF16)
    low_half = lax.broadcasted_iota(jnp.int32, (CHUNK, LANES), 1) < SG_HEAD
    s_rows = []
    for c in range(TM // CHUNK):
        s_cols = []
        for p in range(SG_DIM // LANES):
            vcp = vb[c * CHUNK:(c + 1) * CHUNK, p * LANES:(p + 1) * LANES]
            sa = jnp.dot(wsp_ref[2 * p], vcp, preferred_element_type=F32)
            sb = jnp.dot(wsp_ref[2 * p + 1], vcp, preferred_element_type=F32)
            s_cols.append(jnp.where(low_half, sa, sb))
        s_rows.append(jnp.concatenate(s_cols, axis=1) + bsp_ref[...])
    yb = u * jnp.concatenate(s_rows, axis=0)

    pc = jnp.dot(x, w_in_ref[:, OFF_C:OFF_G], preferred_element_type=F32)
    for group_refs, active in (((qp_ref, kp_ref, vp_ref), is_prompt),
                               ((qs_ref, ks_ref, vs_ref), jnp.logical_not(is_prompt))):
        @pl.when(active)
        def _(group_refs=group_refs):
            for j, ref in enumerate(group_refs):
                ref[...] = pc[:, j * ATT_DIM:(j + 1) * ATT_DIM]

    ga = _sigmoid(jnp.dot(x, w_in_ref[:, OFF_G:OFF_G + D_MODEL], preferred_element_type=F32))
    mab = ga * _dot(ya, wua_ref[...])
    gb = _sigmoid(jnp.dot(x, w_in_ref[:, OFF_G + D_MODEL:OFF_G + 2 * D_MODEL], preferred_element_type=F32))
    mab_ref[...] = (mab + gb * _dot(yb, wub_ref[...])).astype(BF16)
    gc_ref[...] = _sigmoid(jnp.dot(x, w_in_ref[:, OFF_G + 2 * D_MODEL:], preferred_element_type=F32)).astype(BF16)


def _inproj_call(l, x_p, x_s, w_in, conv_w, sg_ln_g, sg_ln_b, wsp2, bsp2, w_up_a, w_up_b, s1, s2, seq, dec_seq):
    n_prompt, n_sample = x_p.shape[0], x_s.shape[0]
    n = n_prompt + n_sample
    n_ptiles = n_prompt // TM
    n_stiles = n_sample // TM
    tiles_per_seq = seq // TM
    batch = n_prompt // seq

    def stile(i):
        return jnp.maximum(i - n_ptiles, 0)

    def mode(i):
        return jnp.where(i < n_ptiles, 0, 1)

    row_spec = lambda w: pl.BlockSpec((TM, w), lambda i: (i, 0))
    srow_spec = lambda w: pl.BlockSpec((TM, w), lambda i: (stile(i), 0))
    prow_spec = lambda w: pl.BlockSpec((TM, w), lambda i: (jnp.minimum(i, n_ptiles - 1), 0))
    kern = functools.partial(_inproj_kernel, n_ptiles, tiles_per_seq, dec_seq)
    return pl.pallas_call(
        kern,
        grid=(n_ptiles + n_stiles,),
        in_specs=[
            prow_spec(D_MODEL), srow_spec(D_MODEL),
            pl.BlockSpec((None, D_MODEL, IN_DIM), lambda i: (l, 0, 0), pipeline_mode=pl.Buffered(1)),
            pl.BlockSpec((None, CONV_W, CONV_DIM), lambda i: (l, 0, 0)),
            pl.BlockSpec((None, 1, SG_DIM), lambda i: (l, 0, 0)),
            pl.BlockSpec((None, 1, SG_DIM), lambda i: (l, 0, 0)),
            pl.BlockSpec((None, None, SG_GROUPS, CHUNK, CHUNK), lambda i: (l, mode(i), 0, 0, 0)),
            pl.BlockSpec((None, None, CHUNK, SG_DIM), lambda i: (l, mode(i), 0, 0)),
            pl.BlockSpec((None, CONV_DIM, D_MODEL), lambda i: (l, 0, 0)),
            pl.BlockSpec((None, SG_DIM, D_MODEL), lambda i: (l, 0, 0)),
            srow_spec(CONV_DIM),
            srow_spec(CONV_DIM),
        ],
        out_specs=[
            prow_spec(ATT_DIM), prow_spec(ATT_DIM), prow_spec(ATT_DIM),
            srow_spec(ATT_DIM), srow_spec(ATT_DIM), srow_spec(ATT_DIM),
            row_spec(D_MODEL), row_spec(D_MODEL),
            pl.BlockSpec((1, SUBLANES, CONV_DIM),
                         lambda i: (jnp.minimum(i // tiles_per_seq, batch - 1), 0, 0)),
            srow_spec(CONV_DIM),
            srow_spec(SG_DIM),
        ],
        out_shape=[jax.ShapeDtypeStruct((n_prompt, ATT_DIM), F32)] * 3 + [jax.ShapeDtypeStruct((n_sample, ATT_DIM), F32)] * 3 + [
            jax.ShapeDtypeStruct((n, D_MODEL), BF16), jax.ShapeDtypeStruct((n, D_MODEL), BF16),
            jax.ShapeDtypeStruct((batch, SUBLANES, CONV_DIM), F32),
            jax.ShapeDtypeStruct((n_sample, CONV_DIM), F32),
            jax.ShapeDtypeStruct((n_sample, SG_DIM), F32),
        ],
        scratch_shapes=[pltpu.VMEM((SUBLANES, CONV_DIM), F32)],
        compiler_params=_params(1),
        name="inproj_mix",
    )(x_p, x_s, w_in, conv_w, sg_ln_g, sg_ln_b, wsp2, bsp2, w_up_a, w_up_b, s1, s2)


def _head_masks(rows):
    lane_head = lax.broadcasted_iota(jnp.int32, (rows, LANES), 1) // HEAD_DIM
    return [lane_head == h for h in range(HPG)]


def _stack_heads(qb, masks):
    return jnp.concatenate([jnp.where(m, qb, 0.0) for m in masks], axis=0)


def _unstack_heads(stacked, masks, r):
    out = None
    for h, m in enumerate(masks):
        part = jnp.where(m, stacked[h * r:(h + 1) * r], 0.0)
        out = part if out is None else out + part
    return out


def _mix_groups(outs, lses):
    mx = functools.reduce(jnp.maximum, lses)
    ws = [jnp.exp(ls - mx) for ls in lses]
    num = functools.reduce(lambda a, b: a + b, [w * o for w, o in zip(ws, outs)])
    return num / functools.reduce(lambda a, b: a + b, ws)


def _bias_table(rel_ref, start_ref, gi, h, steps):
    def body(b, tab):
        return jnp.where(steps >= start_ref[gi * NUM_BUCKETS + b], rel_ref[b * (N_ATT * HPG) + gi * HPG + h], tab)

    tab = lax.fori_loop(0, NUM_BUCKETS, body, jnp.zeros(steps.shape, F32))
    return jnp.where(jnp.logical_and(steps >= 0, steps < N_KEYS), tab, NEG)


def _bias_tables_kernel(dec_seq, pasts, rel_ref, start_ref, tp_ref, tc0, tc1, tc2, tn_ref):
    tcs = (tc0, tc1, tc2)
    qi = lax.broadcasted_iota(jnp.int32, (Q_BLK, 2 * Q_BLK), 0)
    kc = lax.broadcasted_iota(jnp.int32, (Q_BLK, 2 * Q_BLK), 1)
    for gi, (_, dil) in enumerate(ATT_GROUPS):
        past = pasts[gi]
        shift = dil.bit_length() - 1

        def steps(dist, dil=dil, shift=shift):
            return jnp.where((dist & (dil - 1)) == 0, dist >> shift, -1)

        qi_c = lax.broadcasted_iota(jnp.int32, (dec_seq, past), 0)
        row_c = lax.broadcasted_iota(jnp.int32, (dec_seq, past), 1)
        qi_n = lax.broadcasted_iota(jnp.int32, (dec_seq, dec_seq), 0)
        row_n = lax.broadcasted_iota(jnp.int32, (dec_seq, dec_seq), 1)
        for h in range(HPG):
            table = functools.partial(_bias_table, rel_ref, start_ref, gi, h)
            tp_ref[gi, h * Q_BLK:(h + 1) * Q_BLK, :] = table(qi + Q_BLK - kc)
            tcs[gi][h * dec_seq:(h + 1) * dec_seq, :] = table(steps(past + qi_c - row_c))
            tn_ref[gi, h * dec_seq:(h + 1) * dec_seq, :] = table(steps(qi_n - row_n))


def _bias_tables_call(rel_bias, pasts, dec_seq):
    shapes = ([(N_ATT, HPG * Q_BLK, 2 * Q_BLK)] + [(HPG * dec_seq, p) for p in pasts]
              + [(N_ATT, HPG * dec_seq, dec_seq)])
    return pl.pallas_call(
        functools.partial(_bias_tables_kernel, dec_seq, pasts),
        in_specs=[pl.BlockSpec(memory_space=pltpu.SMEM), pl.BlockSpec(memory_space=pltpu.SMEM)],
        out_shape=[jax.ShapeDtypeStruct(s, F32) for s in shapes],
        compiler_params=pltpu.CompilerParams(vmem_limit_bytes=VMEM_LIMIT),
        name="bias_tables",
    )(rel_bias.astype(F32).reshape(-1), _bucket_starts())


def _attn_prompt_kernel(seq, tab_ref, q_ref, k_ref, v_ref, o_ref, *scr):
    g = pl.program_id(1)
    o_scr, l_scr = scr[:N_ATT], scr[N_ATT:]
    masks = _head_masks(Q_BLK)
    scale = HEAD_DIM ** -0.5
    col = lax.broadcasted_iota(jnp.int32, (HPG * Q_BLK, 2 * Q_BLK), 1)

    def run_group(gi, dil):
        rows_per_class = seq // dil
        n_blk = rows_per_class // Q_BLK

        def ld(ref, start):
            if dil > 1:
                return ref[pl.ds(start, Q_BLK, stride=dil), :]
            return ref[pl.ds(start, Q_BLK), :]

        def st(ref, start, val):
            if dil > 1:
                ref[pl.ds(start, Q_BLK, stride=dil), :] = val
            else:
                ref[pl.ds(start, Q_BLK), :] = val

        def block(it, carry):
            r = it // n_blk
            mb = it % n_blk
            cur = r + dil * Q_BLK * mb
            prev = r + dil * Q_BLK * jnp.maximum(mb - 1, 0)
            qb = ld(q_ref, cur) * scale
            kw = jnp.concatenate([ld(k_ref, prev), ld(k_ref, cur)], axis=0)
            vw = jnp.concatenate([ld(v_ref, prev), ld(v_ref, cur)], axis=0)
            s = _dot_nt(_stack_heads(qb, masks), kw) + tab_ref[...]
            s = jnp.where(jnp.logical_or(col >= Q_BLK, mb > 0), s, NEG)
            m = jnp.max(s, axis=1, keepdims=True)
            p = jnp.exp(s - m)
            den = jnp.sum(p, axis=1, keepdims=True)
            pv = _dot(p, vw)
            st(o_scr[gi], cur, _unstack_heads(pv / den, masks, Q_BLK))
            st(l_scr[gi], cur, _unstack_heads(m + jnp.log(den), masks, Q_BLK))
            return carry

        lax.fori_loop(0, dil * n_blk, block, 0, unroll=4)

    for gi, (_, dil) in enumerate(ATT_GROUPS):
        @pl.when(g == gi)
        def _(gi=gi, dil=dil):
            run_group(gi, dil)

    @pl.when(g == N_ATT - 1)
    def _():
        o_ref[...] = _mix_groups([s[...] for s in o_scr], [s[...] for s in l_scr])


def _attn_prompt_call(q, k, v, table, batch, seq):
    blk = lambda: pl.BlockSpec((seq, LANES), lambda b, g: (b, g))
    return pl.pallas_call(
        functools.partial(_attn_prompt_kernel, seq),
        grid=(batch, N_ATT),
        in_specs=[pl.BlockSpec((None, HPG * Q_BLK, 2 * Q_BLK), lambda b, g: (g, 0, 0)), blk(), blk(), blk()],
        out_specs=pl.BlockSpec((seq, ATT_OUT), lambda b, g: (b, 0)),
        out_shape=jax.ShapeDtypeStruct((batch * seq, ATT_OUT), F32),
        scratch_shapes=[pltpu.VMEM((seq, LANES), F32)] * (2 * N_ATT),
        compiler_params=_params(2),
        name="attn_prompt",
    )(table, q, k, v)


def _attn_sample_kernel(dec_seq, q_ref, k_ref, v_ref, kc0, vc0, kc1, vc1, kc2, vc2, tc0, tc1, tc2, tn_ref, o_ref):
    kcs, vcs, tcs = (kc0, kc1, kc2), (vc0, vc1, vc2), (tc0, tc1, tc2)
    masks = _head_masks(dec_seq)
    scale = HEAD_DIM ** -0.5

    for s_i in range(SEQ_BLK):
        rows = slice(s_i * dec_seq, (s_i + 1) * dec_seq)
        outs, lses = [], []
        for gi in range(N_ATT):
            cols = slice(gi * LANES, (gi + 1) * LANES)
            qs = _stack_heads(q_ref[rows, cols] * scale, masks)
            k_new, v_new = k_ref[rows, cols], v_ref[rows, cols]
            s_c = _dot(qs, kcs[gi][s_i]) + tcs[gi][...]
            s_n = _dot_nt(qs, k_new) + tn_ref[gi]
            m = jnp.maximum(jnp.max(s_c, axis=1, keepdims=True), jnp.max(s_n, axis=1, keepdims=True))
            p_c = jnp.exp(s_c - m)
            p_n = jnp.exp(s_n - m)
            den = jnp.sum(p_c, axis=1, keepdims=True) + jnp.sum(p_n, axis=1, keepdims=True)
            pv = _dot_nt(p_c, vcs[gi][s_i]) + _dot(p_n, v_new)
            outs.append(_unstack_heads(pv / den, masks, dec_seq))
            lses.append(_unstack_heads(m + jnp.log(den), masks, dec_seq))
        o_ref[rows, :] = _mix_groups(outs, lses)


def _attn_sample_call(l, q, k, v, caches_t, tables, dec_batch, dec_seq):
    rows = SEQ_BLK * dec_seq
    new_spec = pl.BlockSpec((rows, ATT_DIM), lambda i: (i, 0))
    cache_specs = [pl.BlockSpec((None, SEQ_BLK, LANES, c.shape[3]), lambda i: (l, i, 0, 0)) for c in caches_t]
    table_specs = [pl.BlockSpec(t.shape, lambda i, nd=t.ndim: (0,) * nd) for t in tables]
    return pl.pallas_call(
        functools.partial(_attn_sample_kernel, dec_seq),
        grid=(dec_batch // SEQ_BLK,),
        in_specs=[new_spec, new_spec, new_spec] + cache_specs + table_specs,
        out_specs=pl.BlockSpec((rows, ATT_OUT), lambda i: (i, 0)),
        out_shape=jax.ShapeDtypeStruct((dec_batch * dec_seq, ATT_OUT), F32),
        compiler_params=_params(1),
        name="attn_sample",
    )(q, k, v, *caches_t, *tables)


def _merge_kernel(alpha, n_ptiles, xp_ref, xs_ref, mab_ref, gc_ref, ycp_ref, ycs_ref, wuc_ref, wo_ref, g_ref, b_ref,
                  wrt_ref, br_ref, x1g_ref, w_ref, slot_ref, tcnt_ref, tbase_ref, cnt_scr):
    @pl.when(pl.program_id(0) == 0)
    def _():
        cnt_scr[...] = jnp.zeros_like(cnt_scr)

    is_prompt = pl.program_id(0) < n_ptiles
    x = jnp.where(is_prompt, xp_ref[...], xs_ref[...])
    yc = jnp.where(is_prompt, ycp_ref[...], ycs_ref[...])
    merged = mab_ref[...].astype(F32) + gc_ref[...].astype(F32) * _dot(yc, wuc_ref[...])
    x1 = _layer_norm(alpha * x + _dot(merged, wo_ref[...]), g_ref[...], b_ref[...])
    for c in range(LANE_CHUNKS):
        x1g_ref[pl.ds(c, TM_R, stride=LANE_CHUNKS), :] = x1[:, c * LANES:(c + 1) * LANES]

    wr = wrt_ref[...]
    wr_hi, x1_hi = wr.astype(BF16), x1.astype(BF16)
    wr_lo, x1_lo = (wr - wr_hi.astype(F32)).astype(BF16), (x1 - x1_hi.astype(F32)).astype(BF16)
    logits = _dot_nt(wr_hi, x1_hi) + (_dot_nt(wr_lo, x1_hi) + _dot_nt(wr_hi, x1_lo)) + br_ref[...]
    eio = lax.broadcasted_iota(jnp.int32, (N_EXPERTS, TM_R), 0)
    vals, idxs = [], []
    for _ in range(TOP_K):
        mv = jnp.max(logits, axis=0, keepdims=True)
        ix = jnp.min(jnp.where(logits == mv, eio, N_EXPERTS), axis=0, keepdims=True)
        vals.append(mv)
        idxs.append(ix)
        logits = jnp.where(eio == ix, -jnp.inf, logits)
    tv = jnp.concatenate(vals, axis=0)
    e = jnp.exp(tv - tv[0:1, :])
    w_ref[...] = e / jnp.sum(e, axis=0, keepdims=True)

    onehots = [eio == ix for ix in idxs]
    chosen = functools.reduce(lambda a, c: a + c, [jnp.where(oh, 1.0, 0.0) for oh in onehots])
    earlier = (lax.broadcasted_iota(jnp.int32, (TM_R, TM_R), 0) < lax.broadcasted_iota(jnp.int32, (TM_R, TM_R), 1))
    within = _dot(chosen, jnp.where(earlier, 1.0, 0.0))
    tile_cnt = jnp.broadcast_to(jnp.sum(chosen, axis=1, keepdims=True), (N_EXPERTS, LANES))
    lower = (lax.broadcasted_iota(jnp.int32, (N_EXPERTS, N_EXPERTS), 1)
             < lax.broadcasted_iota(jnp.int32, (N_EXPERTS, N_EXPERTS), 0))
    place = within + jnp.dot(jnp.where(lower, 1.0, 0.0), tile_cnt, precision=lax.Precision.HIGHEST,
                             preferred_element_type=F32)[:, 0:1]
    slot_ref[...] = jnp.concatenate(
        [jnp.sum(jnp.where(oh, place, 0.0), axis=0, keepdims=True) for oh in onehots],
        axis=0).astype(jnp.int32) * LANE_CHUNKS
    tcnt_ref[...] = tile_cnt.astype(jnp.int32)
    tbase_ref[...] = cnt_scr[...].astype(jnp.int32)
    cnt_scr[...] = cnt_scr[...] + tile_cnt


def _merge_call(l, alpha, x_p, x_s, mab, gc, yc_p, yc_s, w_up_c, w_o, ln_g, ln_b, w_router_t, b_router):
    n_ptiles = x_p.shape[0] // TM_R
    n = x_p.shape[0] + x_s.shape[0]
    row_spec = lambda w: pl.BlockSpec((TM_R, w), lambda i: (i, 0))
    prow_spec = lambda w: pl.BlockSpec((TM_R, w), lambda i: (jnp.minimum(i, n_ptiles - 1), 0))
    srow_spec = lambda w: pl.BlockSpec((TM_R, w), lambda i: (jnp.maximum(i - n_ptiles, 0), 0))
    vec_spec = pl.BlockSpec((None, 1, D_MODEL), lambda i: (l, 0, 0))
    return pl.pallas_call(
        functools.partial(_merge_kernel, alpha, n_ptiles),
        grid=(n // TM_R,),
        in_specs=[
            prow_spec(D_MODEL), srow_spec(D_MODEL), row_spec(D_MODEL), row_spec(D_MODEL),
            prow_spec(ATT_OUT), srow_spec(ATT_OUT),
            pl.BlockSpec((None, ATT_OUT, D_MODEL), lambda i: (l, 0, 0)),
            pl.BlockSpec((None, D_MODEL, D_MODEL), lambda i: (l, 0, 0)),
            vec_spec, vec_spec,
            pl.BlockSpec((None, N_EXPERTS, D_MODEL), lambda i: (l, 0, 0)),
            pl.BlockSpec((None, N_EXPERTS, 1), lambda i: (l, 0, 0)),
        ],
        out_specs=[
            pl.BlockSpec((TM_R * LANE_CHUNKS, LANES), lambda i: (i, 0)),
            pl.BlockSpec((TOP_K, TM_R), lambda i: (0, i)),
            pl.BlockSpec((TOP_K, TM_R), lambda i: (0, i)),
            pl.BlockSpec((None, N_EXPERTS, LANES), lambda i: (i, 0, 0)),
            pl.BlockSpec((None, N_EXPERTS, LANES), lambda i: (i, 0, 0)),
        ],
        out_shape=[
            jax.ShapeDtypeStruct((n * LANE_CHUNKS, LANES), F32),
            jax.ShapeDtypeStruct((TOP_K, n), F32),
            jax.ShapeDtypeStruct((TOP_K, n), jnp.int32),
            jax.ShapeDtypeStruct((n // TM_R, N_EXPERTS, LANES), jnp.int32),
            jax.ShapeDtypeStruct((n // TM_R, N_EXPERTS, LANES), jnp.int32),
        ],
        scratch_shapes=[pltpu.VMEM((N_EXPERTS, LANES), F32)],
        compiler_params=_params(1),
        name="merge_ln1_router",
    )(x_p, x_s, mab, gc, yc_p, yc_s, w_up_c, w_o, ln_g, ln_b, w_router_t, b_router)


def _destride(ref, rows):
    return jnp.concatenate([ref[pl.ds(c, rows, stride=LANE_CHUNKS), :] for c in range(LANE_CHUNKS)], axis=1)


def _slab_rows(ref, row, count=1):
    return ref.at[pl.ds(pl.multiple_of(row, LANE_CHUNKS), count * LANE_CHUNKS), :]


def _start_expert_blocks(tile, tcnt_ref, hstart_ref, make_copy):
    big_bits = BIG_RUN.bit_length() - 1

    def per_expert(e, staged):
        cnt = tcnt_ref[tile * N_EXPERTS + e]
        first = hstart_ref[tile * N_EXPERTS + e]
        n_big = cnt >> big_bits

        def big(j, carry):
            make_copy(staged + j * BIG_RUN, first + j * BIG_RUN, BIG_RUN).start()
            return carry

        lax.fori_loop(0, n_big, big, 0)
        for bit in reversed(range(big_bits)):
            size = 1 << bit
            done = (cnt >> (bit + 1)) << (bit + 1)

            @pl.when((cnt & size) != 0)
            def _(size=size, done=done, thread=bit % 2):
                make_copy(staged + done, first + done, size).start(priority=thread)
        return staged + cnt

    lax.fori_loop(0, N_EXPERTS, per_expert, 0)


def _dispatch_kernel(n_tiles, lt_ref, has_ref, nv_ref, tcnt_ref, hstart_ref, slot_ref, x1g_ref, xs_hbm,
                     zbuf, stg, zsem, sem):
    tile_rows = TM_E * LANE_CHUNKS
    i = pl.program_id(0)
    n_steps = pl.num_programs(0)
    buf = i % 2
    staged_all = lambda b: pltpu.make_async_copy(stg.at[b], xs_hbm.at[pl.ds(0, TM_R * TOP_K * LANE_CHUNKS), :],
                                                 sem.at[b])

    @pl.when(i == 0)
    def _():
        zbuf[...] = jnp.zeros_like(zbuf)
        zero_copy = lambda t: pltpu.make_async_copy(zbuf, xs_hbm.at[pl.ds(t * tile_rows, tile_rows), :], zsem)
        for e in range(N_EXPERTS):
            @pl.when(has_ref[e] > 0)
            def _(e=e):
                zero_copy(lt_ref[e]).start()
        lax.fori_loop(nv_ref[0], n_tiles, lambda t, c: (zero_copy(t).start(), c)[1], 0)
        for e in range(N_EXPERTS):
            @pl.when(has_ref[e] > 0)
            def _(e=e):
                zero_copy(lt_ref[e]).wait()
        lax.fori_loop(nv_ref[0], n_tiles, lambda t, c: (zero_copy(t).wait(), c)[1], 0)

    def step(b):
        @pl.when(i >= 2)
        def _():
            staged_all(b).wait()

        def place(t, carry):
            slab = _slab_rows(x1g_ref, t * LANE_CHUNKS)[...]
            for kk in range(TOP_K):
                _slab_rows(stg.at[b], slot_ref[0, 0, t * TOP_K + kk])[...] = slab
            return carry

        lax.fori_loop(0, TM_R, place, 0, unroll=DMA_UNROLL)
        _start_expert_blocks(i, tcnt_ref, hstart_ref, lambda staged, row, size: pltpu.make_async_copy(
            _slab_rows(stg.at[b], staged * LANE_CHUNKS, size), _slab_rows(xs_hbm, row * LANE_CHUNKS, size),
            sem.at[b]))

        @pl.when(i == n_steps - 1)
        def _():
            staged_all(b).wait()

        @pl.when(jnp.logical_and(i == n_steps - 1, i >= 1))
        def _():
            staged_all(1 - b).wait()

    for b in range(2):
        @pl.when(buf == b)
        def _(b=b):
            step(b)


def _dispatch_call(last_tile, has_rows, n_valid, tcnt, hstart, slot, x1g, n_tiles):
    n = slot.shape[0] * TM_R
    grid_spec = pltpu.PrefetchScalarGridSpec(
        num_scalar_prefetch=5,
        grid=(n // TM_R,),
        in_specs=[
            pl.BlockSpec((1, 1, TM_R * TOP_K), lambda i, *_: (i, 0, 0), memory_space=pltpu.SMEM),
            pl.BlockSpec((TM_R * LANE_CHUNKS, LANES), lambda i, *_: (i, 0)),
        ],
        out_specs=pl.BlockSpec(memory_space=pl.ANY),
        scratch_shapes=[pltpu.VMEM((TM_E * LANE_CHUNKS, LANES), F32),
                        pltpu.VMEM((2, TM_R * TOP_K * LANE_CHUNKS, LANES), F32),
                        pltpu.SemaphoreType.DMA(()), pltpu.SemaphoreType.DMA((2,))],
    )
    return pl.pallas_call(
        functools.partial(_dispatch_kernel, n_tiles),
        grid_spec=grid_spec,
        out_shape=jax.ShapeDtypeStruct((n_tiles * TM_E * LANE_CHUNKS, LANES), F32),
        compiler_params=_params(1),
        name="moe_dispatch",
    )(last_tile, has_rows, n_valid, tcnt, hstart, slot, x1g)


def _moe_ffn_kernel(te_ref, nv_ref, wslot_ref, xs_ref, wg_ref, bg_ref, wu_ref, bu_ref, wd_ref, bd_ref, ys_ref, w_b16):
    s = pl.program_id(0)
    t = s - W_WARMUP
    last = nv_ref[0] - 1
    expert_at = lambda tile: te_ref[jnp.clip(tile, 0, last)]

    for j, w_ref in enumerate((wg_ref, wu_ref, wd_ref)):
        arriving = expert_at(t + W_AHEAD[j])

        @pl.when(jnp.logical_or(s == 0, arriving != expert_at(t + W_AHEAD[j] - 1)))
        def _(j=j, w_ref=w_ref, arriving=arriving):
            w_b16[wslot_ref[arriving], j] = w_ref[...].astype(BF16)

    @pl.when(jnp.logical_and(t >= 0, t <= last))
    def _():
        slot = wslot_ref[te_ref[jnp.maximum(t, 0)]]
        x = _destride(xs_ref, TM_E).astype(BF16)
        gl = jnp.minimum(jnp.dot(x, w_b16[slot, 0], preferred_element_type=F32) + bg_ref[...], SWIGLU_LIMIT)
        ul = jnp.clip(jnp.dot(x, w_b16[slot, 1], preferred_element_type=F32) + bu_ref[...],
                      -SWIGLU_LIMIT, SWIGLU_LIMIT)
        hid = gl * _sigmoid(SWIGLU_ALPHA * gl) * (ul + 1.0)
        y = jnp.dot(hid.astype(BF16), w_b16[slot, 2], preferred_element_type=F32) + bd_ref[...]
        for c in range(LANE_CHUNKS):
            ys_ref[pl.ds(c, TM_E, stride=LANE_CHUNKS), :] = y[:, c * LANES:(c + 1) * LANES]

    @pl.when(t > last)
    def _():
        ys_ref[...] = jnp.zeros_like(ys_ref)


def _moe_ffn_call(l, tile_expert, n_valid, weight_slot, xs, w_gate, b_gate, w_up, b_up, w_down, b_down):
    n_tiles = tile_expert.shape[0]
    tile = lambda s, nv, ahead=0: jnp.clip(s - W_WARMUP + ahead, 0, nv[0] - 1)
    mat_spec = lambda ahead: pl.BlockSpec((None, None, D_MODEL, D_MODEL),
                                          lambda s, te, nv, ws: (l, te[tile(s, nv, ahead)], 0, 0))
    vec_spec = pl.BlockSpec((None, None, 1, D_MODEL), lambda s, te, nv, ws: (l, te[tile(s, nv)], 0, 0))
    grid_spec = pltpu.PrefetchScalarGridSpec(
        num_scalar_prefetch=3,
        grid=(n_tiles + W_WARMUP,),
        in_specs=[pl.BlockSpec((TM_E * LANE_CHUNKS, LANES), lambda s, te, nv, ws: (tile(s, nv), 0)),
                  mat_spec(W_AHEAD[0]), vec_spec, mat_spec(W_AHEAD[1]), vec_spec, mat_spec(W_AHEAD[2]), vec_spec],
        out_specs=pl.BlockSpec((TM_E * LANE_CHUNKS, LANES), lambda s, te, nv, ws: (jnp.maximum(s - W_WARMUP, 0), 0)),
        scratch_shapes=[pltpu.VMEM((W_SLOTS, 3, D_MODEL, D_MODEL), BF16)],
    )
    return pl.pallas_call(
        _moe_ffn_kernel,
        grid_spec=grid_spec,
        out_shape=jax.ShapeDtypeStruct((n_tiles * TM_E * LANE_CHUNKS, LANES), F32),
        compiler_params=_params(1),
        name="moe_ffn",
    )(tile_expert, n_valid, weight_slot, xs, w_gate, b_gate, w_up, b_up, w_down, b_down)


def _combine_kernel(alpha, n_ptiles, tcnt_ref, hstart_ref, slot_ref, w_ref, x1g_ref, ys_hbm, g_ref, b_ref,
                    x2p_ref, x2s_ref, stg, mixed, sem):
    i = pl.program_id(0)
    n_steps = pl.num_programs(0)
    buf = i % 2

    def fetch(tile, b):
        _start_expert_blocks(tile, tcnt_ref, hstart_ref, lambda staged, row, size: pltpu.make_async_copy(
            _slab_rows(ys_hbm, row * LANE_CHUNKS, size), _slab_rows(stg.at[b], staged * LANE_CHUNKS, size),
            sem.at[b]))

    def step(b):
        @pl.when(i == 0)
        def _():
            fetch(0, b)

        @pl.when(i + 1 < n_steps)
        def _():
            fetch(i + 1, 1 - b)

        pltpu.make_async_copy(ys_hbm.at[pl.ds(0, TM_R * TOP_K * LANE_CHUNKS), :], stg.at[b], sem.at[b]).wait()

        def mix(t, carry):
            c0 = t * TOP_K
            acc = w_ref[0, 0, c0] * _slab_rows(stg.at[b], slot_ref[0, 0, c0])[...]
            for kk in range(1, TOP_K):
                acc = acc + w_ref[0, 0, c0 + kk] * _slab_rows(stg.at[b], slot_ref[0, 0, c0 + kk])[...]
            _slab_rows(mixed, t * LANE_CHUNKS)[...] = acc
            return carry

        lax.fori_loop(0, TM_R, mix, 0, unroll=DMA_UNROLL)

    for b in range(2):
        @pl.when(buf == b)
        def _(b=b):
            step(b)

    x2 = _layer_norm(alpha * _destride(x1g_ref, TM_R) + _destride(mixed, TM_R), g_ref[...], b_ref[...])

    @pl.when(i < n_ptiles)
    def _():
        x2p_ref[...] = x2

    @pl.when(i >= n_ptiles)
    def _():
        x2s_ref[...] = x2


def _combine_call(l, alpha, tcnt, hstart, slot, top_w, x1g, ys, ln_g, ln_b, n_prompt):
    n = x1g.shape[0] // LANE_CHUNKS
    n_ptiles = n_prompt // TM_R
    vec_spec = pl.BlockSpec((None, 1, D_MODEL), lambda i, *_: (l, 0, 0))
    choice_spec = pl.BlockSpec((1, 1, TM_R * TOP_K), lambda i, *_: (i, 0, 0), memory_space=pltpu.SMEM)
    grid_spec = pltpu.PrefetchScalarGridSpec(
        num_scalar_prefetch=2,
        grid=(n // TM_R,),
        in_specs=[
            choice_spec, choice_spec,
            pl.BlockSpec((TM_R * LANE_CHUNKS, LANES), lambda i, *_: (i, 0)),
            pl.BlockSpec(memory_space=pl.ANY),
            vec_spec, vec_spec,
        ],
        out_specs=[pl.BlockSpec((TM_R, D_MODEL), lambda i, *_: (jnp.minimum(i, n_ptiles - 1), 0)),
                   pl.BlockSpec((TM_R, D_MODEL), lambda i, *_: (jnp.maximum(i - n_ptiles, 0), 0))],
        scratch_shapes=[pltpu.VMEM((2, TM_R * TOP_K * LANE_CHUNKS, LANES), F32),
                        pltpu.VMEM((TM_R * LANE_CHUNKS, LANES), F32),
                        pltpu.SemaphoreType.DMA((2,))],
    )
    return pl.pallas_call(
        functools.partial(_combine_kernel, alpha, n_ptiles),
        grid_spec=grid_spec,
        out_shape=[jax.ShapeDtypeStruct((n_prompt, D_MODEL), F32),
                   jax.ShapeDtypeStruct((n - n_prompt, D_MODEL), F32)],
        compiler_params=_params(1),
        name="moe_combine_ln2",
    )(tcnt, hstart, slot, top_w, x1g, ys, ln_g, ln_b)


def _route(tcnt, tbase, n_tiles):
    experts = jnp.arange(N_EXPERTS, dtype=jnp.int32)
    counts = tbase[-1] + tcnt[-1]
    padded = ((counts + TM_E - 1) // TM_E) * TM_E
    ends = jnp.cumsum(padded)
    hstart = (ends - padded)[None, :] + tbase
    n_valid = ends[-1] // TM_E
    last_expert = jnp.max(jnp.where(counts > 0, experts, 0))
    tile_start = jnp.arange(n_tiles, dtype=jnp.int32) * TM_E
    tile_expert = jnp.minimum(jnp.sum(ends[None, :] <= tile_start[:, None], axis=1), last_expert)
    has = (counts > 0).astype(jnp.int32)
    weight_slot = (jnp.cumsum(has) - has) % W_SLOTS
    return (tile_expert.astype(jnp.int32), n_valid.astype(jnp.int32).reshape(1), weight_slot.astype(jnp.int32),
            tcnt.reshape(-1), hstart.astype(jnp.int32).reshape(-1), (ends // TM_E - 1).astype(jnp.int32), has)


def _t5_bucket(dist):
    max_exact = NUM_BUCKETS // 2
    distf = jnp.maximum(dist, 1).astype(F32)
    large = max_exact + (jnp.log(distf / max_exact) / math.log(MAX_DISTANCE / max_exact)
                         * (NUM_BUCKETS - max_exact)).astype(jnp.int32)
    large = jnp.minimum(large, NUM_BUCKETS - 1)
    return jnp.where(dist < max_exact, dist, large)


def _bucket_starts():
    j = jnp.arange(N_KEYS, dtype=jnp.int32)
    b = jnp.arange(NUM_BUCKETS, dtype=jnp.int32)
    return jnp.stack([jnp.sum(_t5_bucket(dil * j)[None, :] < b[:, None], axis=1) for _, dil in ATT_GROUPS],
                     axis=0).astype(jnp.int32).reshape(-1)


def _spatial_tables(w_sp, b_sp, dec_seq):
    depth = w_sp.shape[0]
    tril = jnp.tril(jnp.ones((CHUNK, CHUNK), bool))
    wp = jnp.where(tril, w_sp, 0.0)
    ws_small = jnp.where(tril[:dec_seq, :dec_seq], w_sp[:, :, :dec_seq, :dec_seq], 0.0)
    eye = jnp.eye(CHUNK // dec_seq, dtype=w_sp.dtype)
    ws = jnp.einsum("ab,lgij->lgaibj", eye, ws_small).reshape(depth, SG_GROUPS, CHUNK, CHUNK)
    bp = jnp.repeat(jnp.swapaxes(b_sp, 1, 2), SG_HEAD, axis=2)
    bs = jnp.tile(bp[:, :dec_seq], (1, CHUNK // dec_seq, 1))
    return jnp.stack([wp, ws], axis=1).astype(BF16), jnp.stack([bp, bs], axis=1)


def kernel(x_prompt, x_sample, state_conv, cache_k_w128, cache_v_w128, cache_k_w512, cache_v_w512,
           cache_k_w2048, cache_v_w2048, w_in, conv_w, sg_ln_g, sg_ln_b, w_sp, b_sp, rel_bias,
           w_up_a, w_up_b, w_up_c, w_o, ln1_g, ln1_b, w_router, b_router, w_gate, b_gate, w_up, b_up,
           w_down, b_down, ln2_g, ln2_b):
    batch, seq, _ = x_prompt.shape
    dec_batch, dec_seq, _ = x_sample.shape
    depth = w_in.shape[0]
    n_prompt, n_sample = batch * seq, dec_batch * dec_seq
    n = n_prompt + n_sample
    alpha = (2 * depth) ** 0.25
    assert seq % TM == 0 and n_sample % TM == 0 and dec_seq & (dec_seq - 1) == 0 and CHUNK % dec_seq == 0
    assert n_prompt % TM_R == 0 and n_sample % TM_R == 0
    assert all(seq % (dil * Q_BLK) == 0 and dil & (dil - 1) == 0 for _, dil in ATT_GROUPS)
    assert dec_batch % SEQ_BLK == 0
    n_tiles = (n * TOP_K) // TM_E + N_EXPERTS

    caches_t = [jnp.transpose(c, (0, 1, 3, 4, 2)).reshape(c.shape[0], c.shape[1], LANES, c.shape[2]) for c in
                (cache_k_w128, cache_v_w128, cache_k_w512, cache_v_w512, cache_k_w2048, cache_v_w2048)]
    pasts = tuple(caches_t[2 * gi].shape[3] for gi in range(N_ATT))
    prompt_table, *sample_tables = _bias_tables_call(rel_bias, pasts, dec_seq)
    wsp2, bsp2 = _spatial_tables(w_sp, b_sp, dec_seq)
    w_in_b, w_up_a_b, w_up_b_b, w_up_c_b, w_o_b = (w.astype(BF16) for w in (w_in, w_up_a, w_up_b, w_up_c, w_o))
    vec = lambda a: a.reshape(depth, 1, a.shape[-1])
    w_router_t = jnp.swapaxes(w_router, 1, 2)
    b_router_c = b_router.reshape(depth, N_EXPERTS, 1)
    expert_vec = lambda a: a.reshape(depth, N_EXPERTS, 1, a.shape[-1])

    x_p, x_s = x_prompt.reshape(n_prompt, D_MODEL), x_sample.reshape(n_sample, D_MODEL)
    outs ={name: [] for name in ("conv_p", "conv_s", "sgv", "kp", "vp", "ks", "vs")}
    for l in range(depth):
        s1 = jnp.zeros((dec_batch, dec_seq, CONV_DIM), F32).at[:, 0].set(state_conv[l, :, 1])
        s2 = jnp.zeros((dec_batch, dec_seq, CONV_DIM), F32).at[:, 0].set(state_conv[l, :, 0])
        s2 = s2.at[:, 1].set(state_conv[l, :, 1])
        q_p, k_p, v_p, q_s, k_s, v_s, mab, gc, tail, z_s, v_rows = _inproj_call(
            l, x_p, x_s, w_in_b, conv_w, vec(sg_ln_g), vec(sg_ln_b), wsp2, bsp2, w_up_a_b, w_up_b_b,
            s1.reshape(n_sample, CONV_DIM), s2.reshape(n_sample, CONV_DIM), seq, dec_seq)
        yc_p = _attn_prompt_call(q_p, k_p, v_p, prompt_table, batch, seq)
        yc_s = _attn_sample_call(l, q_s, k_s, v_s, caches_t, sample_tables, dec_batch, dec_seq)
        x1g, top_w, slot, tcnt, tbase = _merge_call(
            l, alpha, x_p, x_s, mab, gc, yc_p, yc_s, w_up_c_b, w_o_b, vec(ln1_g), vec(ln1_b), w_router_t, b_router_c)
        tile_expert, n_valid, weight_slot, tcnt, hstart, last_tile, has_rows = _route(
            tcnt[:, :, 0], tbase[:, :, 0], n_tiles)
        by_tile = lambda a: a.reshape(TOP_K, n // TM_R, TM_R).transpose(1, 2, 0).reshape(n // TM_R, 1, TM_R * TOP_K)
        slot, top_w = by_tile(slot), by_tile(top_w)
        xs = _dispatch_call(last_tile, has_rows, n_valid, tcnt, hstart, slot, x1g, n_tiles)
        ys = _moe_ffn_call(l, tile_expert, n_valid, weight_slot, xs, w_gate, expert_vec(b_gate),
                           w_up, expert_vec(b_up), w_down, expert_vec(b_down))
        x_p, x_s = _combine_call(l, alpha, tcnt, hstart, slot, top_w, x1g, ys, vec(ln2_g), vec(ln2_b), n_prompt)

        outs["conv_p"].append(tail[:, SUBLANES - (CONV_W - 1):])
        outs["conv_s"].append(z_s.reshape(dec_batch, dec_seq, CONV_DIM)[:, dec_seq - (CONV_W - 1):])
        outs["sgv"].append(v_rows.reshape(dec_batch, dec_seq, SG_DIM))
        kp4 = k_p.reshape(batch, seq, N_ATT, HPG, HEAD_DIM)
        vp4 = v_p.reshape(batch, seq, N_ATT, HPG, HEAD_DIM)
        ks4 = k_s.reshape(dec_batch, dec_seq, N_ATT, HPG, HEAD_DIM)
        vs4 = v_s.reshape(dec_batch, dec_seq, N_ATT, HPG, HEAD_DIM)
        outs["kp"].append([kp4[:, seq - min(win, seq):, gi] for gi, (win, _) in enumerate(ATT_GROUPS)])
        outs["vp"].append([vp4[:, seq - min(win, seq):, gi] for gi, (win, _) in enumerate(ATT_GROUPS)])
        outs["ks"].append([ks4[:, dec_seq - min(win, dec_seq):, gi] for gi, (win, _) in enumerate(ATT_GROUPS)])
        outs["vs"].append([vs4[:, dec_seq - min(win, dec_seq):, gi] for gi, (win, _) in enumerate(ATT_GROUPS)])

    stack = lambda name: jnp.stack(outs[name], axis=0)
    per_group = lambda name, gi: jnp.stack([layer[gi] for layer in outs[name]], axis=0)
    result = [x_p.reshape(batch, seq, D_MODEL), x_s.reshape(dec_batch, dec_seq, D_MODEL),
              stack("conv_p"), stack("conv_s")]
    for gi in range(N_ATT):
        result += [per_group("kp", gi), per_group("vp", gi)]
    for gi in range(N_ATT):
        result += [per_group("ks", gi), per_group("vs", gi)]
    result.append(stack("sgv"))
    return tuple(result)
```

```python
import functools
import math

import jax
import jax.numpy as jnp
from jax import lax
from jax.experimental import pallas as pl
from jax.experimental.pallas import tpu as pltpu

D_MODEL = 1024
CONV_DIM = 384
CONV_W = 3
SG_DIM = 256
SG_GROUPS = 4
SG_HEAD = SG_DIM // SG_GROUPS
CHUNK = 128
ATT_GROUPS = ((128, 1), (512, 4), (2048, 16))
N_ATT = len(ATT_GROUPS)
HPG = 4
HEAD_DIM = 32
ATT_DIM = HPG * N_ATT * HEAD_DIM
ATT_OUT = HPG * HEAD_DIM
N_KEYS = 129
NUM_BUCKETS = 32
MAX_DISTANCE = 2048
N_EXPERTS = 32
TOP_K = 4
SWIGLU_LIMIT = 7.0
SWIGLU_ALPHA = 1.702
LN_EPS = 1e-5
OFF_A = 0
OFF_B = OFF_A + 3 * CONV_DIM
OFF_C = OFF_B + 2 * SG_DIM
OFF_G = OFF_C + 3 * ATT_DIM
IN_DIM = OFF_G + 3 * D_MODEL
NEG = -1e30

LANES = 128
SUBLANES = 8
LANE_CHUNKS = D_MODEL // LANES
VMEM_LIMIT = 56 * 1024 * 1024

TM = 256
TM_R = 512
TM_E = 384
W_AHEAD = (2, 1, 0)
W_WARMUP = max(W_AHEAD)
W_SLOTS = W_WARMUP + 1
Q_BLK = 128
SEQ_BLK = 4
DMA_UNROLL = 8
BIG_RUN = 64

F32 = jnp.float32
BF16 = jnp.bfloat16


def _dot(a, b):
    return jnp.dot(a.astype(BF16), b.astype(BF16), preferred_element_type=F32)


def _dot_nt(a, b):
    return lax.dot_general(a.astype(BF16), b.astype(BF16), (((1,), (1,)), ((), ())),
                           preferred_element_type=F32)


def _layer_norm(x, g, b):
    mu = jnp.mean(x, axis=-1, keepdims=True)
    xc = x - mu
    var = jnp.mean(xc * xc, axis=-1, keepdims=True)
    return xc * lax.rsqrt(var + LN_EPS) * g + b


def _gelu(x):
    return 0.5 * x * (1.0 + lax.erf(x * (2.0 ** -0.5)))


def _sigmoid(x):
    return 1.0 / (1.0 + jnp.exp(-x))


def _params(n_axes):
    return pltpu.CompilerParams(dimension_semantics=("arbitrary",) * n_axes,
                                vmem_limit_bytes=VMEM_LIMIT)


def _inproj_kernel(n_ptiles, tiles_per_seq, dec_seq,
                   xp_ref, xs_ref, w_in_ref, convw_ref, lng_ref, lnb_ref, wsp_ref, bsp_ref, wua_ref, wub_ref,
                   s1_ref, s2_ref,
                   qp_ref, kp_ref, vp_ref, qs_ref, ks_ref, vs_ref, mab_ref, gc_ref, tail_ref, zs_ref, vrows_ref,
                   carry_ref):
    i = pl.program_id(0)
    is_prompt = i < n_ptiles

    @pl.when(jnp.logical_and(is_prompt, i % tiles_per_seq == 0))
    def _():
        carry_ref[...] = jnp.zeros_like(carry_ref)

    x = jnp.where(is_prompt, xp_ref[...], xs_ref[...]).astype(BF16)

    pa = jnp.dot(x, w_in_ref[:, OFF_A:OFF_B], preferred_element_type=F32)
    bg, cg, h = pa[:, :CONV_DIM], pa[:, CONV_DIM:2 * CONV_DIM], pa[:, 2 * CONV_DIM:]
    z = cg * h
    row = lax.broadcasted_iota(jnp.int32, (TM, CONV_DIM), 0)
    pos = jnp.where(is_prompt, row, row & (dec_seq - 1))
    prev1 = jnp.where(is_prompt, jnp.broadcast_to(carry_ref[7:8, :], (TM, CONV_DIM)), s1_ref[...])
    prev2 = jnp.where(is_prompt,
                      jnp.where(row == 0, jnp.broadcast_to(carry_ref[6:7, :], (TM, CONV_DIM)),
                                jnp.broadcast_to(carry_ref[7:8, :], (TM, CONV_DIM))),
                      s2_ref[...])
    z1 = jnp.where(pos >= 1, pltpu.roll(z, 1, axis=0), prev1)
    z2 = jnp.where(pos >= 2, pltpu.roll(z, 2, axis=0), prev2)
    cw = convw_ref[...]
    ya = bg * (cw[0:1, :] * z2 + cw[1:2, :] * z1 + cw[2:3, :] * z)

    @pl.when(is_prompt)
    def _():
        carry_ref[...] = z[TM - SUBLANES:, :]
        tail_ref[0] = z[TM - SUBLANES:, :]

    @pl.when(jnp.logical_not(is_prompt))
    def _():
        zs_ref[...] = z

    pb = jnp.dot(x, w_in_ref[:, OFF_B:OFF_C], preferred_element_type=F32)
    uv = _gelu(pb)
    u = uv[:, :SG_DIM]
    vn = _layer_norm(uv[:, SG_DIM:], lng_ref[...], lnb_ref[...])

    @pl.when(jnp.logical_not(is_prompt))
    def _():
        vrows_ref[...] = vn

    vb = vn.astype(BF16)
    low_half = lax.broadcasted_iota(jnp.int32, (CHUNK, LANES), 1) < SG_HEAD
    s_rows = []
    for c in range(TM // CHUNK):
        s_cols = []
        for p in range(SG_DIM // LANES):
            vcp = vb[c * CHUNK:(c + 1) * CHUNK, p * LANES:(p + 1) * LANES]
            sa = jnp.dot(wsp_ref[2 * p], vcp, preferred_element_type=F32)
            sb = jnp.dot(wsp_ref[2 * p + 1], vcp, preferred_element_type=F32)
            s_cols.append(jnp.where(low_half, sa, sb))
        s_rows.append(jnp.concatenate(s_cols, axis=1) + bsp_ref[...])
    yb = u * jnp.concatenate(s_rows, axis=0)

    pc = jnp.dot(x, w_in_ref[:, OFF_C:OFF_G], preferred_element_type=F32)
    for group_refs, active in (((qp_ref, kp_ref, vp_ref), is_prompt),
                               ((qs_ref, ks_ref, vs_ref), jnp.logical_not(is_prompt))):
        @pl.when(active)
        def _(group_refs=group_refs):
            for j, ref in enumerate(group_refs):
                ref[...] = pc[:, j * ATT_DIM:(j + 1) * ATT_DIM]

    ga = _sigmoid(jnp.dot(x, w_in_ref[:, OFF_G:OFF_G + D_MODEL], preferred_element_type=F32))
    mab = ga * _dot(ya, wua_ref[...])
    gb = _sigmoid(jnp.dot(x, w_in_ref[:, OFF_G + D_MODEL:OFF_G + 2 * D_MODEL], preferred_element_type=F32))
    mab_ref[...] = (mab + gb * _dot(yb, wub_ref[...])).astype(BF16)
    gc_ref[...] = _sigmoid(jnp.dot(x, w_in_ref[:, OFF_G + 2 * D_MODEL:], preferred_element_type=F32)).astype(BF16)


def _inproj_call(l, x_p, x_s, w_in, conv_w, sg_ln_g, sg_ln_b, wsp2, bsp2, w_up_a, w_up_b, s1, s2, seq, dec_seq):
    n_prompt, n_sample = x_p.shape[0], x_s.shape[0]
    n = n_prompt + n_sample
    n_ptiles = n_prompt // TM
    n_stiles = n_sample // TM
    tiles_per_seq = seq // TM
    batch = n_prompt // seq

    def stile(i):
        return jnp.maximum(i - n_ptiles, 0)

    def mode(i):
        return jnp.where(i < n_ptiles, 0, 1)

    row_spec = lambda w: pl.BlockSpec((TM, w), lambda i: (i, 0))
    srow_spec = lambda w: pl.BlockSpec((TM, w), lambda i: (stile(i), 0))
    prow_spec = lambda w: pl.BlockSpec((TM, w), lambda i: (jnp.minimum(i, n_ptiles - 1), 0))
    kern = functools.partial(_inproj_kernel, n_ptiles, tiles_per_seq, dec_seq)
    return pl.pallas_call(
        kern,
        grid=(n_ptiles + n_stiles,),
        in_specs=[
            prow_spec(D_MODEL), srow_spec(D_MODEL),
            pl.BlockSpec((None, D_MODEL, IN_DIM), lambda i: (l, 0, 0), pipeline_mode=pl.Buffered(1)),
            pl.BlockSpec((None, CONV_W, CONV_DIM), lambda i: (l, 0, 0)),
            pl.BlockSpec((None, 1, SG_DIM), lambda i: (l, 0, 0)),
            pl.BlockSpec((None, 1, SG_DIM), lambda i: (l, 0, 0)),
            pl.BlockSpec((None, None, SG_GROUPS, CHUNK, CHUNK), lambda i: (l, mode(i), 0, 0, 0)),
            pl.BlockSpec((None, None, CHUNK, SG_DIM), lambda i: (l, mode(i), 0, 0)),
            pl.BlockSpec((None, CONV_DIM, D_MODEL), lambda i: (l, 0, 0)),
            pl.BlockSpec((None, SG_DIM, D_MODEL), lambda i: (l, 0, 0)),
            srow_spec(CONV_DIM),
            srow_spec(CONV_DIM),
        ],
        out_specs=[
            prow_spec(ATT_DIM), prow_spec(ATT_DIM), prow_spec(ATT_DIM),
            srow_spec(ATT_DIM), srow_spec(ATT_DIM), srow_spec(ATT_DIM),
            row_spec(D_MODEL), row_spec(D_MODEL),
            pl.BlockSpec((1, SUBLANES, CONV_DIM),
                         lambda i: (jnp.minimum(i // tiles_per_seq, batch - 1), 0, 0)),
            srow_spec(CONV_DIM),
            srow_spec(SG_DIM),
        ],
        out_shape=[jax.ShapeDtypeStruct((n_prompt, ATT_DIM), F32)] * 3 + [jax.ShapeDtypeStruct((n_sample, ATT_DIM), F32)] * 3 + [
            jax.ShapeDtypeStruct((n, D_MODEL), BF16), jax.ShapeDtypeStruct((n, D_MODEL), BF16),
            jax.ShapeDtypeStruct((batch, SUBLANES, CONV_DIM), F32),
            jax.ShapeDtypeStruct((n_sample, CONV_DIM), F32),
            jax.ShapeDtypeStruct((n_sample, SG_DIM), F32),
        ],
        scratch_shapes=[pltpu.VMEM((SUBLANES, CONV_DIM), F32)],
        compiler_params=_params(1),
        name="inproj_mix",
    )(x_p, x_s, w_in, conv_w, sg_ln_g, sg_ln_b, wsp2, bsp2, w_up_a, w_up_b, s1, s2)


def _head_masks(rows):
    lane_head = lax.broadcasted_iota(jnp.int32, (rows, LANES), 1) // HEAD_DIM
    return [lane_head == h for h in range(HPG)]


def _stack_heads(qb, masks):
    return jnp.concatenate([jnp.where(m, qb, 0.0) for m in masks], axis=0)


def _unstack_heads(stacked, masks, r):
    out = None
    for h, m in enumerate(masks):
        part = jnp.where(m, stacked[h * r:(h + 1) * r], 0.0)
        out = part if out is None else out + part
    return out


def _mix_groups(outs, lses):
    mx = functools.reduce(jnp.maximum, lses)
    ws = [jnp.exp(ls - mx) for ls in lses]
    num = functools.reduce(lambda a, b: a + b, [w * o for w, o in zip(ws, outs)])
    return num / functools.reduce(lambda a, b: a + b, ws)


def _bias_table(rel_ref, start_ref, gi, h, steps):
    def body(b, tab):
        return jnp.where(steps >= start_ref[gi * NUM_BUCKETS + b], rel_ref[b * (N_ATT * HPG) + gi * HPG + h], tab)

    tab = lax.fori_loop(0, NUM_BUCKETS, body, jnp.zeros(steps.shape, F32))
    return jnp.where(jnp.logical_and(steps >= 0, steps < N_KEYS), tab, NEG)


def _bias_tables_kernel(dec_seq, pasts, rel_ref, start_ref, tp_ref, tc0, tc1, tc2, tn_ref):
    tcs = (tc0, tc1, tc2)
    qi = lax.broadcasted_iota(jnp.int32, (Q_BLK, 2 * Q_BLK), 0)
    kc = lax.broadcasted_iota(jnp.int32, (Q_BLK, 2 * Q_BLK), 1)
    for gi, (_, dil) in enumerate(ATT_GROUPS):
        past = pasts[gi]
        shift = dil.bit_length() - 1

        def steps(dist, dil=dil, shift=shift):
            return jnp.where((dist & (dil - 1)) == 0, dist >> shift, -1)

        qi_c = lax.broadcasted_iota(jnp.int32, (dec_seq, past), 0)
        row_c = lax.broadcasted_iota(jnp.int32, (dec_seq, past), 1)
        qi_n = lax.broadcasted_iota(jnp.int32, (dec_seq, dec_seq), 0)
        row_n = lax.broadcasted_iota(jnp.int32, (dec_seq, dec_seq), 1)
        for h in range(HPG):
            table = functools.partial(_bias_table, rel_ref, start_ref, gi, h)
            tp_ref[gi, h * Q_BLK:(h + 1) * Q_BLK, :] = table(qi + Q_BLK - kc)
            tcs[gi][h * dec_seq:(h + 1) * dec_seq, :] = table(steps(past + qi_c - row_c))
            tn_ref[gi, h * dec_seq:(h + 1) * dec_seq, :] = table(steps(qi_n - row_n))


def _bias_tables_call(rel_bias, pasts, dec_seq):
    shapes = ([(N_ATT, HPG * Q_BLK, 2 * Q_BLK)] + [(HPG * dec_seq, p) for p in pasts]
              + [(N_ATT, HPG * dec_seq, dec_seq)])
    return pl.pallas_call(
        functools.partial(_bias_tables_kernel, dec_seq, pasts),
        in_specs=[pl.BlockSpec(memory_space=pltpu.SMEM), pl.BlockSpec(memory_space=pltpu.SMEM)],
        out_shape=[jax.ShapeDtypeStruct(s, F32) for s in shapes],
        compiler_params=pltpu.CompilerParams(vmem_limit_bytes=VMEM_LIMIT),
        name="bias_tables",
    )(rel_bias.astype(F32).reshape(-1), _bucket_starts())


def _attn_prompt_kernel(seq, tab_ref, q_ref, k_ref, v_ref, o_ref, *scr):
    g = pl.program_id(1)
    o_scr, l_scr = scr[:N_ATT], scr[N_ATT:]
    masks = _head_masks(Q_BLK)
    scale = HEAD_DIM ** -0.5
    col = lax.broadcasted_iota(jnp.int32, (HPG * Q_BLK, 2 * Q_BLK), 1)

    def run_group(gi, dil):
        rows_per_class = seq // dil
        n_blk = rows_per_class // Q_BLK

        def ld(ref, start):
            if dil > 1:
                return ref[pl.ds(start, Q_BLK, stride=dil), :]
            return ref[pl.ds(start, Q_BLK), :]

        def st(ref, start, val):
            if dil > 1:
                ref[pl.ds(start, Q_BLK, stride=dil), :] = val
            else:
                ref[pl.ds(start, Q_BLK), :] = val

        def block(it, carry):
            r = it // n_blk
            mb = it % n_blk
            cur = r + dil * Q_BLK * mb
            prev = r + dil * Q_BLK * jnp.maximum(mb - 1, 0)
            qb = ld(q_ref, cur) * scale
            kw = jnp.concatenate([ld(k_ref, prev), ld(k_ref, cur)], axis=0)
            vw = jnp.concatenate([ld(v_ref, prev), ld(v_ref, cur)], axis=0)
            s = _dot_nt(_stack_heads(qb, masks), kw) + tab_ref[...]
            s = jnp.where(jnp.logical_or(col >= Q_BLK, mb > 0), s, NEG)
            m = jnp.max(s, axis=1, keepdims=True)
            p = jnp.exp(s - m)
            den = jnp.sum(p, axis=1, keepdims=True)
            pv = _dot(p, vw)
            st(o_scr[gi], cur, _unstack_heads(pv / den, masks, Q_BLK))
            st(l_scr[gi], cur, _unstack_heads(m + jnp.log(den), masks, Q_BLK))
            return carry

        lax.fori_loop(0, dil * n_blk, block, 0, unroll=4)

    for gi, (_, dil) in enumerate(ATT_GROUPS):
        @pl.when(g == gi)
        def _(gi=gi, dil=dil):
            run_group(gi, dil)

    @pl.when(g == N_ATT - 1)
    def _():
        o_ref[...] = _mix_groups([s[...] for s in o_scr], [s[...] for s in l_scr])


def _attn_prompt_call(q, k, v, table, batch, seq):
    blk = lambda: pl.BlockSpec((seq, LANES), lambda b, g: (b, g))
    return pl.pallas_call(
        functools.partial(_attn_prompt_kernel, seq),
        grid=(batch, N_ATT),
        in_specs=[pl.BlockSpec((None, HPG * Q_BLK, 2 * Q_BLK), lambda b, g: (g, 0, 0)), blk(), blk(), blk()],
        out_specs=pl.BlockSpec((seq, ATT_OUT), lambda b, g: (b, 0)),
        out_shape=jax.ShapeDtypeStruct((batch * seq, ATT_OUT), F32),
        scratch_shapes=[pltpu.VMEM((seq, LANES), F32)] * (2 * N_ATT),
        compiler_params=_params(2),
        name="attn_prompt",
    )(table, q, k, v)


def _attn_sample_kernel(dec_seq, q_ref, k_ref, v_ref, kc0, vc0, kc1, vc1, kc2, vc2, tc0, tc1, tc2, tn_ref, o_ref):
    kcs, vcs, tcs = (kc0, kc1, kc2), (vc0, vc1, vc2), (tc0, tc1, tc2)
    masks = _head_masks(dec_seq)
    scale = HEAD_DIM ** -0.5

    for s_i in range(SEQ_BLK):
        rows = slice(s_i * dec_seq, (s_i + 1) * dec_seq)
        outs, lses = [], []
        for gi in range(N_ATT):
            cols = slice(gi * LANES, (gi + 1) * LANES)
            qs = _stack_heads(q_ref[rows, cols] * scale, masks)
            k_new, v_new = k_ref[rows, cols], v_ref[rows, cols]
            s_c = _dot(qs, kcs[gi][s_i]) + tcs[gi][...]
            s_n = _dot_nt(qs, k_new) + tn_ref[gi]
            m = jnp.maximum(jnp.max(s_c, axis=1, keepdims=True), jnp.max(s_n, axis=1, keepdims=True))
            p_c = jnp.exp(s_c - m)
            p_n = jnp.exp(s_n - m)
            den = jnp.sum(p_c, axis=1, keepdims=True) + jnp.sum(p_n, axis=1, keepdims=True)
            pv = _dot_nt(p_c, vcs[gi][s_i]) + _dot(p_n, v_new)
            outs.append(_unstack_heads(pv / den, masks, dec_seq))
            lses.append(_unstack_heads(m + jnp.log(den), masks, dec_seq))
        o_ref[rows, :] = _mix_groups(outs, lses)


def _attn_sample_call(l, q, k, v, caches_t, tables, dec_batch, dec_seq):
    rows = SEQ_BLK * dec_seq
    new_spec = pl.BlockSpec((rows, ATT_DIM), lambda i: (i, 0))
    cache_specs = [pl.BlockSpec((None, SEQ_BLK, LANES, c.shape[3]), lambda i: (l, i, 0, 0)) for c in caches_t]
    table_specs = [pl.BlockSpec(t.shape, lambda i, nd=t.ndim: (0,) * nd) for t in tables]
    return pl.pallas_call(
        functools.partial(_attn_sample_kernel, dec_seq),
        grid=(dec_batch // SEQ_BLK,),
        in_specs=[new_spec, new_spec, new_spec] + cache_specs + table_specs,
        out_specs=pl.BlockSpec((rows, ATT_OUT), lambda i: (i, 0)),
        out_shape=jax.ShapeDtypeStruct((dec_batch * dec_seq, ATT_OUT), F32),
        compiler_params=_params(1),
        name="attn_sample",
    )(q, k, v, *caches_t, *tables)


def _merge_kernel(alpha, n_ptiles, xp_ref, xs_ref, mab_ref, gc_ref, ycp_ref, ycs_ref, wuc_ref, wo_ref, g_ref, b_ref,
                  wrt_ref, br_ref, x1g_ref, w_ref, slot_ref, tcnt_ref, tbase_ref, cnt_scr):
    @pl.when(pl.program_id(0) == 0)
    def _():
        cnt_scr[...] = jnp.zeros_like(cnt_scr)

    is_prompt = pl.program_id(0) < n_ptiles
    x = jnp.where(is_prompt, xp_ref[...], xs_ref[...])
    yc = jnp.where(is_prompt, ycp_ref[...], ycs_ref[...])
    merged = mab_ref[...].astype(F32) + gc_ref[...].astype(F32) * _dot(yc, wuc_ref[...])
    x1 = _layer_norm(alpha * x + _dot(merged, wo_ref[...]), g_ref[...], b_ref[...])
    for c in range(LANE_CHUNKS):
        x1g_ref[pl.ds(c, TM_R, stride=LANE_CHUNKS), :] = x1[:, c * LANES:(c + 1) * LANES]

    wr = wrt_ref[...]
    wr_hi, x1_hi = wr.astype(BF16), x1.astype(BF16)
    wr_lo, x1_lo = (wr - wr_hi.astype(F32)).astype(BF16), (x1 - x1_hi.astype(F32)).astype(BF16)
    logits = _dot_nt(wr_hi, x1_hi) + (_dot_nt(wr_lo, x1_hi) + _dot_nt(wr_hi, x1_lo)) + br_ref[...]
    eio = lax.broadcasted_iota(jnp.int32, (N_EXPERTS, TM_R), 0)
    vals, idxs = [], []
    for _ in range(TOP_K):
        mv = jnp.max(logits, axis=0, keepdims=True)
        ix = jnp.min(jnp.where(logits == mv, eio, N_EXPERTS), axis=0, keepdims=True)
        vals.append(mv)
        idxs.append(ix)
        logits = jnp.where(eio == ix, -jnp.inf, logits)
    tv = jnp.concatenate(vals, axis=0)
    e = jnp.exp(tv - tv[0:1, :])
    w_ref[...] = e / jnp.sum(e, axis=0, keepdims=True)

    onehots = [eio == ix for ix in idxs]
    chosen = functools.reduce(lambda a, c: a + c, [jnp.where(oh, 1.0, 0.0) for oh in onehots])
    earlier = (lax.broadcasted_iota(jnp.int32, (TM_R, TM_R), 0) < lax.broadcasted_iota(jnp.int32, (TM_R, TM_R), 1))
    within = _dot(chosen, jnp.where(earlier, 1.0, 0.0))
    tile_cnt = jnp.broadcast_to(jnp.sum(chosen, axis=1, keepdims=True), (N_EXPERTS, LANES))
    lower = (lax.broadcasted_iota(jnp.int32, (N_EXPERTS, N_EXPERTS), 1)
             < lax.broadcasted_iota(jnp.int32, (N_EXPERTS, N_EXPERTS), 0))
    place = within + jnp.dot(jnp.where(lower, 1.0, 0.0), tile_cnt, precision=lax.Precision.HIGHEST,
                             preferred_element_type=F32)[:, 0:1]
    slot_ref[...] = jnp.concatenate(
        [jnp.sum(jnp.where(oh, place, 0.0), axis=0, keepdims=True) for oh in onehots],
        axis=0).astype(jnp.int32) * LANE_CHUNKS
    tcnt_ref[...] = tile_cnt.astype(jnp.int32)
    tbase_ref[...] = cnt_scr[...].astype(jnp.int32)
    cnt_scr[...] = cnt_scr[...] + tile_cnt


def _merge_call(l, alpha, x_p, x_s, mab, gc, yc_p, yc_s, w_up_c, w_o, ln_g, ln_b, w_router_t, b_router):
    n_ptiles = x_p.shape[0] // TM_R
    n = x_p.shape[0] + x_s.shape[0]
    row_spec = lambda w: pl.BlockSpec((TM_R, w), lambda i: (i, 0))
    prow_spec = lambda w: pl.BlockSpec((TM_R, w), lambda i: (jnp.minimum(i, n_ptiles - 1), 0))
    srow_spec = lambda w: pl.BlockSpec((TM_R, w), lambda i: (jnp.maximum(i - n_ptiles, 0), 0))
    vec_spec = pl.BlockSpec((None, 1, D_MODEL), lambda i: (l, 0, 0))
    return pl.pallas_call(
        functools.partial(_merge_kernel, alpha, n_ptiles),
        grid=(n // TM_R,),
        in_specs=[
            prow_spec(D_MODEL), srow_spec(D_MODEL), row_spec(D_MODEL), row_spec(D_MODEL),
            prow_spec(ATT_OUT), srow_spec(ATT_OUT),
            pl.BlockSpec((None, ATT_OUT, D_MODEL), lambda i: (l, 0, 0)),
            pl.BlockSpec((None, D_MODEL, D_MODEL), lambda i: (l, 0, 0)),
            vec_spec, vec_spec,
            pl.BlockSpec((None, N_EXPERTS, D_MODEL), lambda i: (l, 0, 0)),
            pl.BlockSpec((None, N_EXPERTS, 1), lambda i: (l, 0, 0)),
        ],
        out_specs=[
            pl.BlockSpec((TM_R * LANE_CHUNKS, LANES), lambda i: (i, 0)),
            pl.BlockSpec((TOP_K, TM_R), lambda i: (0, i)),
            pl.BlockSpec((TOP_K, TM_R), lambda i: (0, i)),
            pl.BlockSpec((None, N_EXPERTS, LANES), lambda i: (i, 0, 0)),
            pl.BlockSpec((None, N_EXPERTS, LANES), lambda i: (i, 0, 0)),
        ],
        out_shape=[
            jax.ShapeDtypeStruct((n * LANE_CHUNKS, LANES), F32),
            jax.ShapeDtypeStruct((TOP_K, n), F32),
            jax.ShapeDtypeStruct((TOP_K, n), jnp.int32),
            jax.ShapeDtypeStruct((n // TM_R, N_EXPERTS, LANES), jnp.int32),
            jax.ShapeDtypeStruct((n // TM_R, N_EXPERTS, LANES), jnp.int32),
        ],
        scratch_shapes=[pltpu.VMEM((N_EXPERTS, LANES), F32)],
        compiler_params=_params(1),
        name="merge_ln1_router",
    )(x_p, x_s, mab, gc, yc_p, yc_s, w_up_c, w_o, ln_g, ln_b, w_router_t, b_router)


def _destride(ref, rows):
    return jnp.concatenate([ref[pl.ds(c, rows, stride=LANE_CHUNKS), :] for c in range(LANE_CHUNKS)], axis=1)


def _slab_rows(ref, row, count=1):
    return ref.at[pl.ds(pl.multiple_of(row, LANE_CHUNKS), count * LANE_CHUNKS), :]


def _start_expert_blocks(tile, tcnt_ref, hstart_ref, make_copy):
    big_bits = BIG_RUN.bit_length() - 1

    def per_expert(e, staged):
        cnt = tcnt_ref[tile * N_EXPERTS + e]
        first = hstart_ref[tile * N_EXPERTS + e]
        n_big = cnt >> big_bits

        def big(j, carry):
            make_copy(staged + j * BIG_RUN, first + j * BIG_RUN, BIG_RUN).start()
            return carry

        lax.fori_loop(0, n_big, big, 0)
        for bit in reversed(range(big_bits)):
            size = 1 << bit
            done = (cnt >> (bit + 1)) << (bit + 1)

            @pl.when((cnt & size) != 0)
            def _(size=size, done=done):
                make_copy(staged + done, first + done, size).start()
        return staged + cnt

    lax.fori_loop(0, N_EXPERTS, per_expert, 0)


def _dispatch_kernel(n_tiles, lt_ref, has_ref, nv_ref, tcnt_ref, hstart_ref, slot_ref, x1g_ref, xs_hbm,
                     zbuf, stg, zsem, sem):
    tile_rows = TM_E * LANE_CHUNKS
    i = pl.program_id(0)
    n_steps = pl.num_programs(0)
    buf = i % 2
    staged_all = lambda b: pltpu.make_async_copy(stg.at[b], xs_hbm.at[pl.ds(0, TM_R * TOP_K * LANE_CHUNKS), :],
                                                 sem.at[b])

    @pl.when(i == 0)
    def _():
        zbuf[...] = jnp.zeros_like(zbuf)
        zero_copy = lambda t: pltpu.make_async_copy(zbuf, xs_hbm.at[pl.ds(t * tile_rows, tile_rows), :], zsem)
        for e in range(N_EXPERTS):
            @pl.when(has_ref[e] > 0)
            def _(e=e):
                zero_copy(lt_ref[e]).start()
        lax.fori_loop(nv_ref[0], n_tiles, lambda t, c: (zero_copy(t).start(), c)[1], 0)
        for e in range(N_EXPERTS):
            @pl.when(has_ref[e] > 0)
            def _(e=e):
                zero_copy(lt_ref[e]).wait()
        lax.fori_loop(nv_ref[0], n_tiles, lambda t, c: (zero_copy(t).wait(), c)[1], 0)

    def step(b):
        @pl.when(i >= 2)
        def _():
            staged_all(b).wait()

        def place(t, carry):
            slab = _slab_rows(x1g_ref, t * LANE_CHUNKS)[...]
            for kk in range(TOP_K):
                _slab_rows(stg.at[b], slot_ref[0, 0, t * TOP_K + kk])[...] = slab
            return carry

        lax.fori_loop(0, TM_R, place, 0, unroll=DMA_UNROLL)
        _start_expert_blocks(i, tcnt_ref, hstart_ref, lambda staged, row, size: pltpu.make_async_copy(
            _slab_rows(stg.at[b], staged * LANE_CHUNKS, size), _slab_rows(xs_hbm, row * LANE_CHUNKS, size),
            sem.at[b]))

        @pl.when(i == n_steps - 1)
        def _():
            staged_all(b).wait()

        @pl.when(jnp.logical_and(i == n_steps - 1, i >= 1))
        def _():
            staged_all(1 - b).wait()

    for b in range(2):
        @pl.when(buf == b)
        def _(b=b):
            step(b)


def _dispatch_call(last_tile, has_rows, n_valid, tcnt, hstart, slot, x1g, n_tiles):
    n = slot.shape[0] * TM_R
    grid_spec = pltpu.PrefetchScalarGridSpec(
        num_scalar_prefetch=5,
        grid=(n // TM_R,),
        in_specs=[
            pl.BlockSpec((1, 1, TM_R * TOP_K), lambda i, *_: (i, 0, 0), memory_space=pltpu.SMEM),
            pl.BlockSpec((TM_R * LANE_CHUNKS, LANES), lambda i, *_: (i, 0)),
        ],
        out_specs=pl.BlockSpec(memory_space=pl.ANY),
        scratch_shapes=[pltpu.VMEM((TM_E * LANE_CHUNKS, LANES), F32),
                        pltpu.VMEM((2, TM_R * TOP_K * LANE_CHUNKS, LANES), F32),
                        pltpu.SemaphoreType.DMA(()), pltpu.SemaphoreType.DMA((2,))],
    )
    return pl.pallas_call(
        functools.partial(_dispatch_kernel, n_tiles),
        grid_spec=grid_spec,
        out_shape=jax.ShapeDtypeStruct((n_tiles * TM_E * LANE_CHUNKS, LANES), F32),
        compiler_params=_params(1),
        name="moe_dispatch",
    )(last_tile, has_rows, n_valid, tcnt, hstart, slot, x1g)


def _moe_ffn_kernel(te_ref, nv_ref, wslot_ref, xs_ref, wg_ref, bg_ref, wu_ref, bu_ref, wd_ref, bd_ref, ys_ref, w_b16):
    s = pl.program_id(0)
    t = s - W_WARMUP
    last = nv_ref[0] - 1
    expert_at = lambda tile: te_ref[jnp.clip(tile, 0, last)]

    for j, w_ref in enumerate((wg_ref, wu_ref, wd_ref)):
        arriving = expert_at(t + W_AHEAD[j])

        @pl.when(jnp.logical_or(s == 0, arriving != expert_at(t + W_AHEAD[j] - 1)))
        def _(j=j, w_ref=w_ref, arriving=arriving):
            w_b16[wslot_ref[arriving], j] = w_ref[...].astype(BF16)

    @pl.when(jnp.logical_and(t >= 0, t <= last))
    def _():
        slot = wslot_ref[te_ref[jnp.maximum(t, 0)]]
        x = _destride(xs_ref, TM_E).astype(BF16)
        gl = jnp.minimum(jnp.dot(x, w_b16[slot, 0], preferred_element_type=F32) + bg_ref[...], SWIGLU_LIMIT)
        ul = jnp.clip(jnp.dot(x, w_b16[slot, 1], preferred_element_type=F32) + bu_ref[...],
                      -SWIGLU_LIMIT, SWIGLU_LIMIT)
        hid = gl * _sigmoid(SWIGLU_ALPHA * gl) * (ul + 1.0)
        y = jnp.dot(hid.astype(BF16), w_b16[slot, 2], preferred_element_type=F32) + bd_ref[...]
        for c in range(LANE_CHUNKS):
            ys_ref[pl.ds(c, TM_E, stride=LANE_CHUNKS), :] = y[:, c * LANES:(c + 1) * LANES]


def _moe_ffn_call(l, tile_expert, n_valid, weight_slot, xs, w_gate, b_gate, w_up, b_up, w_down, b_down):
    n_tiles = tile_expert.shape[0]
    tile = lambda s, nv, ahead=0: jnp.clip(s - W_WARMUP + ahead, 0, nv[0] - 1)
    mat_spec = lambda ahead: pl.BlockSpec((None, None, D_MODEL, D_MODEL),
                                          lambda s, te, nv, ws: (l, te[tile(s, nv, ahead)], 0, 0))
    vec_spec = pl.BlockSpec((None, None, 1, D_MODEL), lambda s, te, nv, ws: (l, te[tile(s, nv)], 0, 0))
    grid_spec = pltpu.PrefetchScalarGridSpec(
        num_scalar_prefetch=3,
        grid=(n_tiles + W_WARMUP,),
        in_specs=[pl.BlockSpec((TM_E * LANE_CHUNKS, LANES), lambda s, te, nv, ws: (tile(s, nv), 0)),
                  mat_spec(W_AHEAD[0]), vec_spec, mat_spec(W_AHEAD[1]), vec_spec, mat_spec(W_AHEAD[2]), vec_spec],
        out_specs=pl.BlockSpec((TM_E * LANE_CHUNKS, LANES), lambda s, te, nv, ws: (tile(s, nv), 0)),
        scratch_shapes=[pltpu.VMEM((W_SLOTS, 3, D_MODEL, D_MODEL), BF16)],
    )
    return pl.pallas_call(
        _moe_ffn_kernel,
        grid_spec=grid_spec,
        out_shape=jax.ShapeDtypeStruct((n_tiles * TM_E * LANE_CHUNKS, LANES), F32),
        input_output_aliases={3: 0},
        compiler_params=_params(1),
        name="moe_ffn",
    )(tile_expert, n_valid, weight_slot, xs, w_gate, b_gate, w_up, b_up, w_down, b_down)


def _combine_kernel(alpha, n_ptiles, tcnt_ref, hstart_ref, slot_ref, w_ref, x1g_ref, ys_hbm, g_ref, b_ref,
                    x2p_ref, x2s_ref, stg, mixed, sem):
    i = pl.program_id(0)
    n_steps = pl.num_programs(0)
    buf = i % 2

    def fetch(tile, b):
        _start_expert_blocks(tile, tcnt_ref, hstart_ref, lambda staged, row, size: pltpu.make_async_copy(
            _slab_rows(ys_hbm, row * LANE_CHUNKS, size), _slab_rows(stg.at[b], staged * LANE_CHUNKS, size),
            sem.at[b]))

    def step(b):
        @pl.when(i == 0)
        def _():
            fetch(0, b)

        @pl.when(i + 1 < n_steps)
        def _():
            fetch(i + 1, 1 - b)

        pltpu.make_async_copy(ys_hbm.at[pl.ds(0, TM_R * TOP_K * LANE_CHUNKS), :], stg.at[b], sem.at[b]).wait()

        def mix(t, carry):
            c0 = t * TOP_K
            acc = w_ref[0, 0, c0] * _slab_rows(stg.at[b], slot_ref[0, 0, c0])[...]
            for kk in range(1, TOP_K):
                acc = acc + w_ref[0, 0, c0 + kk] * _slab_rows(stg.at[b], slot_ref[0, 0, c0 + kk])[...]
            _slab_rows(mixed, t * LANE_CHUNKS)[...] = acc
            return carry

        lax.fori_loop(0, TM_R, mix, 0, unroll=DMA_UNROLL)

    for b in range(2):
        @pl.when(buf == b)
        def _(b=b):
            step(b)

    x2 = _layer_norm(alpha * _destride(x1g_ref, TM_R) + _destride(mixed, TM_R), g_ref[...], b_ref[...])

    @pl.when(i < n_ptiles)
    def _():
        x2p_ref[...] = x2

    @pl.when(i >= n_ptiles)
    def _():
        x2s_ref[...] = x2


def _combine_call(l, alpha, tcnt, hstart, slot, top_w, x1g, ys, ln_g, ln_b, n_prompt):
    n = x1g.shape[0] // LANE_CHUNKS
    n_ptiles = n_prompt // TM_R
    vec_spec = pl.BlockSpec((None, 1, D_MODEL), lambda i, *_: (l, 0, 0))
    choice_spec = pl.BlockSpec((1, 1, TM_R * TOP_K), lambda i, *_: (i, 0, 0), memory_space=pltpu.SMEM)
    grid_spec = pltpu.PrefetchScalarGridSpec(
        num_scalar_prefetch=2,
        grid=(n // TM_R,),
        in_specs=[
            choice_spec, choice_spec,
            pl.BlockSpec((TM_R * LANE_CHUNKS, LANES), lambda i, *_: (i, 0)),
            pl.BlockSpec(memory_space=pl.ANY),
            vec_spec, vec_spec,
        ],
        out_specs=[pl.BlockSpec((TM_R, D_MODEL), lambda i, *_: (jnp.minimum(i, n_ptiles - 1), 0)),
                   pl.BlockSpec((TM_R, D_MODEL), lambda i, *_: (jnp.maximum(i - n_ptiles, 0), 0))],
        scratch_shapes=[pltpu.VMEM((2, TM_R * TOP_K * LANE_CHUNKS, LANES), F32),
                        pltpu.VMEM((TM_R * LANE_CHUNKS, LANES), F32),
                        pltpu.SemaphoreType.DMA((2,))],
    )
    return pl.pallas_call(
        functools.partial(_combine_kernel, alpha, n_ptiles),
        grid_spec=grid_spec,
        out_shape=[jax.ShapeDtypeStruct((n_prompt, D_MODEL), F32),
                   jax.ShapeDtypeStruct((n - n_prompt, D_MODEL), F32)],
        compiler_params=_params(1),
        name="moe_combine_ln2",
    )(tcnt, hstart, slot, top_w, x1g, ys, ln_g, ln_b)


def _route(tcnt, tbase, n_tiles):
    experts = jnp.arange(N_EXPERTS, dtype=jnp.int32)
    counts = tbase[-1] + tcnt[-1]
    padded = ((counts + TM_E - 1) // TM_E) * TM_E
    ends = jnp.cumsum(padded)
    hstart = (ends - padded)[None, :] + tbase
    n_valid = ends[-1] // TM_E
    last_expert = jnp.max(jnp.where(counts > 0, experts, 0))
    tile_start = jnp.arange(n_tiles, dtype=jnp.int32) * TM_E
    tile_expert = jnp.minimum(jnp.sum(ends[None, :] <= tile_start[:, None], axis=1), last_expert)
    has = (counts > 0).astype(jnp.int32)
    weight_slot = (jnp.cumsum(has) - has) % W_SLOTS
    return (tile_expert.astype(jnp.int32), n_valid.astype(jnp.int32).reshape(1), weight_slot.astype(jnp.int32),
            tcnt.reshape(-1), hstart.astype(jnp.int32).reshape(-1), (ends // TM_E - 1).astype(jnp.int32), has)


def _t5_bucket(dist):
    max_exact = NUM_BUCKETS // 2
    distf = jnp.maximum(dist, 1).astype(F32)
    large = max_exact + (jnp.log(distf / max_exact) / math.log(MAX_DISTANCE / max_exact)
                         * (NUM_BUCKETS - max_exact)).astype(jnp.int32)
    large = jnp.minimum(large, NUM_BUCKETS - 1)
    return jnp.where(dist < max_exact, dist, large)


def _bucket_starts():
    j = jnp.arange(N_KEYS, dtype=jnp.int32)
    b = jnp.arange(NUM_BUCKETS, dtype=jnp.int32)
    return jnp.stack([jnp.sum(_t5_bucket(dil * j)[None, :] < b[:, None], axis=1) for _, dil in ATT_GROUPS],
                     axis=0).astype(jnp.int32).reshape(-1)


def _spatial_tables(w_sp, b_sp, dec_seq):
    depth = w_sp.shape[0]
    tril = jnp.tril(jnp.ones((CHUNK, CHUNK), bool))
    wp = jnp.where(tril, w_sp, 0.0)
    ws_small = jnp.where(tril[:dec_seq, :dec_seq], w_sp[:, :, :dec_seq, :dec_seq], 0.0)
    eye = jnp.eye(CHUNK // dec_seq, dtype=w_sp.dtype)
    ws = jnp.einsum("ab,lgij->lgaibj", eye, ws_small).reshape(depth, SG_GROUPS, CHUNK, CHUNK)
    bp = jnp.repeat(jnp.swapaxes(b_sp, 1, 2), SG_HEAD, axis=2)
    bs = jnp.tile(bp[:, :dec_seq], (1, CHUNK // dec_seq, 1))
    return jnp.stack([wp, ws], axis=1).astype(BF16), jnp.stack([bp, bs], axis=1)


def kernel(x_prompt, x_sample, state_conv, cache_k_w128, cache_v_w128, cache_k_w512, cache_v_w512,
           cache_k_w2048, cache_v_w2048, w_in, conv_w, sg_ln_g, sg_ln_b, w_sp, b_sp, rel_bias,
           w_up_a, w_up_b, w_up_c, w_o, ln1_g, ln1_b, w_router, b_router, w_gate, b_gate, w_up, b_up,
           w_down, b_down, ln2_g, ln2_b):
    batch, seq, _ = x_prompt.shape
    dec_batch, dec_seq, _ = x_sample.shape
    depth = w_in.shape[0]
    n_prompt, n_sample = batch * seq, dec_batch * dec_seq
    n = n_prompt + n_sample
    alpha = (2 * depth) ** 0.25
    assert seq % TM == 0 and n_sample % TM == 0 and dec_seq & (dec_seq - 1) == 0 and CHUNK % dec_seq == 0
    assert n_prompt % TM_R == 0 and n_sample % TM_R == 0
    assert all(seq % (dil * Q_BLK) == 0 and dil & (dil - 1) == 0 for _, dil in ATT_GROUPS)
    assert dec_batch % SEQ_BLK == 0
    n_tiles = (n * TOP_K) // TM_E + N_EXPERTS

    caches_t = [jnp.transpose(c, (0, 1, 3, 4, 2)).reshape(c.shape[0], c.shape[1], LANES, c.shape[2]) for c in
                (cache_k_w128, cache_v_w128, cache_k_w512, cache_v_w512, cache_k_w2048, cache_v_w2048)]
    pasts = tuple(caches_t[2 * gi].shape[3] for gi in range(N_ATT))
    prompt_table, *sample_tables = _bias_tables_call(rel_bias, pasts, dec_seq)
    wsp2, bsp2 = _spatial_tables(w_sp, b_sp, dec_seq)
    w_in_b, w_up_a_b, w_up_b_b, w_up_c_b, w_o_b = (w.astype(BF16) for w in (w_in, w_up_a, w_up_b, w_up_c, w_o))
    vec = lambda a: a.reshape(depth, 1, a.shape[-1])
    w_router_t = jnp.swapaxes(w_router, 1, 2)
    b_router_c = b_router.reshape(depth, N_EXPERTS, 1)
    expert_vec = lambda a: a.reshape(depth, N_EXPERTS, 1, a.shape[-1])

    x_p, x_s = x_prompt.reshape(n_prompt, D_MODEL), x_sample.reshape(n_sample, D_MODEL)
    outs ={name: [] for name in ("conv_p", "conv_s", "sgv", "kp", "vp", "ks", "vs")}
    for l in range(depth):
        s1 = jnp.zeros((dec_batch, dec_seq, CONV_DIM), F32).at[:, 0].set(state_conv[l, :, 1])
        s2 = jnp.zeros((dec_batch, dec_seq, CONV_DIM), F32).at[:, 0].set(state_conv[l, :, 0])
        s2 = s2.at[:, 1].set(state_conv[l, :, 1])
        q_p, k_p, v_p, q_s, k_s, v_s, mab, gc, tail, z_s, v_rows = _inproj_call(
            l, x_p, x_s, w_in_b, conv_w, vec(sg_ln_g), vec(sg_ln_b), wsp2, bsp2, w_up_a_b, w_up_b_b,
            s1.reshape(n_sample, CONV_DIM), s2.reshape(n_sample, CONV_DIM), seq, dec_seq)
        yc_p = _attn_prompt_call(q_p, k_p, v_p, prompt_table, batch, seq)
        yc_s = _attn_sample_call(l, q_s, k_s, v_s, caches_t, sample_tables, dec_batch, dec_seq)
        x1g, top_w, slot, tcnt, tbase = _merge_call(
            l, alpha, x_p, x_s, mab, gc, yc_p, yc_s, w_up_c_b, w_o_b, vec(ln1_g), vec(ln1_b), w_router_t, b_router_c)
        tile_expert, n_valid, weight_slot, tcnt, hstart, last_tile, has_rows = _route(
            tcnt[:, :, 0], tbase[:, :, 0], n_tiles)
        by_tile = lambda a: a.reshape(TOP_K, n // TM_R, TM_R).transpose(1, 2, 0).reshape(n // TM_R, 1, TM_R * TOP_K)
        slot, top_w = by_tile(slot), by_tile(top_w)
        xs = _dispatch_call(last_tile, has_rows, n_valid, tcnt, hstart, slot, x1g, n_tiles)
        ys = _moe_ffn_call(l, tile_expert, n_valid, weight_slot, xs, w_gate, expert_vec(b_gate),
                           w_up, expert_vec(b_up), w_down, expert_vec(b_down))
        x_p, x_s = _combine_call(l, alpha, tcnt, hstart, slot, top_w, x1g, ys, vec(ln2_g), vec(ln2_b), n_prompt)

        outs["conv_p"].append(tail[:, SUBLANES - (CONV_W - 1):])
        outs["conv_s"].append(z_s.reshape(dec_batch, dec_seq, CONV_DIM)[:, dec_seq - (CONV_W - 1):])
        outs["sgv"].append(v_rows.reshape(dec_batch, dec_seq, SG_DIM))
        kp4 = k_p.reshape(batch, seq, N_ATT, HPG, HEAD_DIM)
        vp4 = v_p.reshape(batch, seq, N_ATT, HPG, HEAD_DIM)
        ks4 = k_s.reshape(dec_batch, dec_seq, N_ATT, HPG, HEAD_DIM)
        vs4 = v_s.reshape(dec_batch, dec_seq, N_ATT, HPG, HEAD_DIM)
        outs["kp"].append([kp4[:, seq - min(win, seq):, gi] for gi, (win, _) in enumerate(ATT_GROUPS)])
        outs["vp"].append([vp4[:, seq - min(win, seq):, gi] for gi, (win, _) in enumerate(ATT_GROUPS)])
        outs["ks"].append([ks4[:, dec_seq - min(win, dec_seq):, gi] for gi, (win, _) in enumerate(ATT_GROUPS)])
        outs["vs"].append([vs4[:, dec_seq - min(win, dec_seq):, gi] for gi, (win, _) in enumerate(ATT_GROUPS)])

    stack = lambda name: jnp.stack(outs[name], axis=0)
    per_group = lambda name, gi: jnp.stack([layer[gi] for layer in outs[name]], axis=0)
    result = [x_p.reshape(batch, seq, D_MODEL), x_s.reshape(dec_batch, dec_seq, D_MODEL),
              stack("conv_p"), stack("conv_s")]
    for gi in range(N_ATT):
        result += [per_group("kp", gi), per_group("vp", gi)]
    for gi in range(N_ATT):
        result += [per_group("ks", gi), per_group("vs", gi)]
    result.append(stack("sgv"))
    return tuple(result)
```
